```python
import math
import jax
import jax.numpy as jnp
from jax import lax
import numpy as np


D_MODEL = 2048
BATCH = 2
SEQ = 8192
DEPTH = 2
DEC_BATCH = 8
DEC_SEQ = 32
PAST_LEN = 4096

CHUNK = 64
Q_BLOCK = 128
N_EVEN = (DEPTH + 1) // 2
N_ODD = DEPTH // 2
H_A = 8
DK_A = 128
DV_A = 128
W_A = H_A * DV_A
CONV_W = 4
H_B = 8
DK_B = 128
DV_B = 128
W_B = H_B * DV_B
H_C = 8
DH_C = 128
W_C = H_C * DH_C
GC_D = 16
G_D = 64
P_D = 64
W_D = G_D * GC_D
EV_SIZES = (3 * W_A, W_A, H_A, H_A, W_B, W_B, W_B, W_B, W_B, H_B, H_B)
EV_IN = sum(EV_SIZES)
EV_OUT = W_A + W_B
OD_SIZES = (W_C, W_C, W_C, W_C, W_D, W_D)
OD_IN = sum(OD_SIZES)
OD_OUT = W_C + W_D
DEEPNORM_ALPHA = (2 * DEPTH) ** 0.25
DEEPNORM_BETA = (8 * DEPTH) ** -0.25
LN_EPS = 1e-5
NORM_EPS = 1e-6

kernel_name = 'hybrid_streaming_gdn_mlstm_sb_s5_step'


def _split(y, sizes):
    idx = [int(i) for i in np.cumsum(sizes)[:-1]]
    return jnp.split(y, idx, axis=-1)


def _layer_norm(x, g, b):
    xf = x.astype(jnp.float32)
    mu = jnp.mean(xf, -1, keepdims=True)
    var = jnp.mean(jnp.square(xf - mu), -1, keepdims=True)
    return ((xf - mu) * lax.rsqrt(var + LN_EPS) * g + b).astype(x.dtype)


def _head_rms_norm(x, g):
    return x * lax.rsqrt(jnp.mean(x * x, -1, keepdims=True) + NORM_EPS) * g


def _l2norm(x):
    return x * lax.rsqrt(jnp.sum(x * x, -1, keepdims=True) + NORM_EPS)


def _causal_conv(x_ext, w):
    t = x_ext.shape[1] - (CONV_W - 1)
    return sum(x_ext[:, i:i + t] * w[i] for i in range(CONV_W))


def _to_chunks(x, L):
    b, t, h = x.shape[:3]
    x = x.reshape((b, t // L, L, h) + x.shape[3:])
    return jnp.moveaxis(x, (1, 3), (0, 2))


def _from_chunks(x):
    x = jnp.moveaxis(x, (0, 2), (1, 3))
    return x.reshape((x.shape[0], x.shape[1] * x.shape[2]) + x.shape[3:])


def _gated_delta_chunked(q, k, v, beta, g, S0):
    L = min(CHUNK, q.shape[1])
    qc, kc, vc = _to_chunks(q, L), _to_chunks(k, L), _to_chunks(v, L)
    bc = _to_chunks(beta, L)
    gc = jnp.cumsum(_to_chunks(g, L), axis=-1)
    tri_incl = jnp.tril(jnp.ones((L, L), bool))
    tri_strict = jnp.tril(jnp.ones((L, L), bool), -1)
    decay = jnp.exp(jnp.where(tri_incl, gc[..., :, None] - gc[..., None, :], -jnp.inf))
    kk = jnp.einsum('nbhtd,nbhsd->nbhts', kc, kc)
    m_mat = jnp.where(tri_strict, bc[..., None] * kk * decay, 0.0)
    eye = jnp.eye(L, dtype=q.dtype)
    gam = jnp.exp(gc)
    rhs = jnp.concatenate([vc * bc[..., None], kc * (bc * gam)[..., None]], axis=-1)
    sol = lax.linalg.triangular_solve(eye + m_mat, rhs, left_side=True, lower=True)
    u, w = sol[..., :DV_A], sol[..., DV_A:]
    qk = jnp.einsum('nbhtd,nbhsd->nbhts', qc, kc) * decay
    q_dec = qc * gam[..., None]
    g_last = gc[..., -1]
    k_dec = kc * jnp.exp(g_last[..., None] - gc)[..., None]

    def step(S, xs):
        u_, w_, qk_, qd_, kd_, gl_ = xs
        e = u_ - jnp.einsum('bhtk,bhkv->bhtv', w_, S)
        o = jnp.einsum('bhtk,bhkv->bhtv', qd_, S) + jnp.einsum('bhts,bhsv->bhtv', qk_, e)
        S = jnp.exp(gl_)[..., None, None] * S + jnp.einsum('bhsk,bhsv->bhkv', kd_, e)
        return S, o

    S_fin, o = lax.scan(step, S0, (u, w, qk, q_dec, k_dec, g_last))
    return _from_chunks(o), S_fin


def _mlstm_chunked(q, k, v, ig, logf, C0, n0, m0):
    L = min(CHUNK, q.shape[1])
    qc, kc, vc = _to_chunks(q, L), _to_chunks(k, L), _to_chunks(v, L)
    igc = _to_chunks(ig, L)
    bcum = jnp.cumsum(_to_chunks(logf, L), axis=-1)
    tri = jnp.tril(jnp.ones((L, L), bool))
    d_log = jnp.where(tri, bcum[..., :, None] - bcum[..., None, :] + igc[..., None, :], -jnp.inf)
    m_intra = jnp.max(d_log, axis=-1)
    qk = jnp.einsum('nbhtd,nbhsd->nbhts', qc, kc)

    def step(carry, xs):
        C, n, m = carry
        q_, k_, v_, b_, dl_, mi_, qk_ = xs
        m_t = jnp.maximum(b_ + m[..., None], mi_)
        p = jnp.exp(dl_ - m_t[..., None])
        inter = jnp.exp(b_ + m[..., None] - m_t)
        pqk = p * qk_
        num = inter[..., None] * jnp.einsum('bhtk,bhkv->bhtv', q_, C) + jnp.einsum('bhts,bhsv->bhtv', pqk, v_)
        den = inter * jnp.einsum('bhtk,bhk->bht', q_, n) + jnp.sum(pqk, axis=-1)
        h = num / jnp.maximum(jnp.abs(den), jnp.exp(-m_t))[..., None]
        p_last, i_last = p[..., -1, :], inter[..., -1]
        C = i_last[..., None, None] * C + jnp.einsum('bhs,bhsk,bhsv->bhkv', p_last, k_, v_)
        n = i_last[..., None] * n + jnp.einsum('bhs,bhsk->bhk', p_last, k_)
        return (C, n, m_t[..., -1]), h

    (C, n, m), h = lax.scan(step, (C0, n0, m0), (qc, kc, vc, bcum, d_log, m_intra, qk))
    return _from_chunks(h), C, n, m


def _sb_block(qb, q_pos, k, v, k_pos):
    z = jnp.einsum('bqhd,bkhd->bhqk', qb, k) * (DH_C ** -0.5)
    causal = k_pos[None, :] < q_pos[:, None]
    l1m = jnp.where(causal, jax.nn.log_sigmoid(-z), 0.0)
    between = lax.cumsum(l1m, axis=3, reverse=True) - l1m
    att = jnp.where(causal, jnp.exp(jax.nn.log_sigmoid(z) + between), 0.0)
    return jnp.einsum('bhqk,bkhd->bqhd', att, v)


def _sb_prompt(q, k, v):
    b, t, h, d = q.shape
    nblk = t // Q_BLOCK
    k_pos = jnp.arange(t)
    qb = jnp.moveaxis(q.reshape(b, nblk, Q_BLOCK, h, d), 1, 0)
    pos = jnp.arange(t).reshape(nblk, Q_BLOCK)
    o = lax.map(lambda a: _sb_block(a[0], a[1], k, v, k_pos), (qb, pos))
    return jnp.moveaxis(o, 0, 1).reshape(b, t, h, d)


def _sb_sample(q, k, v, k_cache, v_cache):
    past = k_cache.shape[1]
    t = q.shape[1]
    k_all = jnp.concatenate([k_cache.astype(jnp.float32), k], axis=1)
    v_all = jnp.concatenate([v_cache.astype(jnp.float32), v], axis=1)
    return _sb_block(q, past + jnp.arange(t), k_all, v_all, jnp.arange(past + t))


def _cplx_combine(e1, e2):
    a1r, a1i, b1r, b1i = e1
    a2r, a2i, b2r, b2i = e2
    return (a2r * a1r - a2i * a1i, a2r * a1i + a2i * a1r,
            a2r * b1r - a2i * b1i + b2r, a2r * b1i + a2i * b1r + b2i)


def _s5(u, h_re0, h_im0, lam_re, lam_im, b_re, b_im, c_re, c_im, d, log_dt):
    bsz, t, _ = u.shape
    dt = jnp.exp(log_dt)[:, None]
    mag = jnp.exp(lam_re * dt)
    a_re, a_im = mag * jnp.cos(lam_im * dt), mag * jnp.sin(lam_im * dt)
    den = lam_re * lam_re + lam_im * lam_im
    f_re = ((a_re - 1.0) * lam_re + a_im * lam_im) / den
    f_im = (a_im * lam_re - (a_re - 1.0) * lam_im) / den
    bb_re = f_re[..., None] * b_re - f_im[..., None] * b_im
    bb_im = f_re[..., None] * b_im + f_im[..., None] * b_re
    ug = u.reshape(bsz, t, G_D, GC_D)
    bu_re = jnp.einsum('btgc,gpc->btgp', ug, bb_re)
    bu_im = jnp.einsum('btgc,gpc->btgp', ug, bb_im)
    bu_re = bu_re.at[:, 0].add(a_re * h_re0 - a_im * h_im0)
    bu_im = bu_im.at[:, 0].add(a_re * h_im0 + a_im * h_re0)
    ar = jnp.broadcast_to(a_re, (1, t, G_D, P_D))
    ai = jnp.broadcast_to(a_im, (1, t, G_D, P_D))
    _, _, h_re, h_im = lax.associative_scan(_cplx_combine, (ar, ai, bu_re, bu_im), axis=1)
    y = jnp.einsum('btgp,gcp->btgc', h_re, c_re) - jnp.einsum('btgp,gcp->btgc', h_im, c_im)
    y = y.reshape(bsz, t, W_D) + d * u
    return y, h_re[:, -1], h_im[:, -1]


def _even_layer(x, S0, conv0, C0, n0, m0, w_in, conv_w, a_log, dt_bias, norm_a,
                ig_bias, fg_bias, norm_b, w_out, ln_g, ln_b):
    f32 = jnp.float32
    bsz, t, _ = x.shape
    proj = jnp.einsum('btd,de->bte', x, w_in).astype(f32)
    qkv_a, z_a, b_a, a_a, q_b, k_b, v_b, o_b, z_b, i_b, f_b = _split(proj, EV_SIZES)
    conv_in = jnp.concatenate([conv0.astype(f32), qkv_a], axis=1)
    conv_new = conv_in[:, -(CONV_W - 1):]
    qkv = jax.nn.silu(_causal_conv(conv_in, conv_w.astype(f32)))
    q_a, k_a, v_a = jnp.split(qkv, 3, axis=-1)
    q_a = _l2norm(q_a.reshape(bsz, t, H_A, DK_A)) * (DK_A ** -0.5)
    k_a = _l2norm(k_a.reshape(bsz, t, H_A, DK_A))
    v_a = v_a.reshape(bsz, t, H_A, DV_A)
    beta = jax.nn.sigmoid(b_a)
    g = -jnp.exp(a_log.astype(f32)) * jax.nn.softplus(a_a + dt_bias)
    o_a, S_new = _gated_delta_chunked(q_a, k_a, v_a, beta, g, S0.astype(f32))
    y_a = _head_rms_norm(o_a, norm_a).reshape(bsz, t, W_A) * jax.nn.silu(z_a)
    qh = q_b.reshape(bsz, t, H_B, DK_B)
    kh = k_b.reshape(bsz, t, H_B, DK_B) * (DK_B ** -0.5)
    vh = v_b.reshape(bsz, t, H_B, DV_B)
    h, C_new, n_new, m_new = _mlstm_chunked(qh, kh, vh, i_b + ig_bias, jax.nn.log_sigmoid(f_b + fg_bias),
                                            C0.astype(f32), n0.astype(f32), m0.astype(f32))
    h = jax.nn.sigmoid(o_b).reshape(bsz, t, H_B, DV_B) * h
    y_b = _head_rms_norm(h, norm_b.reshape(H_B, DV_B)).reshape(bsz, t, W_B) * jax.nn.silu(z_b)
    out = jnp.einsum('bte,ed->btd', jnp.concatenate([y_a, y_b], axis=-1).astype(x.dtype), w_out)
    x = _layer_norm(DEEPNORM_ALPHA * x + out, ln_g, ln_b)
    dt_ = x.dtype
    return x, S_new.astype(dt_), conv_new.astype(dt_), C_new.astype(dt_), n_new.astype(dt_), m_new.astype(dt_)


def _odd_layer(x, k_cache, v_cache, h_re0, h_im0, w_in, lam_re, lam_im, b_re, b_im, c_re, c_im,
               d, log_dt, w_glu, b_glu, w_out, ln_g, ln_b):
    f32 = jnp.float32
    bsz, t, _ = x.shape
    proj = jnp.einsum('btd,de->bte', x, w_in).astype(f32)
    q_c, k_c, v_c, z_c, u_d, z_d = _split(proj, OD_SIZES)
    q = q_c.reshape(bsz, t, H_C, DH_C)
    k = k_c.reshape(bsz, t, H_C, DH_C)
    v = v_c.reshape(bsz, t, H_C, DH_C)
    if k_cache is None:
        o_c = _sb_prompt(q, k, v)
    else:
        o_c = _sb_sample(q, k, v, k_cache, v_cache)
    y_c = o_c.reshape(bsz, t, W_C) * jax.nn.silu(z_c)
    y_s, h_re, h_im = _s5(u_d, h_re0.astype(f32), h_im0.astype(f32), lam_re, lam_im, b_re, b_im,
                          c_re, c_im, d, log_dt)
    y_s = jax.nn.gelu(y_s)
    y_s = y_s * jax.nn.sigmoid(jnp.einsum('bte,ef->btf', y_s, w_glu.astype(f32)) + b_glu)
    y_d = y_s * jax.nn.silu(z_d)
    out = jnp.einsum('bte,ed->btd', jnp.concatenate([y_c, y_d], axis=-1).astype(x.dtype), w_out)
    x = _layer_norm(DEEPNORM_ALPHA * x + out, ln_g, ln_b)
    dt_ = x.dtype
    return x, k.astype(dt_), v.astype(dt_), h_re.astype(dt_), h_im.astype(dt_)


def setup_inputs(seed: int = 0) -> dict:
    key = jax.random.key(seed)
    keys = list(jax.random.split(key, 40))
    f32 = jnp.float32

    def nrm(i, shape, scale):
        return jax.random.normal(keys[i], shape, f32) * scale

    def uni(i, shape, lo, hi):
        return jax.random.uniform(keys[i], shape, f32, lo, hi)

    dt_a = jnp.exp(uni(10, (N_EVEN, H_A), math.log(1e-3), math.log(1e-1)))
    n_idx = jnp.arange(P_D, dtype=f32)
    return {
        'x_prompt': nrm(0, (BATCH, SEQ, D_MODEL), 1.0),
        'x_sample': nrm(1, (DEC_BATCH, DEC_SEQ, D_MODEL), 1.0),
        'state_delta_S': nrm(2, (N_EVEN, DEC_BATCH, H_A, DK_A, DV_A), 0.1),
        'state_delta_conv': nrm(3, (N_EVEN, DEC_BATCH, CONV_W - 1, 3 * W_A), 1.0),
        'state_mlstm_C': nrm(4, (N_EVEN, DEC_BATCH, H_B, DK_B, DV_B), 0.1),
        'state_mlstm_n': nrm(5, (N_EVEN, DEC_BATCH, H_B, DK_B), 0.1),
        'state_mlstm_m': nrm(6, (N_EVEN, DEC_BATCH, H_B), 1.0),
        'cache_sb_k': nrm(7, (N_ODD, DEC_BATCH, PAST_LEN, H_C, DH_C), 1.0),
        'cache_sb_v': nrm(8, (N_ODD, DEC_BATCH, PAST_LEN, H_C, DH_C), 1.0),
        'state_s5_re': nrm(9, (N_ODD, DEC_BATCH, G_D, P_D), 0.1),
        'state_s5_im': nrm(11, (N_ODD, DEC_BATCH, G_D, P_D), 0.1),
        'ev_w_in': nrm(12, (N_EVEN, D_MODEL, EV_IN), D_MODEL ** -0.5),
        'ev_conv_w': nrm(13, (N_EVEN, CONV_W, 3 * W_A), CONV_W ** -0.5),
        'ev_a_log': jnp.log(uni(14, (N_EVEN, H_A), 1.0, 16.0)),
        'ev_dt_bias': dt_a + jnp.log(-jnp.expm1(-dt_a)),
        'ev_norm_a': 1.0 + nrm(15, (N_EVEN, DV_A), 0.01),
        'ev_ig_bias': nrm(16, (N_EVEN, H_B), 0.1),
        'ev_fg_bias': 3.0 + nrm(17, (N_EVEN, H_B), 0.5),
        'ev_norm_b': 1.0 + nrm(18, (N_EVEN, W_B), 0.01),
        'ev_w_out': nrm(19, (N_EVEN, EV_OUT, D_MODEL), EV_OUT ** -0.5 * DEEPNORM_BETA),
        'ev_ln_g': 1.0 + nrm(20, (N_EVEN, D_MODEL), 0.01),
        'ev_ln_b': nrm(21, (N_EVEN, D_MODEL), 0.01),
        'od_w_in': nrm(22, (N_ODD, D_MODEL, OD_IN), D_MODEL ** -0.5),
        'od_lam_re': -0.5 + nrm(23, (N_ODD, G_D, P_D), 0.01),
        'od_lam_im': math.pi * n_idx + nrm(24, (N_ODD, G_D, P_D), 0.01),
        'od_b_re': nrm(25, (N_ODD, G_D, P_D, GC_D), (2 * GC_D) ** -0.5),
        'od_b_im': nrm(26, (N_ODD, G_D, P_D, GC_D), (2 * GC_D) ** -0.5),
        'od_c_re': nrm(27, (N_ODD, G_D, GC_D, P_D), P_D ** -0.5),
        'od_c_im': nrm(28, (N_ODD, G_D, GC_D, P_D), P_D ** -0.5),
        'od_d': nrm(29, (N_ODD, W_D), 0.5),
        'od_log_dt': uni(30, (N_ODD, G_D), math.log(1e-3), math.log(1e-1)),
        'od_w_glu': nrm(31, (N_ODD, W_D, W_D), W_D ** -0.5),
        'od_b_glu': nrm(32, (N_ODD, W_D), 0.01),
        'od_w_out': nrm(33, (N_ODD, OD_OUT, D_MODEL), OD_OUT ** -0.5 * DEEPNORM_BETA),
        'od_ln_g': 1.0 + nrm(34, (N_ODD, D_MODEL), 0.01),
        'od_ln_b': nrm(35, (N_ODD, D_MODEL), 0.01),
    }


def reference(x_prompt, x_sample, state_delta_S, state_delta_conv, state_mlstm_C, state_mlstm_n,
              state_mlstm_m, cache_sb_k, cache_sb_v, state_s5_re, state_s5_im,
              ev_w_in, ev_conv_w, ev_a_log, ev_dt_bias, ev_norm_a, ev_ig_bias, ev_fg_bias, ev_norm_b,
              ev_w_out, ev_ln_g, ev_ln_b,
              od_w_in, od_lam_re, od_lam_im, od_b_re, od_b_im, od_c_re, od_c_im, od_d, od_log_dt,
              od_w_glu, od_b_glu, od_w_out, od_ln_g, od_ln_b):
    f32 = jnp.float32
    nb = x_prompt.shape[0]
    xp, xs = x_prompt, x_sample
    dS_p, dS_s, dc_p, dc_s, mC_p, mC_s, mn_p, mn_s, mm_p, mm_s = ([] for _ in range(10))
    sk_p, sk_s, sv_p, sv_s, sr_p, sr_s, si_p, si_s = ([] for _ in range(8))
    for layer in range(DEPTH):
        j = layer // 2
        if layer % 2 == 0:
            w = (ev_w_in[j], ev_conv_w[j], ev_a_log[j], ev_dt_bias[j], ev_norm_a[j], ev_ig_bias[j],
                 ev_fg_bias[j], ev_norm_b[j], ev_w_out[j], ev_ln_g[j], ev_ln_b[j])
            xp, S, cv, C, n, m = _even_layer(
                xp, jnp.zeros((nb, H_A, DK_A, DV_A), f32), jnp.zeros((nb, CONV_W - 1, 3 * W_A), f32),
                jnp.zeros((nb, H_B, DK_B, DV_B), f32), jnp.zeros((nb, H_B, DK_B), f32),
                jnp.zeros((nb, H_B), f32), *w)
            dS_p.append(S)
            dc_p.append(cv)
            mC_p.append(C)
            mn_p.append(n)
            mm_p.append(m)
            xs, S, cv, C, n, m = _even_layer(
                xs, state_delta_S[j], state_delta_conv[j], state_mlstm_C[j], state_mlstm_n[j],
                state_mlstm_m[j], *w)
            dS_s.append(S)
            dc_s.append(cv)
            mC_s.append(C)
            mn_s.append(n)
            mm_s.append(m)
        else:
            w = (od_w_in[j], od_lam_re[j], od_lam_im[j], od_b_re[j], od_b_im[j], od_c_re[j], od_c_im[j],
                 od_d[j], od_log_dt[j], od_w_glu[j], od_b_glu[j], od_w_out[j], od_ln_g[j], od_ln_b[j])
            xp, k, v, hr, hi = _odd_layer(xp, None, None, jnp.zeros((nb, G_D, P_D), f32),
                                          jnp.zeros((nb, G_D, P_D), f32), *w)
            sk_p.append(k)
            sv_p.append(v)
            sr_p.append(hr)
            si_p.append(hi)
            xs, k, v, hr, hi = _odd_layer(xs, cache_sb_k[j], cache_sb_v[j], state_s5_re[j], state_s5_im[j], *w)
            sk_s.append(k)
            sv_s.append(v)
            sr_s.append(hr)
            si_s.append(hi)
    return (xp, xs,
            jnp.stack(dS_p), jnp.stack(dS_s), jnp.stack(dc_p), jnp.stack(dc_s),
            jnp.stack(mC_p), jnp.stack(mC_s), jnp.stack(mn_p), jnp.stack(mn_s),
            jnp.stack(mm_p), jnp.stack(mm_s),
            jnp.stack(sk_p), jnp.stack(sk_s), jnp.stack(sv_p), jnp.stack(sv_s),
            jnp.stack(sr_p), jnp.stack(sr_s), jnp.stack(si_p), jnp.stack(si_s))
```

```python
import functools

import jax
import jax.numpy as jnp
from jax import lax
from jax.experimental import pallas as pl
from jax.experimental.pallas import tpu as pltpu

F32 = jnp.float32
BF16 = jnp.bfloat16

LANES = 128
SUBLANES = 8
CHUNK = 64
CONV_W = 4
H_A = 8
H_B = 8
H_C = 8
DK = 128
GC_D = 16
G_D = 64
P_D = 64
GB_D = 8
NGB_D = G_D // GB_D
W_HEADS = 1024
LN_EPS = 1e-5
NORM_EPS = 1e-6
VMEM_LIMIT = 48 * 1024 * 1024


def _cparams(sem):
    return pltpu.CompilerParams(dimension_semantics=sem, vmem_limit_bytes=VMEM_LIMIT)


def _sigmoid(x):
    return 1.0 / (1.0 + jnp.exp(-x))


def _silu(x):
    return x * _sigmoid(x)


def _softplus(x):
    return jnp.maximum(x, 0.0) + jnp.log1p(jnp.exp(-jnp.abs(x)))


def _dot(a, b):
    return jnp.dot(a, b, preferred_element_type=F32)


def _dot_nt(a, b):
    return lax.dot_general(a, b, (((1,), (1,)), ((), ())), preferred_element_type=F32)


def _dot_tn(a, b):
    return lax.dot_general(a, b, (((0,), (0,)), ((), ())), preferred_element_type=F32)


def _split3(x):
    hi = x.astype(BF16)
    r1 = x - hi.astype(F32)
    mid = r1.astype(BF16)
    lo = (r1 - mid.astype(F32)).astype(BF16)
    return hi, mid, lo


def _dot_mask_lhs(mask_bf16, x):
    hi, mid, lo = _split3(x)
    return _dot(mask_bf16, hi) + _dot(mask_bf16, mid) + _dot(mask_bf16, lo)


def _dot_mask_rhs(x, mask_bf16, pieces):
    parts = _split3(x)[:pieces]
    out = _dot(parts[0], mask_bf16)
    for p in parts[1:]:
        out = out + _dot(p, mask_bf16)
    return out


def _iota2(shape):
    return (lax.broadcasted_iota(jnp.int32, shape, 0), lax.broadcasted_iota(jnp.int32, shape, 1))


def _head_column(x, idx):
    lane = lax.broadcasted_iota(jnp.int32, x.shape, 1)
    return jnp.sum(jnp.where(lane == idx, x, 0.0), axis=1, keepdims=True)


def _proj_kernel(x_ref, w_ref, o_ref):
    o_ref[0] = _dot(x_ref[...].astype(BF16), w_ref[0])


def _proj(x2, w, tm, name):
    m, d = x2.shape
    ng, _, wn = w.shape
    return pl.pallas_call(
        _proj_kernel,
        grid=(m // tm, ng),
        in_specs=[pl.BlockSpec((tm, d), lambda i, j: (i, 0)),
                  pl.BlockSpec((1, d, wn), lambda i, j: (j, 0, 0))],
        out_specs=pl.BlockSpec((1, tm, wn), lambda i, j: (j, i, 0)),
        out_shape=jax.ShapeDtypeStruct((ng, m, wn), F32),
        compiler_params=_cparams(("parallel", "arbitrary")),
        name=name,
    )(x2, w)


def _outproj_ln_kernel(ya_ref, yb_ref, x_ref, wa_ref, wb_ref, g_ref, b_ref, o_ref, *, alpha):
    acc = _dot(ya_ref[...].astype(BF16), wa_ref[...]) + _dot(yb_ref[...].astype(BF16), wb_ref[...])
    r = alpha * x_ref[...] + acc
    mu = jnp.mean(r, axis=-1, keepdims=True)
    c = r - mu
    var = jnp.mean(c * c, axis=-1, keepdims=True)
    o_ref[...] = c * lax.rsqrt(var + LN_EPS) * g_ref[...] + b_ref[...]


def _outproj_ln(ya, yb, x2, w_out_bf16, ln_g, ln_b, alpha, tm, name):
    m, d = x2.shape
    wa = ya.shape[1]
    wb = yb.shape[1]
    return pl.pallas_call(
        functools.partial(_outproj_ln_kernel, alpha=alpha),
        grid=(m // tm,),
        in_specs=[pl.BlockSpec((tm, wa), lambda i: (i, 0)),
                  pl.BlockSpec((tm, wb), lambda i: (i, 0)),
                  pl.BlockSpec((tm, d), lambda i: (i, 0)),
                  pl.BlockSpec((wa, d), lambda i: (0, 0)),
                  pl.BlockSpec((wb, d), lambda i: (wa // wb, 0)),
                  pl.BlockSpec((1, d), lambda i: (0, 0)),
                  pl.BlockSpec((1, d), lambda i: (0, 0))],
        out_specs=pl.BlockSpec((tm, d), lambda i: (i, 0)),
        out_shape=jax.ShapeDtypeStruct((m, d), F32),
        compiler_params=_cparams(("parallel",)),
        name=name,
    )(ya, yb, x2, w_out_bf16, w_out_bf16, ln_g.reshape(1, d), ln_b.reshape(1, d))


def _conv_silu(x_ref, scr, w_ref, tb):
    scr[SUBLANES:SUBLANES + tb, :] = x_ref[...]
    w = w_ref[...]
    base = SUBLANES - (CONV_W - 1)
    acc = scr[base:base + tb, :] * w[0:1, :]
    for i in range(1, CONV_W):
        acc = acc + scr[base + i:base + i + tb, :] * w[i:i + 1, :]
    scr[0:SUBLANES, :] = scr[tb:tb + SUBLANES, :]
    return _silu(acc)


def _l2norm(x):
    return x * lax.rsqrt(jnp.sum(x * x, axis=-1, keepdims=True) + NORM_EPS)


def _rms_norm(x, g):
    return x * lax.rsqrt(jnp.mean(x * x, axis=-1, keepdims=True) + NORM_EPS) * g


def _unit_lower_inverse(n_mat, length):
    r, c = _iota2((length, length))
    x = jnp.where(r == c, 1.0, 0.0) + n_mat
    p = n_mat
    steps = length.bit_length() - 2
    for _ in range(steps):
        pb = p.astype(BF16)
        p = _dot(pb, pb)
        x = x + _dot(x.astype(BF16), p.astype(BF16))
    return x


def _gdn_kernel(qp_ref, kp_ref, vp_ref, z_ref, gt_ref, c0q_ref, c0k_ref, c0v_ref,
                cwq_ref, cwk_ref, cwv_ref, prm_ref, norm_ref, s0_ref,
                y_ref, sout_ref, s_scr, xq_scr, xk_scr, xv_scr, *, chunk, tb):
    h = pl.program_id(1)
    t = pl.program_id(2)

    @pl.when(t == 0)
    def _():
        s_scr[...] = s0_ref[...]
        xq_scr[0:SUBLANES, :] = c0q_ref[...]
        xk_scr[0:SUBLANES, :] = c0k_ref[...]
        xv_scr[0:SUBLANES, :] = c0v_ref[...]

    q = _l2norm(_conv_silu(qp_ref, xq_scr, cwq_ref, tb)) * (DK ** -0.5)
    k = _l2norm(_conv_silu(kp_ref, xk_scr, cwk_ref, tb))
    v = _conv_silu(vp_ref, xv_scr, cwv_ref, tb)

    gt = gt_ref[...]
    prm = prm_ref[...]
    beta_all = _sigmoid(gt)
    g_all = -jnp.exp(prm[0:1, :]) * _softplus(gt + prm[1:2, :])
    beta = _head_column(beta_all, h)
    g = _head_column(g_all, h + H_A)

    r, c = _iota2((chunk, chunk))
    incl = r >= c
    strict = r > c
    a_incl = jnp.where(incl, 1.0, 0.0).astype(BF16)
    b_gt = jnp.where(strict, 1.0, 0.0)
    ones_w = jnp.ones((chunk, LANES), F32)
    norm_g = norm_ref[...]
    z = z_ref[...]

    s_state = s_scr[...]
    for ci in range(tb // chunk):
        sl = slice(ci * chunk, (ci + 1) * chunk)
        qc, kc, vc = q[sl], k[sl], v[sl]
        bc, gcol = beta[sl], g[sl]
        e_mat = _dot_mask_lhs(a_incl, gcol * b_gt)
        gc = _dot_mask_lhs(a_incl, gcol * ones_w)
        decay = jnp.where(incl, jnp.exp(e_mat), 0.0)
        gam = jnp.exp(gc)
        g_last = gc[chunk - 1:chunk, :]
        kb = kc.astype(BF16)
        kk = _dot_nt(kb, kb)
        n_mat = jnp.where(strict, -(bc * kk * decay), 0.0)
        t_inv = _unit_lower_inverse(n_mat, chunk)
        rhs = jnp.concatenate([vc * bc, kc * (bc * gam)], axis=-1)
        sol = _dot(t_inv.astype(BF16), rhs.astype(BF16))
        u, w = sol[:, :DK], sol[:, DK:]
        qk = _dot_nt(qc.astype(BF16), kb) * decay
        q_dec = qc * gam
        k_dec = kc * jnp.exp(g_last - gc)
        sb = s_state.astype(BF16)
        e = u - _dot(w.astype(BF16), sb)
        eb = e.astype(BF16)
        o = _dot(q_dec.astype(BF16), sb) + _dot(qk.astype(BF16), eb)
        s_state = jnp.exp(g_last) * s_state + _dot_tn(k_dec.astype(BF16), eb)
        y_ref[sl, :] = _rms_norm(o, norm_g) * _silu(z[sl])
    s_scr[...] = s_state

    @pl.when(t == pl.num_programs(2) - 1)
    def _():
        sout_ref[...] = s_state


def _gdn(p5, g3, conv0p, conv_w, prm, norm_a, s0, chunk, tb, name):
    _, bx, t, _ = p5.shape
    nh = H_A

    def pspec(group):
        return pl.BlockSpec((None, None, tb, DK), lambda b, h, i, g=group: (g, b, i, h))

    def c0spec(group):
        return pl.BlockSpec((None, SUBLANES, DK), lambda b, h, i, g=group: (b, 0, g * nh + h))

    def cwspec(group):
        return pl.BlockSpec((CONV_W, DK), lambda b, h, i, g=group: (0, g * nh + h))

    return pl.pallas_call(
        functools.partial(_gdn_kernel, chunk=chunk, tb=tb),
        grid=(bx, nh, t // tb),
        in_specs=[pspec(0), pspec(1), pspec(2), pspec(3),
                  pl.BlockSpec((None, tb, LANES), lambda b, h, i: (b, i, 0)),
                  c0spec(0), c0spec(1), c0spec(2),
                  cwspec(0), cwspec(1), cwspec(2),
                  pl.BlockSpec((SUBLANES, LANES), lambda b, h, i: (0, 0)),
                  pl.BlockSpec((1, DK), lambda b, h, i: (0, 0)),
                  pl.BlockSpec((None, None, DK, DK), lambda b, h, i: (b, h, 0, 0))],
        out_specs=[pl.BlockSpec((None, tb, DK), lambda b, h, i: (b, i, h)),
                   pl.BlockSpec((None, None, DK, DK), lambda b, h, i: (b, h, 0, 0))],
        out_shape=[jax.ShapeDtypeStruct((bx, t, nh * DK), F32),
                   jax.ShapeDtypeStruct((bx, nh, DK, DK), F32)],
        scratch_shapes=[pltpu.VMEM((DK, DK), F32),
                        pltpu.VMEM((tb + SUBLANES, DK), F32),
                        pltpu.VMEM((tb + SUBLANES, DK), F32),
                        pltpu.VMEM((tb + SUBLANES, DK), F32)],
        compiler_params=_cparams(("parallel", "parallel", "arbitrary")),
        name=name,
    )(p5, p5, p5, p5, g3, conv0p, conv0p, conv0p, conv_w, conv_w, conv_w, prm, norm_a, s0)


def _mlstm_kernel(q_ref, k_ref, v_ref, o_ref, z_ref, gt_ref, prm_ref, norm_ref, c0_ref, n0_ref, m0_ref,
                  y_ref, cout_ref, nout_ref, mout_ref, c_scr, n_scr, m_scr, *, chunk, tb):
    h = pl.program_id(1)
    t = pl.program_id(2)

    @pl.when(t == 0)
    def _():
        c_scr[...] = c0_ref[...]
        n_scr[...] = n0_ref[...]
        m_scr[...] = m0_ref[...]

    q = q_ref[...]
    k = k_ref[...] * (DK ** -0.5)
    v = v_ref[...]
    gt = gt_ref[...]
    prm = prm_ref[...]
    ig_all = gt + prm[2:3, :]
    lf_all = -_softplus(-(gt + prm[3:4, :]))
    ig = _head_column(ig_all, h + 2 * H_A)
    lf = _head_column(lf_all, h + 2 * H_A + H_B)

    r, c = _iota2((chunk, chunk))
    incl = r >= c
    a_incl = jnp.where(incl, 1.0, 0.0).astype(BF16)
    b_gt = jnp.where(r > c, 1.0, 0.0)
    eye = jnp.where(r == c, 1.0, 0.0)
    ones_sq = jnp.ones((chunk, chunk), BF16)
    ones_w = jnp.ones((chunk, LANES), F32)
    norm_g = norm_ref[...]

    c_state = c_scr[...]
    n_state = n_scr[...]
    m_state = m_scr[...]
    for ci in range(tb // chunk):
        sl = slice(ci * chunk, (ci + 1) * chunk)
        qc, kc, vc = q[sl], k[sl], v[sl]
        igc, lfc = ig[sl], lf[sl]
        e_mat = _dot_mask_lhs(a_incl, lfc * b_gt)
        bcum_w = _dot_mask_lhs(a_incl, lfc * ones_w)
        ig_row = _dot_mask_lhs(ones_sq, igc * eye)
        bcum = bcum_w[:, 0:1]
        d_log = jnp.where(incl, e_mat + ig_row, -jnp.inf)
        m_intra = jnp.max(d_log, axis=-1, keepdims=True)
        qb = qc.astype(BF16)
        kb = kc.astype(BF16)
        vb = vc.astype(BF16)
        qk = _dot_nt(qb, kb)
        m_prev = m_state[:, 0:1]
        bm = bcum + m_prev
        m_t = jnp.maximum(bm, m_intra)
        p = jnp.exp(d_log - m_t)
        inter = jnp.exp(bm - m_t)
        pqk = p * qk
        num = inter * _dot(qb, c_state.astype(BF16)) + _dot(pqk.astype(BF16), vb)
        den = inter * jnp.sum(qc * n_state, axis=-1, keepdims=True) + jnp.sum(pqk, axis=-1, keepdims=True)
        hh = num / jnp.maximum(jnp.abs(den), jnp.exp(-m_t))
        m_last = m_t[chunk - 1:chunk, :]
        i_last = inter[chunk - 1:chunk, :]
        p_last = jnp.exp(bcum[chunk - 1:chunk, :] - bcum + igc - m_last)
        kp = kc * p_last
        c_state = i_last * c_state + _dot_tn(kp.astype(BF16), vb)
        n_state = i_last * n_state + jnp.sum(kp, axis=0, keepdims=True)
        m_state = m_last + jnp.zeros_like(m_state)
        hh = _sigmoid(o_ref[sl, :]) * hh
        y_ref[sl, :] = _rms_norm(hh, norm_g) * _silu(z_ref[sl, :])
    c_scr[...] = c_state
    n_scr[...] = n_state
    m_scr[...] = m_state

    @pl.when(t == pl.num_programs(2) - 1)
    def _():
        cout_ref[...] = c_state
        nout_ref[...] = n_state
        mout_ref[...] = m_state


def _mlstm(p5, g3, prm, norm_b, c0, n0, m0, chunk, tb, name):
    _, bx, t, _ = p5.shape
    nh = H_B

    def pspec(group):
        return pl.BlockSpec((None, None, tb, DK), lambda b, h, i, g=group: (g, b, i, h))

    row = pl.BlockSpec((None, None, 1, DK), lambda b, h, i: (b, h, 0, 0))
    return pl.pallas_call(
        functools.partial(_mlstm_kernel, chunk=chunk, tb=tb),
        grid=(bx, nh, t // tb),
        in_specs=[pspec(4), pspec(5), pspec(6), pspec(7), pspec(8),
                  pl.BlockSpec((None, tb, LANES), lambda b, h, i: (b, i, 0)),
                  pl.BlockSpec((SUBLANES, LANES), lambda b, h, i: (0, 0)),
                  pl.BlockSpec((1, DK), lambda b, h, i: (0, h)),
                  pl.BlockSpec((None, None, DK, DK), lambda b, h, i: (b, h, 0, 0)),
                  row, row],
        out_specs=[pl.BlockSpec((None, tb, DK), lambda b, h, i: (b, i, h)),
                   pl.BlockSpec((None, None, DK, DK), lambda b, h, i: (b, h, 0, 0)),
                   row, row],
        out_shape=[jax.ShapeDtypeStruct((bx, t, nh * DK), F32),
                   jax.ShapeDtypeStruct((bx, nh, DK, DK), F32),
                   jax.ShapeDtypeStruct((bx, nh, 1, DK), F32),
                   jax.ShapeDtypeStruct((bx, nh, 1, DK), F32)],
        scratch_shapes=[pltpu.VMEM((DK, DK), F32), pltpu.VMEM((1, DK), F32), pltpu.VMEM((1, DK), F32)],
        compiler_params=_cparams(("parallel", "parallel", "arbitrary")),
        name=name,
    )(p5, p5, p5, p5, p5, g3, prm, norm_b, c0, n0, m0)


def _sb_tile(qb, kt, vt, run, acc, u_incl, causal):
    z = _dot_nt(qb, kt.astype(BF16))
    sp = _softplus(z)
    l1m = -sp
    if causal is not None:
        l1m = jnp.where(causal, l1m, 0.0)
    incl = _dot_mask_rhs(l1m, u_incl, 2)
    att = jnp.exp((z - sp) + (incl - l1m) + run)
    if causal is not None:
        att = jnp.where(causal, att, 0.0)
    acc = acc + _dot(att.astype(BF16), vt.astype(BF16))
    return run + incl[:, 0:1], acc


def _sb_kernel(q_ref, kd_ref, vd_ref, kp_ref, vp_ref, z_ref, y_ref, *, tq, tk, prev_blocks_per_step, prev_blocks_fixed):
    i = pl.program_id(2)
    qb = (q_ref[...] * (DK ** -0.5)).astype(BF16)
    r, c = _iota2((tq, tq))
    rk, ck = _iota2((tk, tk))
    u_diag = jnp.where(r >= c, 1.0, 0.0).astype(BF16)
    u_prev = jnp.where(rk >= ck, 1.0, 0.0).astype(BF16)
    run = jnp.zeros((tq, 1), F32)
    acc = jnp.zeros((tq, DK), F32)
    run, acc = _sb_tile(qb, kd_ref[...], vd_ref[...], run, acc, u_diag, c < r)
    nprev = i * prev_blocks_per_step + prev_blocks_fixed

    def body(it, carry):
        run, acc = carry
        start = pl.multiple_of((nprev - 1 - it) * tk, tk)
        return _sb_tile(qb, kp_ref[pl.ds(start, tk), :], vp_ref[pl.ds(start, tk), :], run, acc, u_prev, None)

    run, acc = lax.fori_loop(0, nprev, body, (run, acc))
    y_ref[...] = acc * _silu(z_ref[...])


def _sb(p5, kprev, vprev, kgroup, vgroup, tq, tk, causal_prev, name):
    _, bx, t, _ = p5.shape
    tp = kprev.shape[2]

    def blk(group):
        return pl.BlockSpec((None, None, tq, DK), lambda b, h, i, g=group: (g, b, i, h))

    def full(group):
        return pl.BlockSpec((None, None, tp, DK), lambda b, h, i, g=group: (g, b, 0, h))

    kern = functools.partial(_sb_kernel, tq=tq, tk=tk,
                             prev_blocks_per_step=(tq // tk) if causal_prev else 0,
                             prev_blocks_fixed=0 if causal_prev else tp // tk)
    return pl.pallas_call(
        kern,
        grid=(bx, H_C, t // tq),
        in_specs=[blk(0), blk(1), blk(2), full(kgroup), full(vgroup), blk(3)],
        out_specs=pl.BlockSpec((None, tq, DK), lambda b, h, i: (b, i, h)),
        out_shape=jax.ShapeDtypeStruct((bx, t, H_C * DK), F32),
        compiler_params=_cparams(("parallel", "parallel", "arbitrary")),
        name=name,
    )(p5, p5, p5, kprev, vprev, p5)


def _s5_kernel(u_ref, bb_ref, cc_ref, ar_ref, ai_ref, d_ref, h0r_ref, h0i_ref,
               y_ref, hr_out, hi_out, hs, hr_scr, hi_scr, *, tt):
    t = pl.program_id(1)
    half = GB_D * P_D

    @pl.when(t == 0)
    def _():
        hr_scr[...] = h0r_ref[...]
        hi_scr[...] = h0i_ref[...]

    nl = half // LANES
    for gb in range(NGB_D):
        rows = pl.ds(gb, tt, stride=NGB_D)
        bu = _dot(u_ref[rows, :].astype(BF16), bb_ref[gb])
        for j in range(2 * nl):
            hs[j, rows, :] = bu[:, j * LANES:(j + 1) * LANES]

    ar = [ar_ref[:, j * LANES:(j + 1) * LANES] for j in range(nl)]
    ai = [ai_ref[:, j * LANES:(j + 1) * LANES] for j in range(nl)]

    def step(s, carry):
        rows = pl.ds(pl.multiple_of(s * NGB_D, NGB_D), NGB_D)
        new = [None] * (2 * nl)
        for j in range(nl):
            hr, hi = carry[j], carry[nl + j]
            new[j] = ar[j] * hr - ai[j] * hi + hs[j, rows, :]
            new[nl + j] = ar[j] * hi + ai[j] * hr + hs[nl + j, rows, :]
        for j in range(2 * nl):
            hs[j, rows, :] = new[j]
        return tuple(new)

    init = tuple(hr_scr[:, j * LANES:(j + 1) * LANES] for j in range(nl)) + tuple(
        hi_scr[:, j * LANES:(j + 1) * LANES] for j in range(nl))
    fin = lax.fori_loop(0, tt, step, init, unroll=8)
    hr = jnp.concatenate(fin[:nl], axis=1)
    hi = jnp.concatenate(fin[nl:], axis=1)
    hr_scr[...] = hr
    hi_scr[...] = hi

    for gb in range(NGB_D):
        rows = pl.ds(gb, tt, stride=NGB_D)
        hcat = jnp.concatenate([hs[j, rows, :] for j in range(2 * nl)], axis=1)
        y = _dot(hcat.astype(BF16), cc_ref[gb])
        y_ref[rows, :] = y + d_ref[gb:gb + 1, :] * u_ref[rows, :]

    @pl.when(t == pl.num_programs(1) - 1)
    def _():
        hr_out[...] = hr
        hi_out[...] = hi


def _s5(u8, ugroup, bb, cc, a_re, a_im, d8, h0r, h0i, tt, name):
    _, bx, t8, _ = u8.shape
    half = GB_D * P_D
    st = pl.BlockSpec((None, NGB_D, half), lambda b, i: (b, 0, 0))
    prm = pl.BlockSpec((NGB_D, half), lambda b, i: (0, 0))
    return pl.pallas_call(
        functools.partial(_s5_kernel, tt=tt),
        grid=(bx, t8 // (tt * NGB_D)),
        in_specs=[pl.BlockSpec((None, None, tt * NGB_D, LANES), lambda b, i: (ugroup, b, i, 0)),
                  pl.BlockSpec((NGB_D, LANES, 2 * half), lambda b, i: (0, 0, 0)),
                  pl.BlockSpec((NGB_D, 2 * half, LANES), lambda b, i: (0, 0, 0)),
                  prm, prm,
                  pl.BlockSpec((NGB_D, LANES), lambda b, i: (0, 0)),
                  st, st],
        out_specs=[pl.BlockSpec((None, tt * NGB_D, LANES), lambda b, i: (b, i, 0)), st, st],
        out_shape=[jax.ShapeDtypeStruct((bx, t8, LANES), F32),
                   jax.ShapeDtypeStruct((bx, NGB_D, half), F32),
                   jax.ShapeDtypeStruct((bx, NGB_D, half), F32)],
        scratch_shapes=[pltpu.VMEM((2 * half // LANES, tt * NGB_D, LANES), F32),
                        pltpu.VMEM((NGB_D, half), F32), pltpu.VMEM((NGB_D, half), F32)],
        compiler_params=_cparams(("parallel", "arbitrary")),
        name=name,
    )(u8, bb, cc, a_re, a_im, d8, h0r, h0i)


def _glu_kernel(y_ref, z_ref, w_ref, b_ref, o_ref):
    y = y_ref[...]
    ys = 0.5 * y * (1.0 + jnp.tanh(0.7978845608028654 * (y + 0.044715 * (y * y * y))))
    gate = _sigmoid(_dot(ys.astype(BF16), w_ref[...]) + b_ref[...])
    o_ref[...] = ys * gate * _silu(z_ref[...])


def _glu(y2, z3, zgroup, w_glu_bf16, b_glu, tm, name):
    m, w = y2.shape
    return pl.pallas_call(
        _glu_kernel,
        grid=(m // tm,),
        in_specs=[pl.BlockSpec((tm, w), lambda i: (i, 0)),
                  pl.BlockSpec((None, tm, w), lambda i: (zgroup, i, 0)),
                  pl.BlockSpec((w, w), lambda i: (0, 0)),
                  pl.BlockSpec((1, w), lambda i: (0, 0))],
        out_specs=pl.BlockSpec((tm, w), lambda i: (i, 0)),
        out_shape=jax.ShapeDtypeStruct((m, w), F32),
        compiler_params=_cparams(("parallel",)),
        name=name,
    )(y2, z3, w_glu_bf16, b_glu.reshape(1, w))


def _pack_even_weights(w_in, conv_w, a_log, dt_bias, norm_a, ig_bias, fg_bias, norm_b, w_out, ln_g, ln_b):
    d = w_in.shape[0]
    wa = H_A * DK
    wb = H_B * DK
    o = 0
    qkv_a = w_in[:, o:o + 3 * wa]; o += 3 * wa
    z_a = w_in[:, o:o + wa]; o += wa
    gate_a = w_in[:, o:o + 2 * H_A]; o += 2 * H_A
    main_b = w_in[:, o:o + 5 * wb]; o += 5 * wb
    gate_b = w_in[:, o:o + 2 * H_B]
    w_main = jnp.concatenate([qkv_a, z_a, main_b], axis=1).astype(BF16)
    w_main = w_main.reshape(d, 9, W_HEADS).transpose(1, 0, 2)
    n_gate = 2 * H_A + 2 * H_B
    w_gate = jnp.concatenate([gate_a, gate_b, jnp.zeros((d, LANES - n_gate), F32)], axis=1).astype(BF16)[None]
    prm = jnp.zeros((SUBLANES, LANES), F32)
    prm = prm.at[0, H_A:2 * H_A].set(a_log)
    prm = prm.at[1, H_A:2 * H_A].set(dt_bias)
    prm = prm.at[2, 2 * H_A:2 * H_A + H_B].set(ig_bias)
    prm = prm.at[3, 2 * H_A + H_B:2 * H_A + 2 * H_B].set(fg_bias)
    return dict(w_main=w_main, w_gate=w_gate, conv_w=conv_w, prm=prm, norm_a=norm_a.reshape(1, DK),
                norm_b=norm_b.reshape(1, wb), w_out=w_out.astype(BF16), ln_g=ln_g, ln_b=ln_b)


def _even_layer(x, s0, conv0, c0, n0, m0, wp, alpha, tm, tb):
    bx, t, d = x.shape
    m = bx * t
    chunk = min(CHUNK, t)
    x2 = x.reshape(m, d)
    p5 = _proj(x2, wp["w_main"], tm, "ev_proj").reshape(9, bx, t, W_HEADS)
    g3 = _proj(x2, wp["w_gate"], tm, "ev_gate_proj").reshape(bx, t, LANES)
    conv0p = jnp.concatenate([jnp.zeros((bx, SUBLANES - (CONV_W - 1), conv0.shape[2]), F32), conv0], axis=1)
    ya, s_new = _gdn(p5, g3, conv0p, wp["conv_w"], wp["prm"], wp["norm_a"], s0, chunk, tb, "gdn")
    m0w = jnp.broadcast_to(m0[:, :, None, None], (bx, H_B, 1, DK))
    yb, c_new, n_new, m_new = _mlstm(p5, g3, wp["prm"], wp["norm_b"], c0, n0[:, :, None, :], m0w, chunk, tb, "mlstm")
    xn = _outproj_ln(ya.reshape(m, -1), yb.reshape(m, -1), x2, wp["w_out"], wp["ln_g"], wp["ln_b"], alpha, min(tm, 256),
                     "ev_outproj_ln")
    conv_new = jnp.concatenate([p5[0, :, t - (CONV_W - 1):], p5[1, :, t - (CONV_W - 1):], p5[2, :, t - (CONV_W - 1):]],
                               axis=-1)
    return xn.reshape(bx, t, d), s_new, conv_new, c_new, n_new[:, :, 0, :], m_new[:, :, 0, 0]


def _pack_odd_weights(w_in, lam_re, lam_im, b_re, b_im, c_re, c_im, d, log_dt, w_glu, b_glu, w_out, ln_g, ln_b):
    dm = w_in.shape[0]
    w_main = w_in.astype(BF16).reshape(dm, 6, W_HEADS).transpose(1, 0, 2)
    dt = jnp.exp(log_dt)[:, None]
    mag = jnp.exp(lam_re * dt)
    a_re, a_im = mag * jnp.cos(lam_im * dt), mag * jnp.sin(lam_im * dt)
    den = lam_re * lam_re + lam_im * lam_im
    f_re = ((a_re - 1.0) * lam_re + a_im * lam_im) / den
    f_im = (a_im * lam_re - (a_re - 1.0) * lam_im) / den
    bb_re = f_re[..., None] * b_re - f_im[..., None] * b_im
    bb_im = f_re[..., None] * b_im + f_im[..., None] * b_re
    half = GB_D * P_D
    eye = jnp.eye(GB_D, dtype=F32)

    def blockdiag_in(bb):
        bbg = bb.reshape(NGB_D, GB_D, P_D, GC_D)
        return jnp.einsum("bgpc,gh->bgchp", bbg, eye).reshape(NGB_D, GB_D * GC_D, half)

    def blockdiag_out(cm):
        cg = cm.reshape(NGB_D, GB_D, GC_D, P_D)
        return jnp.einsum("bgcp,gh->bgphc", cg, eye).reshape(NGB_D, half, GB_D * GC_D)

    bb = jnp.concatenate([blockdiag_in(bb_re), blockdiag_in(bb_im)], axis=2).astype(BF16)
    cc = jnp.concatenate([blockdiag_out(c_re), -blockdiag_out(c_im)], axis=1).astype(BF16)
    return dict(w_main=w_main, bb=bb, cc=cc, a_re=a_re.reshape(NGB_D, half), a_im=a_im.reshape(NGB_D, half),
                d8=d.reshape(NGB_D, LANES), w_glu=w_glu.astype(BF16), b_glu=b_glu, w_out=w_out.astype(BF16),
                ln_g=ln_g, ln_b=ln_b)


def _odd_layer(x, k_cache, v_cache, h_re0, h_im0, wp, alpha, tm, tq, tt):
    bx, t, d = x.shape
    m = bx * t
    x2 = x.reshape(m, d)
    p3 = _proj(x2, wp["w_main"], tm, "od_proj")
    p5 = p3.reshape(6, bx, t, W_HEADS)
    if k_cache is None:
        yc = _sb(p5, p5, p5, 1, 2, tq, tq, True, "sb_prompt")
    else:
        past = k_cache.shape[1]
        yc = _sb(p5, k_cache.reshape(1, bx, past, -1), v_cache.reshape(1, bx, past, -1), 0, 0, tq, LANES, False,
                 "sb_sample")
    half = GB_D * P_D
    y8, hr, hi = _s5(p3.reshape(6, bx, t * NGB_D, LANES), 4, wp["bb"], wp["cc"], wp["a_re"], wp["a_im"], wp["d8"],
                     h_re0.reshape(bx, NGB_D, half), h_im0.reshape(bx, NGB_D, half), tt, "s5")
    yd = _glu(y8.reshape(m, -1), p3, 5, wp["w_glu"], wp["b_glu"], min(tm, 256), "glu")
    xn = _outproj_ln(yc.reshape(m, -1), yd, x2, wp["w_out"], wp["ln_g"], wp["ln_b"], alpha, min(tm, 256),
                     "od_outproj_ln")
    kk = p5[1].reshape(bx, t, H_C, DK)
    vv = p5[2].reshape(bx, t, H_C, DK)
    return xn.reshape(bx, t, d), kk, vv, hr.reshape(bx, G_D, P_D), hi.reshape(bx, G_D, P_D)


def kernel(x_prompt, x_sample, state_delta_S, state_delta_conv, state_mlstm_C, state_mlstm_n, state_mlstm_m, cache_sb_k, cache_sb_v, state_s5_re, state_s5_im, ev_w_in, ev_conv_w, ev_a_log, ev_dt_bias, ev_norm_a, ev_ig_bias, ev_fg_bias, ev_norm_b, ev_w_out, ev_ln_g, ev_ln_b, od_w_in, od_lam_re, od_lam_im, od_b_re, od_b_im, od_c_re, od_c_im, od_d, od_log_dt, od_w_glu, od_b_glu, od_w_out, od_ln_g, od_ln_b):
    n_even = ev_w_in.shape[0]
    n_odd = od_w_in.shape[0]
    depth = n_even + n_odd
    alpha = (2 * depth) ** 0.25
    nb, seq, _ = x_prompt.shape
    ns, dseq, _ = x_sample.shape
    xp, xs = x_prompt, x_sample
    outs = [[] for _ in range(18)]
    for layer in range(depth):
        j = layer // 2
        if layer % 2 == 0:
            wp = _pack_even_weights(ev_w_in[j], ev_conv_w[j], ev_a_log[j], ev_dt_bias[j], ev_norm_a[j], ev_ig_bias[j],
                                    ev_fg_bias[j], ev_norm_b[j], ev_w_out[j], ev_ln_g[j], ev_ln_b[j])
            zs = jnp.zeros((nb, H_A, DK, DK), F32)
            xp, s, cv, c, n, m = _even_layer(
                xp, zs, jnp.zeros((nb, CONV_W - 1, 3 * H_A * DK), F32), zs, jnp.zeros((nb, H_B, DK), F32),
                jnp.zeros((nb, H_B), F32), wp, alpha, 512, 256)
            for idx, val in zip((0, 2, 4, 6, 8), (s, cv, c, n, m)):
                outs[idx].append(val)
            xs, s, cv, c, n, m = _even_layer(
                xs, state_delta_S[j], state_delta_conv[j], state_mlstm_C[j], state_mlstm_n[j], state_mlstm_m[j],
                wp, alpha, ns * dseq, dseq)
            for idx, val in zip((1, 3, 5, 7, 9), (s, cv, c, n, m)):
                outs[idx].append(val)
        else:
            wp = _pack_odd_weights(od_w_in[j], od_lam_re[j], od_lam_im[j], od_b_re[j], od_b_im[j], od_c_re[j],
                                   od_c_im[j], od_d[j], od_log_dt[j], od_w_glu[j], od_b_glu[j], od_w_out[j],
                                   od_ln_g[j], od_ln_b[j])
            zst = jnp.zeros((nb, G_D, P_D), F32)
            xp, k, v, hr, hi = _odd_layer(xp, None, None, zst, zst, wp, alpha, 512, 128, 256)
            for idx, val in zip((10, 12, 14, 16), (k, v, hr, hi)):
                outs[idx].append(val)
            xs, k, v, hr, hi = _odd_layer(xs, cache_sb_k[j], cache_sb_v[j], state_s5_re[j], state_s5_im[j], wp, alpha,
                                          ns * dseq, dseq, dseq)
            for idx, val in zip((11, 13, 15, 17), (k, v, hr, hi)):
                outs[idx].append(val)
    return (xp, xs) + tuple(jnp.stack(o) for o in outs)
```

```python
import functools

import jax
import jax.numpy as jnp
from jax import lax
from jax.experimental import pallas as pl
from jax.experimental.pallas import tpu as pltpu

F32 = jnp.float32
BF16 = jnp.bfloat16

LANES = 128
SUBLANES = 8
CHUNK = 64
CONV_W = 4
H_A = 8
H_B = 8
H_C = 8
DK = 128
GC_D = 16
G_D = 64
P_D = 64
GB_D = 8
NGB_D = G_D // GB_D
W_HEADS = 1024
LN_EPS = 1e-5
NORM_EPS = 1e-6
VMEM_LIMIT = 48 * 1024 * 1024
SB_DEAD_LOG_WEIGHT = -104.0


def _cparams(sem):
    return pltpu.CompilerParams(dimension_semantics=sem, vmem_limit_bytes=VMEM_LIMIT)


def _sigmoid(x):
    return 1.0 / (1.0 + jnp.exp(-x))


def _silu(x):
    return x * _sigmoid(x)


def _softplus(x):
    return jnp.maximum(x, 0.0) + jnp.log1p(jnp.exp(-jnp.abs(x)))


def _dot(a, b):
    return jnp.dot(a, b, preferred_element_type=F32)


def _dot_nt(a, b):
    return lax.dot_general(a, b, (((1,), (1,)), ((), ())), preferred_element_type=F32)


def _dot_tn(a, b):
    return lax.dot_general(a, b, (((0,), (0,)), ((), ())), preferred_element_type=F32)


def _split3(x):
    hi = x.astype(BF16)
    r1 = x - hi.astype(F32)
    mid = r1.astype(BF16)
    lo = (r1 - mid.astype(F32)).astype(BF16)
    return hi, mid, lo


def _dot_mask_lhs(mask_bf16, x):
    hi, mid, lo = _split3(x)
    return _dot(mask_bf16, hi) + _dot(mask_bf16, mid) + _dot(mask_bf16, lo)


def _dot_mask_rhs(x, mask_bf16, pieces):
    parts = _split3(x)[:pieces]
    out = _dot(parts[0], mask_bf16)
    for p in parts[1:]:
        out = out + _dot(p, mask_bf16)
    return out


def _iota2(shape):
    return (lax.broadcasted_iota(jnp.int32, shape, 0), lax.broadcasted_iota(jnp.int32, shape, 1))


def _head_column(x, idx):
    lane = lax.broadcasted_iota(jnp.int32, x.shape, 1)
    return jnp.sum(jnp.where(lane == idx, x, 0.0), axis=1, keepdims=True)


def _proj_kernel(x_ref, w_ref, o_ref):
    o_ref[0] = _dot(x_ref[...].astype(BF16), w_ref[0])


def _proj(x2, w, tm, name):
    m, d = x2.shape
    ng, _, wn = w.shape
    return pl.pallas_call(
        _proj_kernel,
        grid=(m // tm, ng),
        in_specs=[pl.BlockSpec((tm, d), lambda i, j: (i, 0)),
                  pl.BlockSpec((1, d, wn), lambda i, j: (j, 0, 0))],
        out_specs=pl.BlockSpec((1, tm, wn), lambda i, j: (j, i, 0)),
        out_shape=jax.ShapeDtypeStruct((ng, m, wn), F32),
        compiler_params=_cparams(("parallel", "arbitrary")),
        name=name,
    )(x2, w)


def _outproj_ln_kernel(ya_ref, yb_ref, x_ref, wa_ref, wb_ref, g_ref, b_ref, o_ref, *, alpha):
    acc = _dot(ya_ref[...].astype(BF16), wa_ref[...]) + _dot(yb_ref[...].astype(BF16), wb_ref[...])
    r = alpha * x_ref[...] + acc
    mu = jnp.mean(r, axis=-1, keepdims=True)
    c = r - mu
    var = jnp.mean(c * c, axis=-1, keepdims=True)
    o_ref[...] = c * lax.rsqrt(var + LN_EPS) * g_ref[...] + b_ref[...]


def _outproj_ln(ya, yb, x2, w_out_bf16, ln_g, ln_b, alpha, tm, name):
    m, d = x2.shape
    wa = ya.shape[1]
    wb = yb.shape[1]
    return pl.pallas_call(
        functools.partial(_outproj_ln_kernel, alpha=alpha),
        grid=(m // tm,),
        in_specs=[pl.BlockSpec((tm, wa), lambda i: (i, 0)),
                  pl.BlockSpec((tm, wb), lambda i: (i, 0)),
                  pl.BlockSpec((tm, d), lambda i: (i, 0)),
                  pl.BlockSpec((wa, d), lambda i: (0, 0)),
                  pl.BlockSpec((wb, d), lambda i: (wa // wb, 0)),
                  pl.BlockSpec((1, d), lambda i: (0, 0)),
                  pl.BlockSpec((1, d), lambda i: (0, 0))],
        out_specs=pl.BlockSpec((tm, d), lambda i: (i, 0)),
        out_shape=jax.ShapeDtypeStruct((m, d), F32),
        compiler_params=_cparams(("parallel",)),
        name=name,
    )(ya, yb, x2, w_out_bf16, w_out_bf16, ln_g.reshape(1, d), ln_b.reshape(1, d))


def _conv_silu(x_ref, scr, w_ref, tb):
    scr[SUBLANES:SUBLANES + tb, :] = x_ref[...]
    w = w_ref[...]
    base = SUBLANES - (CONV_W - 1)
    acc = scr[base:base + tb, :] * w[0:1, :]
    for i in range(1, CONV_W):
        acc = acc + scr[base + i:base + i + tb, :] * w[i:i + 1, :]
    scr[0:SUBLANES, :] = scr[tb:tb + SUBLANES, :]
    return _silu(acc)


def _l2norm(x):
    return x * lax.rsqrt(jnp.sum(x * x, axis=-1, keepdims=True) + NORM_EPS)


def _rms_norm(x, g):
    return x * lax.rsqrt(jnp.mean(x * x, axis=-1, keepdims=True) + NORM_EPS) * g


def _unit_lower_inverse(n_mat, length):
    r, c = _iota2((length, length))
    x = jnp.where(r == c, 1.0, 0.0) + n_mat
    p = n_mat
    steps = length.bit_length() - 2
    for _ in range(steps):
        pb = p.astype(BF16)
        p = _dot(pb, pb)
        x = x + _dot(x.astype(BF16), p.astype(BF16))
    return x


def _gdn_kernel(qp_ref, kp_ref, vp_ref, z_ref, gt_ref, c0q_ref, c0k_ref, c0v_ref,
                cwq_ref, cwk_ref, cwv_ref, prm_ref, norm_ref, s0_ref,
                y_ref, sout_ref, s_scr, xq_scr, xk_scr, xv_scr, *, chunk, tb):
    h = pl.program_id(1)
    t = pl.program_id(2)

    @pl.when(t == 0)
    def _():
        s_scr[...] = s0_ref[...]
        xq_scr[0:SUBLANES, :] = c0q_ref[...]
        xk_scr[0:SUBLANES, :] = c0k_ref[...]
        xv_scr[0:SUBLANES, :] = c0v_ref[...]

    q = _l2norm(_conv_silu(qp_ref, xq_scr, cwq_ref, tb)) * (DK ** -0.5)
    k = _l2norm(_conv_silu(kp_ref, xk_scr, cwk_ref, tb))
    v = _conv_silu(vp_ref, xv_scr, cwv_ref, tb)

    gt = gt_ref[...]
    prm = prm_ref[...]
    beta_all = _sigmoid(gt)
    g_all = -jnp.exp(prm[0:1, :]) * _softplus(gt + prm[1:2, :])
    beta = _head_column(beta_all, h)
    g = _head_column(g_all, h + H_A)

    r, c = _iota2((chunk, chunk))
    incl = r >= c
    strict = r > c
    a_incl = jnp.where(incl, 1.0, 0.0).astype(BF16)
    b_gt = jnp.where(strict, 1.0, 0.0)
    ones_w = jnp.ones((chunk, LANES), F32)
    norm_g = norm_ref[...]
    z = z_ref[...]

    s_state = s_scr[...]
    for ci in range(tb // chunk):
        sl = slice(ci * chunk, (ci + 1) * chunk)
        qc, kc, vc = q[sl], k[sl], v[sl]
        bc, gcol = beta[sl], g[sl]
        e_mat = _dot_mask_lhs(a_incl, gcol * b_gt)
        gc = _dot_mask_lhs(a_incl, gcol * ones_w)
        decay = jnp.where(incl, jnp.exp(e_mat), 0.0)
        gam = jnp.exp(gc)
        g_last = gc[chunk - 1:chunk, :]
        kb = kc.astype(BF16)
        kk = _dot_nt(kb, kb)
        n_mat = jnp.where(strict, -(bc * kk * decay), 0.0)
        t_inv = _unit_lower_inverse(n_mat, chunk)
        rhs = jnp.concatenate([vc * bc, kc * (bc * gam)], axis=-1)
        sol = _dot(t_inv.astype(BF16), rhs.astype(BF16))
        u, w = sol[:, :DK], sol[:, DK:]
        qk = _dot_nt(qc.astype(BF16), kb) * decay
        q_dec = qc * gam
        k_dec = kc * jnp.exp(g_last - gc)
        sb = s_state.astype(BF16)
        e = u - _dot(w.astype(BF16), sb)
        eb = e.astype(BF16)
        o = _dot(q_dec.astype(BF16), sb) + _dot(qk.astype(BF16), eb)
        s_state = jnp.exp(g_last) * s_state + _dot_tn(k_dec.astype(BF16), eb)
        y_ref[sl, :] = _rms_norm(o, norm_g) * _silu(z[sl])
    s_scr[...] = s_state

    @pl.when(t == pl.num_programs(2) - 1)
    def _():
        sout_ref[...] = s_state


def _gdn(p5, g3, conv0p, conv_w, prm, norm_a, s0, chunk, tb, name):
    _, bx, t, _ = p5.shape
    nh = H_A

    def pspec(group):
        return pl.BlockSpec((None, None, tb, DK), lambda b, h, i, g=group: (g, b, i, h))

    def c0spec(group):
        return pl.BlockSpec((None, SUBLANES, DK), lambda b, h, i, g=group: (b, 0, g * nh + h))

    def cwspec(group):
        return pl.BlockSpec((CONV_W, DK), lambda b, h, i, g=group: (0, g * nh + h))

    return pl.pallas_call(
        functools.partial(_gdn_kernel, chunk=chunk, tb=tb),
        grid=(bx, nh, t // tb),
        in_specs=[pspec(0), pspec(1), pspec(2), pspec(3),
                  pl.BlockSpec((None, tb, LANES), lambda b, h, i: (b, i, 0)),
                  c0spec(0), c0spec(1), c0spec(2),
                  cwspec(0), cwspec(1), cwspec(2),
                  pl.BlockSpec((SUBLANES, LANES), lambda b, h, i: (0, 0)),
                  pl.BlockSpec((1, DK), lambda b, h, i: (0, 0)),
                  pl.BlockSpec((None, None, DK, DK), lambda b, h, i: (b, h, 0, 0))],
        out_specs=[pl.BlockSpec((None, tb, DK), lambda b, h, i: (b, i, h)),
                   pl.BlockSpec((None, None, DK, DK), lambda b, h, i: (b, h, 0, 0))],
        out_shape=[jax.ShapeDtypeStruct((bx, t, nh * DK), F32),
                   jax.ShapeDtypeStruct((bx, nh, DK, DK), F32)],
        scratch_shapes=[pltpu.VMEM((DK, DK), F32),
                        pltpu.VMEM((tb + SUBLANES, DK), F32),
                        pltpu.VMEM((tb + SUBLANES, DK), F32),
                        pltpu.VMEM((tb + SUBLANES, DK), F32)],
        compiler_params=_cparams(("parallel", "parallel", "arbitrary")),
        name=name,
    )(p5, p5, p5, p5, g3, conv0p, conv0p, conv0p, conv_w, conv_w, conv_w, prm, norm_a, s0)


def _mlstm_kernel(q_ref, k_ref, v_ref, o_ref, z_ref, gt_ref, prm_ref, norm_ref, c0_ref, n0_ref, m0_ref,
                  y_ref, cout_ref, nout_ref, mout_ref, c_scr, n_scr, m_scr, *, chunk, tb):
    h = pl.program_id(1)
    t = pl.program_id(2)

    @pl.when(t == 0)
    def _():
        c_scr[...] = c0_ref[...]
        n_scr[...] = n0_ref[...]
        m_scr[...] = m0_ref[...]

    q = q_ref[...]
    k = k_ref[...] * (DK ** -0.5)
    v = v_ref[...]
    gt = gt_ref[...]
    prm = prm_ref[...]
    ig_all = gt + prm[2:3, :]
    lf_all = -_softplus(-(gt + prm[3:4, :]))
    ig = _head_column(ig_all, h + 2 * H_A)
    lf = _head_column(lf_all, h + 2 * H_A + H_B)

    r, c = _iota2((chunk, chunk))
    incl = r >= c
    a_incl = jnp.where(incl, 1.0, 0.0).astype(BF16)
    b_gt = jnp.where(r > c, 1.0, 0.0)
    eye = jnp.where(r == c, 1.0, 0.0)
    ones_sq = jnp.ones((chunk, chunk), BF16)
    ones_w = jnp.ones((chunk, LANES), F32)
    norm_g = norm_ref[...]

    c_state = c_scr[...]
    n_state = n_scr[...]
    m_state = m_scr[...]
    for ci in range(tb // chunk):
        sl = slice(ci * chunk, (ci + 1) * chunk)
        qc, kc, vc = q[sl], k[sl], v[sl]
        igc, lfc = ig[sl], lf[sl]
        e_mat = _dot_mask_lhs(a_incl, lfc * b_gt)
        bcum_w = _dot_mask_lhs(a_incl, lfc * ones_w)
        ig_row = _dot_mask_lhs(ones_sq, igc * eye)
        bcum = bcum_w[:, 0:1]
        d_log = jnp.where(incl, e_mat + ig_row, -jnp.inf)
        m_intra = jnp.max(d_log, axis=-1, keepdims=True)
        qb = qc.astype(BF16)
        kb = kc.astype(BF16)
        vb = vc.astype(BF16)
        qk = _dot_nt(qb, kb)
        m_prev = m_state[:, 0:1]
        bm = bcum + m_prev
        m_t = jnp.maximum(bm, m_intra)
        p = jnp.exp(d_log - m_t)
        inter = jnp.exp(bm - m_t)
        pqk = p * qk
        num = inter * _dot(qb, c_state.astype(BF16)) + _dot(pqk.astype(BF16), vb)
        den = inter * jnp.sum(qc * n_state, axis=-1, keepdims=True) + jnp.sum(pqk, axis=-1, keepdims=True)
        hh = num / jnp.maximum(jnp.abs(den), jnp.exp(-m_t))
        m_last = m_t[chunk - 1:chunk, :]
        i_last = inter[chunk - 1:chunk, :]
        p_last = jnp.exp(bcum[chunk - 1:chunk, :] - bcum + igc - m_last)
        kp = kc * p_last
        c_state = i_last * c_state + _dot_tn(kp.astype(BF16), vb)
        n_state = i_last * n_state + jnp.sum(kp, axis=0, keepdims=True)
        m_state = m_last + jnp.zeros_like(m_state)
        hh = _sigmoid(o_ref[sl, :]) * hh
        y_ref[sl, :] = _rms_norm(hh, norm_g) * _silu(z_ref[sl, :])
    c_scr[...] = c_state
    n_scr[...] = n_state
    m_scr[...] = m_state

    @pl.when(t == pl.num_programs(2) - 1)
    def _():
        cout_ref[...] = c_state
        nout_ref[...] = n_state
        mout_ref[...] = m_state


def _mlstm(p5, g3, prm, norm_b, c0, n0, m0, chunk, tb, name):
    _, bx, t, _ = p5.shape
    nh = H_B

    def pspec(group):
        return pl.BlockSpec((None, None, tb, DK), lambda b, h, i, g=group: (g, b, i, h))

    row = pl.BlockSpec((None, None, 1, DK), lambda b, h, i: (b, h, 0, 0))
    return pl.pallas_call(
        functools.partial(_mlstm_kernel, chunk=chunk, tb=tb),
        grid=(bx, nh, t // tb),
        in_specs=[pspec(4), pspec(5), pspec(6), pspec(7), pspec(8),
                  pl.BlockSpec((None, tb, LANES), lambda b, h, i: (b, i, 0)),
                  pl.BlockSpec((SUBLANES, LANES), lambda b, h, i: (0, 0)),
                  pl.BlockSpec((1, DK), lambda b, h, i: (0, h)),
                  pl.BlockSpec((None, None, DK, DK), lambda b, h, i: (b, h, 0, 0)),
                  row, row],
        out_specs=[pl.BlockSpec((None, tb, DK), lambda b, h, i: (b, i, h)),
                   pl.BlockSpec((None, None, DK, DK), lambda b, h, i: (b, h, 0, 0)),
                   row, row],
        out_shape=[jax.ShapeDtypeStruct((bx, t, nh * DK), F32),
                   jax.ShapeDtypeStruct((bx, nh, DK, DK), F32),
                   jax.ShapeDtypeStruct((bx, nh, 1, DK), F32),
                   jax.ShapeDtypeStruct((bx, nh, 1, DK), F32)],
        scratch_shapes=[pltpu.VMEM((DK, DK), F32), pltpu.VMEM((1, DK), F32), pltpu.VMEM((1, DK), F32)],
        compiler_params=_cparams(("parallel", "parallel", "arbitrary")),
        name=name,
    )(p5, p5, p5, p5, p5, g3, prm, norm_b, c0, n0, m0)


def _sb_tile(qb, kt, vt, run, acc, u_incl, causal):
    z = _dot_nt(qb, kt.astype(BF16))
    sp = _softplus(z)
    l1m = -sp
    if causal is not None:
        l1m = jnp.where(causal, l1m, 0.0)
    incl = _dot_mask_rhs(l1m, u_incl, 2)
    att = jnp.exp((z - sp) + (incl - l1m) + run)
    if causal is not None:
        att = jnp.where(causal, att, 0.0)
    acc = acc + _dot(att.astype(BF16), vt.astype(BF16))
    return run + incl[:, 0:1], acc


def _sb_kernel(q_ref, kd_ref, vd_ref, kp_ref, vp_ref, z_ref, y_ref, *, tq, tk, prev_blocks_per_step, prev_blocks_fixed):
    i = pl.program_id(2)
    qb = (q_ref[...] * (DK ** -0.5)).astype(BF16)
    r, c = _iota2((tq, tq))
    rk, ck = _iota2((tk, tk))
    u_diag = jnp.where(r >= c, 1.0, 0.0).astype(BF16)
    u_prev = jnp.where(rk >= ck, 1.0, 0.0).astype(BF16)
    run = jnp.zeros((tq, 1), F32)
    acc = jnp.zeros((tq, DK), F32)
    run, acc = _sb_tile(qb, kd_ref[...], vd_ref[...], run, acc, u_diag, c < r)
    nprev = i * prev_blocks_per_step + prev_blocks_fixed

    def live(run):
        return (jnp.max(run) > SB_DEAD_LOG_WEIGHT).astype(jnp.int32)

    def cond(carry):
        it, go, _, _ = carry
        return jnp.logical_and(it < nprev, go > 0)

    def body(carry):
        it, _, run, acc = carry
        start = pl.multiple_of((nprev - 1 - it) * tk, tk)
        run, acc = _sb_tile(qb, kp_ref[pl.ds(start, tk), :], vp_ref[pl.ds(start, tk), :], run, acc, u_prev, None)
        return it + 1, live(run), run, acc

    _, _, run, acc = lax.while_loop(cond, body, (jnp.int32(0), live(run), run, acc))
    y_ref[...] = acc * _silu(z_ref[...])


def _sb(p5, kprev, vprev, kgroup, vgroup, tq, tk, causal_prev, name):
    _, bx, t, _ = p5.shape
    tp = kprev.shape[2]

    def blk(group):
        return pl.BlockSpec((None, None, tq, DK), lambda b, h, i, g=group: (g, b, i, h))

    def full(group):
        return pl.BlockSpec((None, None, tp, DK), lambda b, h, i, g=group: (g, b, 0, h))

    kern = functools.partial(_sb_kernel, tq=tq, tk=tk,
                             prev_blocks_per_step=(tq // tk) if causal_prev else 0,
                             prev_blocks_fixed=0 if causal_prev else tp // tk)
    return pl.pallas_call(
        kern,
        grid=(bx, H_C, t // tq),
        in_specs=[blk(0), blk(1), blk(2), full(kgroup), full(vgroup), blk(3)],
        out_specs=pl.BlockSpec((None, tq, DK), lambda b, h, i: (b, i, h)),
        out_shape=jax.ShapeDtypeStruct((bx, t, H_C * DK), F32),
        compiler_params=_cparams(("parallel", "parallel", "arbitrary")),
        name=name,
    )(p5, p5, p5, kprev, vprev, p5)


def _s5_kernel(u_ref, bb_ref, cc_ref, ar_ref, ai_ref, d_ref, h0r_ref, h0i_ref,
               y_ref, hr_out, hi_out, hs, hr_scr, hi_scr, *, tt):
    t = pl.program_id(1)
    half = GB_D * P_D

    @pl.when(t == 0)
    def _():
        hr_scr[...] = h0r_ref[...]
        hi_scr[...] = h0i_ref[...]

    nl = half // LANES
    for gb in range(NGB_D):
        rows = pl.ds(gb, tt, stride=NGB_D)
        bu = _dot(u_ref[rows, :].astype(BF16), bb_ref[gb])
        for j in range(2 * nl):
            hs[j, rows, :] = bu[:, j * LANES:(j + 1) * LANES]

    ar = [ar_ref[:, j * LANES:(j + 1) * LANES] for j in range(nl)]
    ai = [ai_ref[:, j * LANES:(j + 1) * LANES] for j in range(nl)]

    def step(s, carry):
        rows = pl.ds(pl.multiple_of(s * NGB_D, NGB_D), NGB_D)
        new = [None] * (2 * nl)
        for j in range(nl):
            hr, hi = carry[j], carry[nl + j]
            new[j] = ar[j] * hr - ai[j] * hi + hs[j, rows, :]
            new[nl + j] = ar[j] * hi + ai[j] * hr + hs[nl + j, rows, :]
        for j in range(2 * nl):
            hs[j, rows, :] = new[j]
        return tuple(new)

    init = tuple(hr_scr[:, j * LANES:(j + 1) * LANES] for j in range(nl)) + tuple(
        hi_scr[:, j * LANES:(j + 1) * LANES] for j in range(nl))
    fin = lax.fori_loop(0, tt, step, init, unroll=8)
    hr = jnp.concatenate(fin[:nl], axis=1)
    hi = jnp.concatenate(fin[nl:], axis=1)
    hr_scr[...] = hr
    hi_scr[...] = hi

    for gb in range(NGB_D):
        rows = pl.ds(gb, tt, stride=NGB_D)
        hcat = jnp.concatenate([hs[j, rows, :] for j in range(2 * nl)], axis=1)
        y = _dot(hcat.astype(BF16), cc_ref[gb])
        y_ref[rows, :] = y + d_ref[gb:gb + 1, :] * u_ref[rows, :]

    @pl.when(t == pl.num_programs(1) - 1)
    def _():
        hr_out[...] = hr
        hi_out[...] = hi


def _s5(u8, ugroup, bb, cc, a_re, a_im, d8, h0r, h0i, tt, name):
    _, bx, t8, _ = u8.shape
    half = GB_D * P_D
    st = pl.BlockSpec((None, NGB_D, half), lambda b, i: (b, 0, 0))
    prm = pl.BlockSpec((NGB_D, half), lambda b, i: (0, 0))
    return pl.pallas_call(
        functools.partial(_s5_kernel, tt=tt),
        grid=(bx, t8 // (tt * NGB_D)),
        in_specs=[pl.BlockSpec((None, None, tt * NGB_D, LANES), lambda b, i: (ugroup, b, i, 0)),
                  pl.BlockSpec((NGB_D, LANES, 2 * half), lambda b, i: (0, 0, 0)),
                  pl.BlockSpec((NGB_D, 2 * half, LANES), lambda b, i: (0, 0, 0)),
                  prm, prm,
                  pl.BlockSpec((NGB_D, LANES), lambda b, i: (0, 0)),
                  st, st],
        out_specs=[pl.BlockSpec((None, tt * NGB_D, LANES), lambda b, i: (b, i, 0)), st, st],
        out_shape=[jax.ShapeDtypeStruct((bx, t8, LANES), F32),
                   jax.ShapeDtypeStruct((bx, NGB_D, half), F32),
                   jax.ShapeDtypeStruct((bx, NGB_D, half), F32)],
        scratch_shapes=[pltpu.VMEM((2 * half // LANES, tt * NGB_D, LANES), F32),
                        pltpu.VMEM((NGB_D, half), F32), pltpu.VMEM((NGB_D, half), F32)],
        compiler_params=_cparams(("parallel", "arbitrary")),
        name=name,
    )(u8, bb, cc, a_re, a_im, d8, h0r, h0i)


def _glu_kernel(y_ref, z_ref, w_ref, b_ref, o_ref):
    y = y_ref[...]
    ys = 0.5 * y * (1.0 + jnp.tanh(0.7978845608028654 * (y + 0.044715 * (y * y * y))))
    gate = _sigmoid(_dot(ys.astype(BF16), w_ref[...]) + b_ref[...])
    o_ref[...] = ys * gate * _silu(z_ref[...])


def _glu(y2, z3, zgroup, w_glu_bf16, b_glu, tm, name):
    m, w = y2.shape
    return pl.pallas_call(
        _glu_kernel,
        grid=(m // tm,),
        in_specs=[pl.BlockSpec((tm, w), lambda i: (i, 0)),
                  pl.BlockSpec((None, tm, w), lambda i: (zgroup, i, 0)),
                  pl.BlockSpec((w, w), lambda i: (0, 0)),
                  pl.BlockSpec((1, w), lambda i: (0, 0))],
        out_specs=pl.BlockSpec((tm, w), lambda i: (i, 0)),
        out_shape=jax.ShapeDtypeStruct((m, w), F32),
        compiler_params=_cparams(("parallel",)),
        name=name,
    )(y2, z3, w_glu_bf16, b_glu.reshape(1, w))


def _pack_even_weights(w_in, conv_w, a_log, dt_bias, norm_a, ig_bias, fg_bias, norm_b, w_out, ln_g, ln_b):
    d = w_in.shape[0]
    wa = H_A * DK
    wb = H_B * DK
    o = 0
    qkv_a = w_in[:, o:o + 3 * wa]; o += 3 * wa
    z_a = w_in[:, o:o + wa]; o += wa
    gate_a = w_in[:, o:o + 2 * H_A]; o += 2 * H_A
    main_b = w_in[:, o:o + 5 * wb]; o += 5 * wb
    gate_b = w_in[:, o:o + 2 * H_B]
    w_main = jnp.concatenate([qkv_a, z_a, main_b], axis=1).astype(BF16)
    w_main = w_main.reshape(d, 9, W_HEADS).transpose(1, 0, 2)
    n_gate = 2 * H_A + 2 * H_B
    w_gate = jnp.concatenate([gate_a, gate_b, jnp.zeros((d, LANES - n_gate), F32)], axis=1).astype(BF16)[None]
    prm = jnp.zeros((SUBLANES, LANES), F32)
    prm = prm.at[0, H_A:2 * H_A].set(a_log)
    prm = prm.at[1, H_A:2 * H_A].set(dt_bias)
    prm = prm.at[2, 2 * H_A:2 * H_A + H_B].set(ig_bias)
    prm = prm.at[3, 2 * H_A + H_B:2 * H_A + 2 * H_B].set(fg_bias)
    return dict(w_main=w_main, w_gate=w_gate, conv_w=conv_w, prm=prm, norm_a=norm_a.reshape(1, DK),
                norm_b=norm_b.reshape(1, wb), w_out=w_out.astype(BF16), ln_g=ln_g, ln_b=ln_b)


def _even_layer(x, s0, conv0, c0, n0, m0, wp, alpha, tm, tb):
    bx, t, d = x.shape
    m = bx * t
    chunk = min(CHUNK, t)
    x2 = x.reshape(m, d)
    p5 = _proj(x2, wp["w_main"], tm, "ev_proj").reshape(9, bx, t, W_HEADS)
    g3 = _proj(x2, wp["w_gate"], tm, "ev_gate_proj").reshape(bx, t, LANES)
    conv0p = jnp.concatenate([jnp.zeros((bx, SUBLANES - (CONV_W - 1), conv0.shape[2]), F32), conv0], axis=1)
    ya, s_new = _gdn(p5, g3, conv0p, wp["conv_w"], wp["prm"], wp["norm_a"], s0, chunk, tb, "gdn")
    m0w = jnp.broadcast_to(m0[:, :, None, None], (bx, H_B, 1, DK))
    yb, c_new, n_new, m_new = _mlstm(p5, g3, wp["prm"], wp["norm_b"], c0, n0[:, :, None, :], m0w, chunk, tb, "mlstm")
    xn = _outproj_ln(ya.reshape(m, -1), yb.reshape(m, -1), x2, wp["w_out"], wp["ln_g"], wp["ln_b"], alpha, min(tm, 256),
                     "ev_outproj_ln")
    conv_new = jnp.concatenate([p5[0, :, t - (CONV_W - 1):], p5[1, :, t - (CONV_W - 1):], p5[2, :, t - (CONV_W - 1):]],
                               axis=-1)
    return xn.reshape(bx, t, d), s_new, conv_new, c_new, n_new[:, :, 0, :], m_new[:, :, 0, 0]


def _pack_odd_weights(w_in, lam_re, lam_im, b_re, b_im, c_re, c_im, d, log_dt, w_glu, b_glu, w_out, ln_g, ln_b):
    dm = w_in.shape[0]
    w_main = w_in.astype(BF16).reshape(dm, 6, W_HEADS).transpose(1, 0, 2)
    dt = jnp.exp(log_dt)[:, None]
    mag = jnp.exp(lam_re * dt)
    a_re, a_im = mag * jnp.cos(lam_im * dt), mag * jnp.sin(lam_im * dt)
    den = lam_re * lam_re + lam_im * lam_im
    f_re = ((a_re - 1.0) * lam_re + a_im * lam_im) / den
    f_im = (a_im * lam_re - (a_re - 1.0) * lam_im) / den
    bb_re = f_re[..., None] * b_re - f_im[..., None] * b_im
    bb_im = f_re[..., None] * b_im + f_im[..., None] * b_re
    half = GB_D * P_D
    eye = jnp.eye(GB_D, dtype=F32)

    def blockdiag_in(bb):
        bbg = bb.reshape(NGB_D, GB_D, P_D, GC_D)
        return jnp.einsum("bgpc,gh->bgchp", bbg, eye).reshape(NGB_D, GB_D * GC_D, half)

    def blockdiag_out(cm):
        cg = cm.reshape(NGB_D, GB_D, GC_D, P_D)
        return jnp.einsum("bgcp,gh->bgphc", cg, eye).reshape(NGB_D, half, GB_D * GC_D)

    bb = jnp.concatenate([blockdiag_in(bb_re), blockdiag_in(bb_im)], axis=2).astype(BF16)
    cc = jnp.concatenate([blockdiag_out(c_re), -blockdiag_out(c_im)], axis=1).astype(BF16)
    return dict(w_main=w_main, bb=bb, cc=cc, a_re=a_re.reshape(NGB_D, half), a_im=a_im.reshape(NGB_D, half),
                d8=d.reshape(NGB_D, LANES), w_glu=w_glu.astype(BF16), b_glu=b_glu, w_out=w_out.astype(BF16),
                ln_g=ln_g, ln_b=ln_b)


def _odd_layer(x, k_cache, v_cache, h_re0, h_im0, wp, alpha, tm, tq, tt):
    bx, t, d = x.shape
    m = bx * t
    x2 = x.reshape(m, d)
    p3 = _proj(x2, wp["w_main"], tm, "od_proj")
    p5 = p3.reshape(6, bx, t, W_HEADS)
    if k_cache is None:
        yc = _sb(p5, p5, p5, 1, 2, tq, tq, True, "sb_prompt")
    else:
        past = k_cache.shape[1]
        yc = _sb(p5, k_cache.reshape(1, bx, past, -1), v_cache.reshape(1, bx, past, -1), 0, 0, tq, LANES, False,
                 "sb_sample")
    half = GB_D * P_D
    y8, hr, hi = _s5(p3.reshape(6, bx, t * NGB_D, LANES), 4, wp["bb"], wp["cc"], wp["a_re"], wp["a_im"], wp["d8"],
                     h_re0.reshape(bx, NGB_D, half), h_im0.reshape(bx, NGB_D, half), tt, "s5")
    yd = _glu(y8.reshape(m, -1), p3, 5, wp["w_glu"], wp["b_glu"], min(tm, 256), "glu")
    xn = _outproj_ln(yc.reshape(m, -1), yd, x2, wp["w_out"], wp["ln_g"], wp["ln_b"], alpha, min(tm, 256),
                     "od_outproj_ln")
    kk = p5[1].reshape(bx, t, H_C, DK)
    vv = p5[2].reshape(bx, t, H_C, DK)
    return xn.reshape(bx, t, d), kk, vv, hr.reshape(bx, G_D, P_D), hi.reshape(bx, G_D, P_D)


def kernel(x_prompt, x_sample, state_delta_S, state_delta_conv, state_mlstm_C, state_mlstm_n, state_mlstm_m, cache_sb_k, cache_sb_v, state_s5_re, state_s5_im, ev_w_in, ev_conv_w, ev_a_log, ev_dt_bias, ev_norm_a, ev_ig_bias, ev_fg_bias, ev_norm_b, ev_w_out, ev_ln_g, ev_ln_b, od_w_in, od_lam_re, od_lam_im, od_b_re, od_b_im, od_c_re, od_c_im, od_d, od_log_dt, od_w_glu, od_b_glu, od_w_out, od_ln_g, od_ln_b):
    n_even = ev_w_in.shape[0]
    n_odd = od_w_in.shape[0]
    depth = n_even + n_odd
    alpha = (2 * depth) ** 0.25
    nb, seq, _ = x_prompt.shape
    ns, dseq, _ = x_sample.shape
    xp, xs = x_prompt, x_sample
    outs = [[] for _ in range(18)]
    for layer in range(depth):
        j = layer // 2
        if layer % 2 == 0:
            wp = _pack_even_weights(ev_w_in[j], ev_conv_w[j], ev_a_log[j], ev_dt_bias[j], ev_norm_a[j], ev_ig_bias[j],
                                    ev_fg_bias[j], ev_norm_b[j], ev_w_out[j], ev_ln_g[j], ev_ln_b[j])
            zs = jnp.zeros((nb, H_A, DK, DK), F32)
            xp, s, cv, c, n, m = _even_layer(
                xp, zs, jnp.zeros((nb, CONV_W - 1, 3 * H_A * DK), F32), zs, jnp.zeros((nb, H_B, DK), F32),
                jnp.zeros((nb, H_B), F32), wp, alpha, 512, 256)
            for idx, val in zip((0, 2, 4, 6, 8), (s, cv, c, n, m)):
                outs[idx].append(val)
            xs, s, cv, c, n, m = _even_layer(
                xs, state_delta_S[j], state_delta_conv[j], state_mlstm_C[j], state_mlstm_n[j], state_mlstm_m[j],
                wp, alpha, ns * dseq, dseq)
            for idx, val in zip((1, 3, 5, 7, 9), (s, cv, c, n, m)):
                outs[idx].append(val)
        else:
            wp = _pack_odd_weights(od_w_in[j], od_lam_re[j], od_lam_im[j], od_b_re[j], od_b_im[j], od_c_re[j],
                                   od_c_im[j], od_d[j], od_log_dt[j], od_w_glu[j], od_b_glu[j], od_w_out[j],
                                   od_ln_g[j], od_ln_b[j])
            zst = jnp.zeros((nb, G_D, P_D), F32)
            xp, k, v, hr, hi = _odd_layer(xp, None, None, zst, zst, wp, alpha, 512, 128, 256)
            for idx, val in zip((10, 12, 14, 16), (k, v, hr, hi)):
                outs[idx].append(val)
            xs, k, v, hr, hi = _odd_layer(xs, cache_sb_k[j], cache_sb_v[j], state_s5_re[j], state_s5_im[j], wp, alpha,
                                          ns * dseq, dseq, dseq)
            for idx, val in zip((11, 13, 15, 17), (k, v, hr, hi)):
                outs[idx].append(val)
    return (xp, xs) + tuple(jnp.stack(o) for o in outs)
```

```python
import functools

import jax
import jax.numpy as jnp
from jax import lax
from jax.experimental import pallas as pl
from jax.experimental.pallas import tpu as pltpu

F32 = jnp.float32
BF16 = jnp.bfloat16

LANES = 128
SUBLANES = 8
CHUNK = 64
CONV_W = 4
H_A = 8
H_B = 8
H_C = 8
DK = 128
GC_D = 16
G_D = 64
P_D = 64
GB_D = 8
NGB_D = G_D // GB_D
W_HEADS = 1024
LN_EPS = 1e-5
NORM_EPS = 1e-6
VMEM_LIMIT = 48 * 1024 * 1024
SB_DEAD_LOG_WEIGHT = -104.0
SB_TILE = 128


def _cparams(sem):
    return pltpu.CompilerParams(dimension_semantics=sem, vmem_limit_bytes=VMEM_LIMIT)


def _sigmoid(x):
    return 1.0 / (1.0 + jnp.exp(-x))


def _silu(x):
    return x * _sigmoid(x)


def _softplus(x):
    return jnp.maximum(x, 0.0) + jnp.log1p(jnp.exp(-jnp.abs(x)))


def _dot(a, b):
    return jnp.dot(a, b, preferred_element_type=F32)


def _dot_nt(a, b):
    return lax.dot_general(a, b, (((1,), (1,)), ((), ())), preferred_element_type=F32)


def _dot_tn(a, b):
    return lax.dot_general(a, b, (((0,), (0,)), ((), ())), preferred_element_type=F32)


def _split3(x):
    hi = x.astype(BF16)
    r1 = x - hi.astype(F32)
    mid = r1.astype(BF16)
    lo = (r1 - mid.astype(F32)).astype(BF16)
    return hi, mid, lo


def _dot_mask_lhs(mask_bf16, x):
    hi, mid, lo = _split3(x)
    return _dot(mask_bf16, hi) + _dot(mask_bf16, mid) + _dot(mask_bf16, lo)


def _dot_tn_split(x, mask_bf16):
    hi, mid, lo = _split3(x)
    return _dot_tn(hi, mask_bf16) + _dot_tn(mid, mask_bf16) + _dot_tn(lo, mask_bf16)


def _iota2(shape):
    return (lax.broadcasted_iota(jnp.int32, shape, 0), lax.broadcasted_iota(jnp.int32, shape, 1))


def _head(x, h):
    return x[:, h * DK:(h + 1) * DK]


def _proj_kernel(x_ref, w_ref, o_ref):
    o_ref[0] = _dot(x_ref[...].astype(BF16), w_ref[0])


def _proj(x2, w, tm, name):
    m, d = x2.shape
    ng, _, wn = w.shape
    return pl.pallas_call(
        _proj_kernel,
        grid=(m // tm, ng),
        in_specs=[pl.BlockSpec((tm, d), lambda i, j: (i, 0)),
                  pl.BlockSpec((1, d, wn), lambda i, j: (j, 0, 0))],
        out_specs=pl.BlockSpec((1, tm, wn), lambda i, j: (j, i, 0)),
        out_shape=jax.ShapeDtypeStruct((ng, m, wn), F32),
        compiler_params=_cparams(("parallel", "arbitrary")),
        name=name,
    )(x2, w)


def _outproj_ln_kernel(ya_ref, yb_ref, x_ref, wa_ref, wb_ref, g_ref, b_ref, o_ref, *, alpha):
    acc = _dot(ya_ref[...].astype(BF16), wa_ref[...]) + _dot(yb_ref[...].astype(BF16), wb_ref[...])
    r = alpha * x_ref[...] + acc
    mu = jnp.mean(r, axis=-1, keepdims=True)
    c = r - mu
    var = jnp.mean(c * c, axis=-1, keepdims=True)
    o_ref[...] = c * lax.rsqrt(var + LN_EPS) * g_ref[...] + b_ref[...]


def _outproj_ln(ya, yb, x2, w_out_bf16, ln_g, ln_b, alpha, tm, name):
    m, d = x2.shape
    wa = ya.shape[1]
    wb = yb.shape[1]
    return pl.pallas_call(
        functools.partial(_outproj_ln_kernel, alpha=alpha),
        grid=(m // tm,),
        in_specs=[pl.BlockSpec((tm, wa), lambda i: (i, 0)),
                  pl.BlockSpec((tm, wb), lambda i: (i, 0)),
                  pl.BlockSpec((tm, d), lambda i: (i, 0)),
                  pl.BlockSpec((wa, d), lambda i: (0, 0)),
                  pl.BlockSpec((wb, d), lambda i: (wa // wb, 0)),
                  pl.BlockSpec((1, d), lambda i: (0, 0)),
                  pl.BlockSpec((1, d), lambda i: (0, 0))],
        out_specs=pl.BlockSpec((tm, d), lambda i: (i, 0)),
        out_shape=jax.ShapeDtypeStruct((m, d), F32),
        compiler_params=_cparams(("parallel",)),
        name=name,
    )(ya, yb, x2, w_out_bf16, w_out_bf16, ln_g.reshape(1, d), ln_b.reshape(1, d))


def _conv_silu(x_ref, scr, w_ref, tb):
    scr[SUBLANES:SUBLANES + tb, :] = x_ref[...]
    w = w_ref[...]
    base = SUBLANES - (CONV_W - 1)
    acc = scr[base:base + tb, :] * w[0:1, :]
    for i in range(1, CONV_W):
        acc = acc + scr[base + i:base + i + tb, :] * w[i:i + 1, :]
    scr[0:SUBLANES, :] = scr[tb:tb + SUBLANES, :]
    return _silu(acc)


def _l2norm(x):
    return x * lax.rsqrt(jnp.sum(x * x, axis=-1, keepdims=True) + NORM_EPS)


def _rms_norm(x, g):
    return x * lax.rsqrt(jnp.mean(x * x, axis=-1, keepdims=True) + NORM_EPS) * g


def _unit_lower_inverses(n_mats, length):
    r, c = _iota2((length, length))
    eye = jnp.where(r == c, 1.0, 0.0)
    xs = [eye + n for n in n_mats]
    ps = list(n_mats)
    for _ in range(length.bit_length() - 2):
        pbs = [p.astype(BF16) for p in ps]
        ps = [_dot(pb, pb) for pb in pbs]
        xs = [x + _dot(x.astype(BF16), p.astype(BF16)) for x, p in zip(xs, ps)]
    return xs


def _gdn_kernel(qp_ref, kp_ref, vp_ref, z_ref, gt_ref, c0q_ref, c0k_ref, c0v_ref,
                cwq_ref, cwk_ref, cwv_ref, prm_ref, norm_ref, s0_ref,
                y_ref, sout_ref, s_scr, xq_scr, xk_scr, xv_scr, *, chunk, tb):
    t = pl.program_id(1)
    nh = H_A

    @pl.when(t == 0)
    def _():
        s_scr[...] = s0_ref[...]
        xq_scr[0:SUBLANES, :] = c0q_ref[...]
        xk_scr[0:SUBLANES, :] = c0k_ref[...]
        xv_scr[0:SUBLANES, :] = c0v_ref[...]

    q_all = _conv_silu(qp_ref, xq_scr, cwq_ref, tb)
    k_all = _conv_silu(kp_ref, xk_scr, cwk_ref, tb)
    v_all = _conv_silu(vp_ref, xv_scr, cwv_ref, tb)
    q_h = [_l2norm(_head(q_all, h)) * (DK ** -0.5) for h in range(nh)]
    k_h = [_l2norm(_head(k_all, h)) for h in range(nh)]

    gt = gt_ref[...]
    prm = prm_ref[...]
    beta_all = _sigmoid(gt)
    g_all = -jnp.exp(prm[0:1, :]) * _softplus(gt + prm[1:2, :])

    r, c = _iota2((chunk, chunk))
    incl = r >= c
    strict = r > c
    a_incl = jnp.where(incl, 1.0, 0.0).astype(BF16)
    a_incl_t = jnp.where(r <= c, 1.0, 0.0).astype(BF16)
    norm_g = norm_ref[...]

    def precompute(ci):
        sl = slice(ci * chunk, (ci + 1) * chunk)
        g_c = g_all[sl]
        gc_cols = _dot_mask_lhs(a_incl, g_c)
        gc_rows = _dot_tn_split(g_c, a_incl_t)
        beta_c = beta_all[sl]
        heads = []
        for h in range(nh):
            gcol = gc_cols[:, nh + h:nh + h + 1]
            grow = gc_rows[nh + h:nh + h + 1, :]
            bc = beta_c[:, h:h + 1]
            qc, kc, vc = q_h[h][sl], k_h[h][sl], _head(v_all, h)[sl]
            decay = jnp.where(incl, jnp.exp(gcol - grow), 0.0)
            kb = kc.astype(BF16)
            qkk = _dot_nt(jnp.concatenate([qc, kc], axis=0).astype(BF16), kb)
            gam = jnp.exp(gcol)
            g_last = gcol[chunk - 1:chunk, :]
            heads.append(dict(
                n=jnp.where(strict, -(bc * qkk[chunk:] * decay), 0.0),
                qk=(qkk[:chunk] * decay).astype(BF16),
                rhs=jnp.concatenate([vc * bc, kc * (bc * gam)], axis=-1).astype(BF16),
                q_dec=qc * gam,
                k_dec=(kc * jnp.exp(g_last - gcol)).astype(BF16),
                s_decay=jnp.exp(g_last)))
        t_inv = _unit_lower_inverses([hd["n"] for hd in heads], chunk)
        for hd, ti in zip(heads, t_inv):
            sol = _dot(ti.astype(BF16), hd["rhs"])
            hd["u"] = sol[:, :DK]
            hd["wq"] = jnp.concatenate([sol[:, DK:], hd["q_dec"]], axis=0).astype(BF16)
        return heads

    def advance(ci, heads, states):
        sl = slice(ci * chunk, (ci + 1) * chunk)
        wss = [_dot(hd["wq"], s.astype(BF16)) for hd, s in zip(heads, states)]
        es = [(hd["u"] - ws[:chunk]).astype(BF16) for hd, ws in zip(heads, wss)]
        new_states = [hd["s_decay"] * s + _dot_tn(hd["k_dec"], e) for hd, s, e in zip(heads, states, es)]
        outs = [ws[chunk:] + _dot(hd["qk"], e) for hd, ws, e in zip(heads, wss, es)]
        for h in range(nh):
            y_ref[sl, h * DK:(h + 1) * DK] = _rms_norm(outs[h], norm_g) * _silu(z_ref[sl, h * DK:(h + 1) * DK])
        return new_states

    n_chunks = tb // chunk
    states = [s_scr[h] for h in range(nh)]
    pre = precompute(0)
    for ci in range(n_chunks):
        nxt = precompute(ci + 1) if ci + 1 < n_chunks else None
        states = advance(ci, pre, states)
        pre = nxt
    for h in range(nh):
        s_scr[h] = states[h]

    @pl.when(t == pl.num_programs(1) - 1)
    def _():
        for h in range(nh):
            sout_ref[h] = states[h]


def _gdn(p5, g3, conv0p, conv_w, prm, norm_a, s0, chunk, tb, name):
    _, bx, t, w = p5.shape
    nh = H_A

    def pspec(group):
        return pl.BlockSpec((None, None, tb, w), lambda b, i, g=group: (g, b, i, 0))

    def c0spec(group):
        return pl.BlockSpec((None, SUBLANES, w), lambda b, i, g=group: (b, 0, g))

    def cwspec(group):
        return pl.BlockSpec((CONV_W, w), lambda b, i, g=group: (0, g))

    return pl.pallas_call(
        functools.partial(_gdn_kernel, chunk=chunk, tb=tb),
        grid=(bx, t // tb),
        in_specs=[pspec(0), pspec(1), pspec(2), pspec(3),
                  pl.BlockSpec((None, tb, LANES), lambda b, i: (b, i, 0)),
                  c0spec(0), c0spec(1), c0spec(2),
                  cwspec(0), cwspec(1), cwspec(2),
                  pl.BlockSpec((SUBLANES, LANES), lambda b, i: (0, 0)),
                  pl.BlockSpec((1, DK), lambda b, i: (0, 0)),
                  pl.BlockSpec((None, nh, DK, DK), lambda b, i: (b, 0, 0, 0))],
        out_specs=[pl.BlockSpec((None, tb, w), lambda b, i: (b, i, 0)),
                   pl.BlockSpec((None, nh, DK, DK), lambda b, i: (b, 0, 0, 0))],
        out_shape=[jax.ShapeDtypeStruct((bx, t, w), F32),
                   jax.ShapeDtypeStruct((bx, nh, DK, DK), F32)],
        scratch_shapes=[pltpu.VMEM((nh, DK, DK), F32),
                        pltpu.VMEM((tb + SUBLANES, w), F32),
                        pltpu.VMEM((tb + SUBLANES, w), F32),
                        pltpu.VMEM((tb + SUBLANES, w), F32)],
        compiler_params=_cparams(("parallel", "arbitrary")),
        name=name,
    )(p5, p5, p5, p5, g3, conv0p, conv0p, conv0p, conv_w, conv_w, conv_w, prm, norm_a, s0)


def _mlstm_kernel(q_ref, k_ref, v_ref, o_ref, z_ref, gt_ref, prm_ref, norm_ref, c0_ref, n0_ref, m0_ref,
                  y_ref, cout_ref, nout_ref, mout_ref, cn_scr, m_scr, *, chunk, tb):
    t = pl.program_id(1)
    nh = H_B
    heads = range(nh)
    ig_lane = 2 * H_A
    lf_lane = 2 * H_A + H_B

    @pl.when(t == 0)
    def _():
        cn_scr[:, :, :DK] = c0_ref[...]
        cn_scr[:, :, DK:] = n0_ref[...]
        m_scr[...] = m0_ref[...]

    gt = gt_ref[...]
    prm = prm_ref[...]
    ig_all = gt + prm[2:3, :]
    lf_all = -_softplus(-(gt + prm[3:4, :]))

    r, c = _iota2((chunk, chunk))
    incl = r >= c
    a_incl = jnp.where(incl, 1.0, 0.0).astype(BF16)
    a_incl_t = jnp.where(r <= c, 1.0, 0.0).astype(BF16)
    eye = jnp.where(r == c, 1.0, 0.0).astype(BF16)
    ones_w = jnp.ones((chunk, DK), F32)
    ones_b = jnp.ones((chunk, DK), BF16)
    norm_g = norm_ref[...]

    def precompute(ci):
        sl = slice(ci * chunk, (ci + 1) * chunk)
        lf_c = lf_all[sl]
        ig_c = ig_all[sl]
        b_cols = _dot_mask_lhs(a_incl, lf_c)
        b_rows = _dot_tn_split(lf_c, a_incl_t)
        ig_rows = _dot_tn_split(ig_c, eye)
        bcols = [b_cols[:, lf_lane + h:lf_lane + h + 1] for h in heads]
        d_logs = [jnp.where(incl, bcols[h] - b_rows[lf_lane + h:lf_lane + h + 1, :]
                            + ig_rows[ig_lane + h:ig_lane + h + 1, :], -jnp.inf) for h in heads]
        m_intras = [jnp.max(d, axis=-1, keepdims=True) for d in d_logs]
        qbs = [q_ref[sl, h * DK:(h + 1) * DK].astype(BF16) for h in heads]
        kcs = [k_ref[sl, h * DK:(h + 1) * DK] * (DK ** -0.5) for h in heads]
        v1s = [jnp.concatenate([v_ref[sl, h * DK:(h + 1) * DK].astype(BF16), ones_b], axis=1) for h in heads]
        qks = [_dot_nt(qb, kc.astype(BF16)) for qb, kc in zip(qbs, kcs)]
        pq0s = [(jnp.exp(d - mi) * qk).astype(BF16) for d, mi, qk in zip(d_logs, m_intras, qks)]
        avs = [_dot(pq0, v1) for pq0, v1 in zip(pq0s, v1s)]
        return [dict(b_w=bcols[h] * ones_w, mi_w=m_intras[h] * ones_w,
                     tail_w=(bcols[h][chunk - 1:chunk, :] - bcols[h] + ig_c[:, ig_lane + h:ig_lane + h + 1]) * ones_w,
                     qb=qbs[h], kc=kcs[h], v1=v1s[h], av=avs[h]) for h in heads]

    def advance(ci, pre, cns, ms):
        sl = slice(ci * chunk, (ci + 1) * chunk)
        qcns = [_dot(pre[h]["qb"], cns[h].astype(BF16)) for h in heads]
        bms = [pre[h]["b_w"] + ms[h] for h in heads]
        m_ts = [jnp.maximum(bms[h], pre[h]["mi_w"]) for h in heads]
        corrs = [jnp.exp(pre[h]["mi_w"] - m_ts[h]) for h in heads]
        inters = [jnp.exp(bms[h] - m_ts[h]) for h in heads]
        m_lasts = [m_t[chunk - 1:chunk, :] for m_t in m_ts]
        i_lasts = [inter[chunk - 1:chunk, :] for inter in inters]
        kps = [(pre[h]["kc"] * jnp.exp(pre[h]["tail_w"] - m_lasts[h])).astype(BF16) for h in heads]
        new_cns = [jnp.concatenate([i_lasts[h], i_lasts[h]], axis=1) * cns[h] + _dot_tn(kps[h], pre[h]["v1"])
                   for h in heads]
        for h in heads:
            num = inters[h] * qcns[h][:, :DK] + corrs[h] * pre[h]["av"][:, :DK]
            den = inters[h] * qcns[h][:, DK:] + corrs[h] * pre[h]["av"][:, DK:]
            hh = num / jnp.maximum(jnp.abs(den), jnp.exp(-m_ts[h]))
            hh = _sigmoid(o_ref[sl, h * DK:(h + 1) * DK]) * hh
            y_ref[sl, h * DK:(h + 1) * DK] = (_rms_norm(hh, norm_g[:, h * DK:(h + 1) * DK])
                                              * _silu(z_ref[sl, h * DK:(h + 1) * DK]))
        return new_cns, m_lasts

    n_chunks = tb // chunk
    cns = [cn_scr[h] for h in heads]
    ms = [m_scr[h] for h in heads]
    pre = precompute(0)
    for ci in range(n_chunks):
        nxt = precompute(ci + 1) if ci + 1 < n_chunks else None
        cns, ms = advance(ci, pre, cns, ms)
        pre = nxt
    for h in heads:
        cn_scr[h] = cns[h]
        m_scr[h] = ms[h]

    @pl.when(t == pl.num_programs(1) - 1)
    def _():
        for h in heads:
            cout_ref[h] = cns[h][:, :DK]
            nout_ref[h] = cns[h][:, DK:]
            mout_ref[h] = ms[h]


def _mlstm(p5, g3, prm, norm_b, c0, n0w, m0w, chunk, tb, name):
    _, bx, t, w = p5.shape
    nh = H_B

    def pspec(group):
        return pl.BlockSpec((None, None, tb, w), lambda b, i, g=group: (g, b, i, 0))

    mat = pl.BlockSpec((None, nh, DK, DK), lambda b, i: (b, 0, 0, 0))
    row = pl.BlockSpec((None, nh, 1, DK), lambda b, i: (b, 0, 0, 0))
    return pl.pallas_call(
        functools.partial(_mlstm_kernel, chunk=chunk, tb=tb),
        grid=(bx, t // tb),
        in_specs=[pspec(4), pspec(5), pspec(6), pspec(7), pspec(8),
                  pl.BlockSpec((None, tb, LANES), lambda b, i: (b, i, 0)),
                  pl.BlockSpec((SUBLANES, LANES), lambda b, i: (0, 0)),
                  pl.BlockSpec((1, w), lambda b, i: (0, 0)),
                  mat, mat, row],
        out_specs=[pl.BlockSpec((None, tb, w), lambda b, i: (b, i, 0)), mat, mat, row],
        out_shape=[jax.ShapeDtypeStruct((bx, t, w), F32),
                   jax.ShapeDtypeStruct((bx, nh, DK, DK), F32),
                   jax.ShapeDtypeStruct((bx, nh, DK, DK), F32),
                   jax.ShapeDtypeStruct((bx, nh, 1, DK), F32)],
        scratch_shapes=[pltpu.VMEM((nh, DK, 2 * DK), F32), pltpu.VMEM((nh, 1, DK), F32)],
        compiler_params=_cparams(("parallel", "arbitrary")),
        name=name,
    )(p5, p5, p5, p5, p5, g3, prm, norm_b, c0, n0w, m0w)


def _sb_tiles(qbs, kts, vts, runs, u_ext, causal):
    tk = kts[0].shape[0]
    zs = [_dot_nt(qb, kt.astype(BF16)) for qb, kt in zip(qbs, kts)]
    sps = [_softplus(z) for z in zs]
    l1ms = [-sp for sp in sps]
    if causal is not None:
        l1ms = [jnp.where(causal, l1m, 0.0) for l1m in l1ms]
    his = [l1m.astype(BF16) for l1m in l1ms]
    los = [(l1m - hi.astype(F32)).astype(BF16) for l1m, hi in zip(l1ms, his)]
    exts = [_dot(hi, u_ext) + _dot(lo, u_ext) for hi, lo in zip(his, los)]
    atts = [jnp.exp((z - sp) + (ext[:, :tk] - l1m) + run[:, :tk])
            for z, sp, ext, l1m, run in zip(zs, sps, exts, l1ms, runs)]
    if causal is not None:
        atts = [jnp.where(causal, att, 0.0) for att in atts]
    contribs = [_dot(att.astype(BF16), vt.astype(BF16)) for att, vt in zip(atts, vts)]
    return [run + ext[:, tk:] for run, ext in zip(runs, exts)], contribs


def _sb_kernel(q_ref, z_ref, kc_ref, vc_ref, k1_ref, v1_ref, k2_ref, v2_ref, kany_ref, vany_ref,
               y_ref, ko_ref, vo_ref, acc_scr, run_scr, kbuf, vbuf, flag, sem,
               *, tq, tk, prev_blocks_per_step, prev_blocks_fixed, prev_token_rows, kgroup, vgroup):
    nh = H_C
    b = pl.program_id(0)
    i = pl.program_id(1)
    r, c = _iota2((tq, tq))
    causal = c < r
    rd, cd = _iota2((tq, tq + LANES))
    u_diag = jnp.where((rd >= cd) | (cd >= tq), 1.0, 0.0).astype(BF16)
    rp, cp = _iota2((tk, tk + LANES))
    u_prev = jnp.where((rp >= cp) | (cp >= tk), 1.0, 0.0).astype(BF16)

    def q_head(h):
        return (q_ref[:, h * DK:(h + 1) * DK] * (DK ** -0.5)).astype(BF16)

    def set_flag():
        best = run_scr[0]
        for h in range(1, nh):
            best = jnp.maximum(best, run_scr[h])
        flag[0] = (jnp.max(best) > SB_DEAD_LOG_WEIGHT).astype(jnp.int32)

    heads = range(nh)
    kts = [kc_ref[:, h * DK:(h + 1) * DK] for h in heads]
    vts = [vc_ref[:, h * DK:(h + 1) * DK] for h in heads]
    for h in heads:
        ko_ref[pl.ds(h, tq, stride=nh), :] = kts[h]
        vo_ref[pl.ds(h, tq, stride=nh), :] = vts[h]
    runs, contribs = _sb_tiles([q_head(h) for h in heads], kts, vts, [jnp.zeros((tq, LANES), F32)] * nh, u_diag,
                               causal)
    for h in heads:
        run_scr[h] = runs[h]
        acc_scr[:, h * DK:(h + 1) * DK] = contribs[h]
    set_flag()

    def prev_tile(ref, h):
        if prev_token_rows:
            return ref[pl.ds(h, tk, stride=nh), :]
        return ref[:, h * DK:(h + 1) * DK]

    def stage(k_ref, v_ref):
        runs, contribs = _sb_tiles([q_head(h) for h in heads], [prev_tile(k_ref, h) for h in heads],
                                   [prev_tile(v_ref, h) for h in heads], [run_scr[h] for h in heads], u_prev, None)
        for h in heads:
            run_scr[h] = runs[h]
            acc_scr[:, h * DK:(h + 1) * DK] += contribs[h]
        set_flag()

    nprev = i * prev_blocks_per_step + prev_blocks_fixed

    @pl.when(jnp.logical_and(nprev >= 1, flag[0] > 0))
    def _():
        stage(k1_ref, v1_ref)

    @pl.when(jnp.logical_and(nprev >= 2, flag[0] > 0))
    def _():
        stage(k2_ref, v2_ref)

    def far_copy(any_ref, group, buf, j, slot):
        start = (nprev - j) * tk
        if prev_token_rows:
            src = any_ref.at[b, pl.ds(pl.multiple_of(start * nh, tk * nh), tk * nh), :]
        else:
            src = any_ref.at[group, b, pl.ds(pl.multiple_of(start, tk), tk), :]
        return pltpu.make_async_copy(src, buf, sem.at[slot])

    def cond(carry):
        j, go = carry
        return jnp.logical_and(j <= nprev, go > 0)

    def body(carry):
        j, _ = carry
        kcopy = far_copy(kany_ref, kgroup, kbuf, j, 0)
        vcopy = far_copy(vany_ref, vgroup, vbuf, j, 1)
        kcopy.start()
        vcopy.start()
        kcopy.wait()
        vcopy.wait()
        stage(kbuf, vbuf)
        return j + 1, flag[0]

    lax.while_loop(cond, body, (jnp.int32(3), flag[0]))
    y_ref[...] = acc_scr[...] * _silu(z_ref[...])


def _sb(p5, kprev, vprev, tq, causal_prev, name):
    _, bx, t, w = p5.shape
    nh = H_C
    tk = SB_TILE if not causal_prev else tq

    def blk(group):
        return pl.BlockSpec((None, None, tq, w), lambda b, i, g=group: (g, b, i, 0))

    if causal_prev:
        def near(group, back):
            return pl.BlockSpec((None, None, tk, w), lambda b, i, g=group: (g, b, jnp.maximum(i - back, 0), 0))
        kprev, vprev = p5, p5
        near_specs = [near(1, 1), near(2, 1), near(1, 2), near(2, 2)]
        buf_shape = (tk, w)
        per_step, fixed = tq // tk, 0
    else:
        nblk = kprev.shape[1] // (tk * nh)

        def near(back):
            return pl.BlockSpec((None, tk * nh, DK), lambda b, i: (b, max(nblk - back, 0), 0))
        near_specs = [near(1), near(1), near(2), near(2)]
        buf_shape = (tk * nh, DK)
        per_step, fixed = 0, nblk
    kern = functools.partial(_sb_kernel, tq=tq, tk=tk, prev_blocks_per_step=per_step, prev_blocks_fixed=fixed,
                             prev_token_rows=not causal_prev, kgroup=1, vgroup=2)
    rows_out = pl.BlockSpec((None, tq * nh, DK), lambda b, i: (b, i, 0))
    return pl.pallas_call(
        kern,
        grid=(bx, t // tq),
        in_specs=[blk(0), blk(3), blk(1), blk(2)] + near_specs
                 + [pl.BlockSpec(memory_space=pl.ANY), pl.BlockSpec(memory_space=pl.ANY)],
        out_specs=[pl.BlockSpec((None, tq, w), lambda b, i: (b, i, 0)), rows_out, rows_out],
        out_shape=[jax.ShapeDtypeStruct((bx, t, w), F32),
                   jax.ShapeDtypeStruct((bx, t * nh, DK), F32),
                   jax.ShapeDtypeStruct((bx, t * nh, DK), F32)],
        scratch_shapes=[pltpu.VMEM((tq, w), F32), pltpu.VMEM((nh, tq, LANES), F32),
                        pltpu.VMEM(buf_shape, F32), pltpu.VMEM(buf_shape, F32),
                        pltpu.SMEM((1,), jnp.int32), pltpu.SemaphoreType.DMA((2,))],
        compiler_params=_cparams(("parallel", "arbitrary")),
        name=name,
    )(p5, p5, p5, p5, kprev, vprev, kprev, vprev, kprev, vprev)


def _s5_kernel(u_ref, z_ref, bb_ref, cc_ref, ar_ref, ai_ref, d_ref, wg_ref, bg_ref, h0r_ref, h0i_ref,
               y_ref, hr_out, hi_out, hs, ys, hr_scr, hi_scr, *, tt):
    t = pl.program_id(1)
    half = GB_D * P_D
    nl = half // LANES

    @pl.when(t == 0)
    def _():
        hr_scr[...] = h0r_ref[...]
        hi_scr[...] = h0i_ref[...]

    for gb in range(NGB_D):
        rows = pl.ds(gb, tt, stride=NGB_D)
        bu = _dot(u_ref[:, gb * LANES:(gb + 1) * LANES].astype(BF16), bb_ref[gb])
        for j in range(2 * nl):
            hs[j, rows, :] = bu[:, j * LANES:(j + 1) * LANES]

    ar = [ar_ref[:, j * LANES:(j + 1) * LANES] for j in range(nl)]
    ai = [ai_ref[:, j * LANES:(j + 1) * LANES] for j in range(nl)]

    def step(s, carry):
        rows = pl.ds(pl.multiple_of(s * NGB_D, NGB_D), NGB_D)
        new = [None] * (2 * nl)
        for j in range(nl):
            hr, hi = carry[j], carry[nl + j]
            new[j] = ar[j] * hr - ai[j] * hi + hs[j, rows, :]
            new[nl + j] = ar[j] * hi + ai[j] * hr + hs[nl + j, rows, :]
        for j in range(2 * nl):
            hs[j, rows, :] = new[j]
        return tuple(new)

    init = tuple(hr_scr[:, j * LANES:(j + 1) * LANES] for j in range(nl)) + tuple(
        hi_scr[:, j * LANES:(j + 1) * LANES] for j in range(nl))
    fin = lax.fori_loop(0, tt, step, init, unroll=8)
    hr = jnp.concatenate(fin[:nl], axis=1)
    hi = jnp.concatenate(fin[nl:], axis=1)
    hr_scr[...] = hr
    hi_scr[...] = hi

    for gb in range(NGB_D):
        rows = pl.ds(gb, tt, stride=NGB_D)
        hcat = jnp.concatenate([hs[j, rows, :] for j in range(2 * nl)], axis=1)
        lanes = slice(gb * LANES, (gb + 1) * LANES)
        ys[:, lanes] = _dot(hcat.astype(BF16), cc_ref[gb]) + d_ref[:, lanes] * u_ref[:, lanes]

    y = ys[...]
    yg = 0.5 * y * (1.0 + jnp.tanh(0.7978845608028654 * (y + 0.044715 * (y * y * y))))
    gate = _sigmoid(_dot(yg.astype(BF16), wg_ref[...]) + bg_ref[...])
    y_ref[...] = yg * gate * _silu(z_ref[...])

    @pl.when(t == pl.num_programs(1) - 1)
    def _():
        hr_out[...] = hr
        hi_out[...] = hi


def _s5(p5, ugroup, zgroup, bb, cc, a_re, a_im, d, w_glu_bf16, b_glu, h0r, h0i, tt, name):
    _, bx, t, w = p5.shape
    half = GB_D * P_D
    st = pl.BlockSpec((None, NGB_D, half), lambda b, i: (b, 0, 0))
    prm = pl.BlockSpec((NGB_D, half), lambda b, i: (0, 0))
    row = pl.BlockSpec((1, w), lambda b, i: (0, 0))
    return pl.pallas_call(
        functools.partial(_s5_kernel, tt=tt),
        grid=(bx, t // tt),
        in_specs=[pl.BlockSpec((None, None, tt, w), lambda b, i: (ugroup, b, i, 0)),
                  pl.BlockSpec((None, None, tt, w), lambda b, i: (zgroup, b, i, 0)),
                  pl.BlockSpec((NGB_D, LANES, 2 * half), lambda b, i: (0, 0, 0)),
                  pl.BlockSpec((NGB_D, 2 * half, LANES), lambda b, i: (0, 0, 0)),
                  prm, prm, row,
                  pl.BlockSpec((w, w), lambda b, i: (0, 0)),
                  row, st, st],
        out_specs=[pl.BlockSpec((None, tt, w), lambda b, i: (b, i, 0)), st, st],
        out_shape=[jax.ShapeDtypeStruct((bx, t, w), F32),
                   jax.ShapeDtypeStruct((bx, NGB_D, half), F32),
                   jax.ShapeDtypeStruct((bx, NGB_D, half), F32)],
        scratch_shapes=[pltpu.VMEM((2 * half // LANES, tt * NGB_D, LANES), F32),
                        pltpu.VMEM((tt, w), F32),
                        pltpu.VMEM((NGB_D, half), F32), pltpu.VMEM((NGB_D, half), F32)],
        compiler_params=_cparams(("parallel", "arbitrary")),
        name=name,
    )(p5, p5, bb, cc, a_re, a_im, d.reshape(1, w), w_glu_bf16, b_glu.reshape(1, w), h0r, h0i)


def _pack_even_weights(w_in, conv_w, a_log, dt_bias, norm_a, ig_bias, fg_bias, norm_b, w_out, ln_g, ln_b):
    d = w_in.shape[0]
    wa = H_A * DK
    wb = H_B * DK
    o = 0
    qkv_a = w_in[:, o:o + 3 * wa]; o += 3 * wa
    z_a = w_in[:, o:o + wa]; o += wa
    gate_a = w_in[:, o:o + 2 * H_A]; o += 2 * H_A
    main_b = w_in[:, o:o + 5 * wb]; o += 5 * wb
    gate_b = w_in[:, o:o + 2 * H_B]
    w_main = jnp.concatenate([qkv_a, z_a, main_b], axis=1).astype(BF16)
    w_main = w_main.reshape(d, 9, W_HEADS).transpose(1, 0, 2)
    n_gate = 2 * H_A + 2 * H_B
    w_gate = jnp.concatenate([gate_a, gate_b, jnp.zeros((d, LANES - n_gate), F32)], axis=1).astype(BF16)[None]
    prm = jnp.zeros((SUBLANES, LANES), F32)
    prm = prm.at[0, H_A:2 * H_A].set(a_log)
    prm = prm.at[1, H_A:2 * H_A].set(dt_bias)
    prm = prm.at[2, 2 * H_A:2 * H_A + H_B].set(ig_bias)
    prm = prm.at[3, 2 * H_A + H_B:2 * H_A + 2 * H_B].set(fg_bias)
    return dict(w_main=w_main, w_gate=w_gate, conv_w=conv_w, prm=prm, norm_a=norm_a.reshape(1, DK),
                norm_b=norm_b.reshape(1, wb), w_out=w_out.astype(BF16), ln_g=ln_g, ln_b=ln_b)


def _even_layer(x, s0, conv0, c0, n0, m0, wp, alpha, tm, tb):
    bx, t, d = x.shape
    m = bx * t
    chunk = min(CHUNK, t)
    x2 = x.reshape(m, d)
    p5 = _proj(x2, wp["w_main"], tm, "ev_proj").reshape(9, bx, t, W_HEADS)
    g3 = _proj(x2, wp["w_gate"], tm, "ev_gate_proj").reshape(bx, t, LANES)
    conv0p = jnp.concatenate([jnp.zeros((bx, SUBLANES - (CONV_W - 1), conv0.shape[2]), F32), conv0], axis=1)
    ya, s_new = _gdn(p5, g3, conv0p, wp["conv_w"], wp["prm"], wp["norm_a"], s0, chunk, tb, "gdn")
    m0w = jnp.broadcast_to(m0[:, :, None, None], (bx, H_B, 1, DK))
    n0w = jnp.broadcast_to(n0[:, :, :, None], (bx, H_B, DK, DK))
    yb, c_new, n_new, m_new = _mlstm(p5, g3, wp["prm"], wp["norm_b"], c0, n0w, m0w, chunk, tb, "mlstm")
    xn = _outproj_ln(ya.reshape(m, -1), yb.reshape(m, -1), x2, wp["w_out"], wp["ln_g"], wp["ln_b"], alpha, min(tm, 256),
                     "ev_outproj_ln")
    conv_new = jnp.concatenate([p5[0, :, t - (CONV_W - 1):], p5[1, :, t - (CONV_W - 1):], p5[2, :, t - (CONV_W - 1):]],
                               axis=-1)
    return xn.reshape(bx, t, d), s_new, conv_new, c_new, n_new[:, :, :, 0], m_new[:, :, 0, 0]


def _pack_odd_weights(w_in, lam_re, lam_im, b_re, b_im, c_re, c_im, d, log_dt, w_glu, b_glu, w_out, ln_g, ln_b):
    dm = w_in.shape[0]
    w_main = w_in.astype(BF16).reshape(dm, 6, W_HEADS).transpose(1, 0, 2)
    dt = jnp.exp(log_dt)[:, None]
    mag = jnp.exp(lam_re * dt)
    a_re, a_im = mag * jnp.cos(lam_im * dt), mag * jnp.sin(lam_im * dt)
    den = lam_re * lam_re + lam_im * lam_im
    f_re = ((a_re - 1.0) * lam_re + a_im * lam_im) / den
    f_im = (a_im * lam_re - (a_re - 1.0) * lam_im) / den
    bb_re = f_re[..., None] * b_re - f_im[..., None] * b_im
    bb_im = f_re[..., None] * b_im + f_im[..., None] * b_re
    half = GB_D * P_D
    eye = jnp.eye(GB_D, dtype=F32)

    def blockdiag_in(bb):
        bbg = bb.reshape(NGB_D, GB_D, P_D, GC_D)
        return jnp.einsum("bgpc,gh->bgchp", bbg, eye).reshape(NGB_D, GB_D * GC_D, half)

    def blockdiag_out(cm):
        cg = cm.reshape(NGB_D, GB_D, GC_D, P_D)
        return jnp.einsum("bgcp,gh->bgphc", cg, eye).reshape(NGB_D, half, GB_D * GC_D)

    bb = jnp.concatenate([blockdiag_in(bb_re), blockdiag_in(bb_im)], axis=2).astype(BF16)
    cc = jnp.concatenate([blockdiag_out(c_re), -blockdiag_out(c_im)], axis=1).astype(BF16)
    return dict(w_main=w_main, bb=bb, cc=cc, a_re=a_re.reshape(NGB_D, half), a_im=a_im.reshape(NGB_D, half),
                d=d, w_glu=w_glu.astype(BF16), b_glu=b_glu, w_out=w_out.astype(BF16), ln_g=ln_g, ln_b=ln_b)


def _odd_layer(x, k_cache, v_cache, h_re0, h_im0, wp, alpha, tm, tq, tt):
    bx, t, d = x.shape
    m = bx * t
    x2 = x.reshape(m, d)
    p5 = _proj(x2, wp["w_main"], tm, "od_proj").reshape(6, bx, t, W_HEADS)
    if k_cache is None:
        yc, kk, vv = _sb(p5, None, None, tq, True, "sb_prompt")
    else:
        past = k_cache.shape[1]
        yc, kk, vv = _sb(p5, k_cache.reshape(bx, past * H_C, DK), v_cache.reshape(bx, past * H_C, DK), tq, False,
                         "sb_sample")
    half = GB_D * P_D
    yd, hr, hi = _s5(p5, 4, 5, wp["bb"], wp["cc"], wp["a_re"], wp["a_im"], wp["d"], wp["w_glu"], wp["b_glu"],
                     h_re0.reshape(bx, NGB_D, half), h_im0.reshape(bx, NGB_D, half), tt, "s5_glu")
    xn = _outproj_ln(yc.reshape(m, -1), yd.reshape(m, -1), x2, wp["w_out"], wp["ln_g"], wp["ln_b"], alpha,
                     min(tm, 256), "od_outproj_ln")
    return (xn.reshape(bx, t, d), kk.reshape(bx, t, H_C, DK), vv.reshape(bx, t, H_C, DK),
            hr.reshape(bx, G_D, P_D), hi.reshape(bx, G_D, P_D))


def kernel(x_prompt, x_sample, state_delta_S, state_delta_conv, state_mlstm_C, state_mlstm_n, state_mlstm_m, cache_sb_k, cache_sb_v, state_s5_re, state_s5_im, ev_w_in, ev_conv_w, ev_a_log, ev_dt_bias, ev_norm_a, ev_ig_bias, ev_fg_bias, ev_norm_b, ev_w_out, ev_ln_g, ev_ln_b, od_w_in, od_lam_re, od_lam_im, od_b_re, od_b_im, od_c_re, od_c_im, od_d, od_log_dt, od_w_glu, od_b_glu, od_w_out, od_ln_g, od_ln_b):
    n_even = ev_w_in.shape[0]
    n_odd = od_w_in.shape[0]
    depth = n_even + n_odd
    alpha = (2 * depth) ** 0.25
    nb, seq, _ = x_prompt.shape
    ns, dseq, _ = x_sample.shape
    xp, xs = x_prompt, x_sample
    outs = [[] for _ in range(18)]
    for layer in range(depth):
        j = layer // 2
        if layer % 2 == 0:
            wp = _pack_even_weights(ev_w_in[j], ev_conv_w[j], ev_a_log[j], ev_dt_bias[j], ev_norm_a[j], ev_ig_bias[j],
                                    ev_fg_bias[j], ev_norm_b[j], ev_w_out[j], ev_ln_g[j], ev_ln_b[j])
            zs = jnp.zeros((nb, H_A, DK, DK), F32)
            xp, s, cv, c, n, m = _even_layer(
                xp, zs, jnp.zeros((nb, CONV_W - 1, 3 * H_A * DK), F32), zs, jnp.zeros((nb, H_B, DK), F32),
                jnp.zeros((nb, H_B), F32), wp, alpha, 512, 256)
            for idx, val in zip((0, 2, 4, 6, 8), (s, cv, c, n, m)):
                outs[idx].append(val)
            xs, s, cv, c, n, m = _even_layer(
                xs, state_delta_S[j], state_delta_conv[j], state_mlstm_C[j], state_mlstm_n[j], state_mlstm_m[j],
                wp, alpha, ns * dseq, dseq)
            for idx, val in zip((1, 3, 5, 7, 9), (s, cv, c, n, m)):
                outs[idx].append(val)
        else:
            wp = _pack_odd_weights(od_w_in[j], od_lam_re[j], od_lam_im[j], od_b_re[j], od_b_im[j], od_c_re[j],
                                   od_c_im[j], od_d[j], od_log_dt[j], od_w_glu[j], od_b_glu[j], od_w_out[j],
                                   od_ln_g[j], od_ln_b[j])
            zst = jnp.zeros((nb, G_D, P_D), F32)
            xp, k, v, hr, hi = _odd_layer(xp, None, None, zst, zst, wp, alpha, 512, SB_TILE, 256)
            for idx, val in zip((10, 12, 14, 16), (k, v, hr, hi)):
                outs[idx].append(val)
            xs, k, v, hr, hi = _odd_layer(xs, cache_sb_k[j], cache_sb_v[j], state_s5_re[j], state_s5_im[j], wp, alpha,
                                          ns * dseq, dseq, dseq)
            for idx, val in zip((11, 13, 15, 17), (k, v, hr, hi)):
                outs[idx].append(val)
    return (xp, xs) + tuple(jnp.stack(o) for o in outs)
```

```python
import functools

import jax
import jax.numpy as jnp
from jax import lax
from jax.experimental import pallas as pl
from jax.experimental.pallas import tpu as pltpu

F32 = jnp.float32
BF16 = jnp.bfloat16

LANES = 128
SUBLANES = 8
CHUNK = 64
CONV_W = 4
H_A = 8
H_B = 8
H_C = 8
DK = 128
GC_D = 16
G_D = 64
P_D = 64
GB_D = 8
NGB_D = G_D // GB_D
W_HEADS = 1024
LN_EPS = 1e-5
NORM_EPS = 1e-6
VMEM_LIMIT = 48 * 1024 * 1024
SB_DEAD_LOG_WEIGHT = -88.0
SB_TILE = 128
PROJ_ROWS = 1024
OUTPROJ_ROWS = 512
MIXER_ROWS = 256


def _cparams(sem):
    return pltpu.CompilerParams(dimension_semantics=sem, vmem_limit_bytes=VMEM_LIMIT)


def _sigmoid(x):
    return 1.0 / (1.0 + jnp.exp(-x))


def _silu(x):
    return x * _sigmoid(x)


def _softplus(x):
    return jnp.maximum(x, 0.0) + jnp.log1p(jnp.exp(-jnp.abs(x)))


def _dot(a, b):
    return jnp.dot(a, b, preferred_element_type=F32)


def _dot_nt(a, b):
    return lax.dot_general(a, b, (((1,), (1,)), ((), ())), preferred_element_type=F32)


def _dot_tn(a, b):
    return lax.dot_general(a, b, (((0,), (0,)), ((), ())), preferred_element_type=F32)


def _split3(x):
    hi = x.astype(BF16)
    r1 = x - hi.astype(F32)
    mid = r1.astype(BF16)
    lo = (r1 - mid.astype(F32)).astype(BF16)
    return hi, mid, lo


def _dot_mask_lhs(mask_bf16, x):
    hi, mid, lo = _split3(x)
    return _dot(mask_bf16, hi) + _dot(mask_bf16, mid) + _dot(mask_bf16, lo)


def _dot_tn_split(x, mask_bf16):
    hi, mid, lo = _split3(x)
    return _dot_tn(hi, mask_bf16) + _dot_tn(mid, mask_bf16) + _dot_tn(lo, mask_bf16)


def _iota2(shape):
    return (lax.broadcasted_iota(jnp.int32, shape, 0), lax.broadcasted_iota(jnp.int32, shape, 1))


def _head(x, h):
    return x[:, h * DK:(h + 1) * DK]


def _proj_kernel(x_ref, w_ref, o_ref):
    o_ref[0] = _dot(x_ref[...].astype(BF16), w_ref[0])


def _proj(x2, w, tm, name):
    m, d = x2.shape
    ng, _, wn = w.shape
    return pl.pallas_call(
        _proj_kernel,
        grid=(m // tm, ng),
        in_specs=[pl.BlockSpec((tm, d), lambda i, j: (i, 0)),
                  pl.BlockSpec((1, d, wn), lambda i, j: (j, 0, 0))],
        out_specs=pl.BlockSpec((1, tm, wn), lambda i, j: (j, i, 0)),
        out_shape=jax.ShapeDtypeStruct((ng, m, wn), F32),
        compiler_params=_cparams(("parallel", "arbitrary")),
        name=name,
    )(x2, w)


def _outproj_ln_kernel(ya_ref, yb_ref, x_ref, wa_ref, wb_ref, g_ref, b_ref, o_ref, *, alpha):
    acc = _dot(ya_ref[...].astype(BF16), wa_ref[...]) + _dot(yb_ref[...].astype(BF16), wb_ref[...])
    r = alpha * x_ref[...] + acc
    mu = jnp.mean(r, axis=-1, keepdims=True)
    c = r - mu
    var = jnp.mean(c * c, axis=-1, keepdims=True)
    o_ref[...] = c * lax.rsqrt(var + LN_EPS) * g_ref[...] + b_ref[...]


def _outproj_ln(ya, yb, x2, w_out_bf16, ln_g, ln_b, alpha, tm, name):
    m, d = x2.shape
    wa = ya.shape[1]
    wb = yb.shape[1]
    return pl.pallas_call(
        functools.partial(_outproj_ln_kernel, alpha=alpha),
        grid=(m // tm,),
        in_specs=[pl.BlockSpec((tm, wa), lambda i: (i, 0)),
                  pl.BlockSpec((tm, wb), lambda i: (i, 0)),
                  pl.BlockSpec((tm, d), lambda i: (i, 0)),
                  pl.BlockSpec((wa, d), lambda i: (0, 0)),
                  pl.BlockSpec((wb, d), lambda i: (wa // wb, 0)),
                  pl.BlockSpec((1, d), lambda i: (0, 0)),
                  pl.BlockSpec((1, d), lambda i: (0, 0))],
        out_specs=pl.BlockSpec((tm, d), lambda i: (i, 0)),
        out_shape=jax.ShapeDtypeStruct((m, d), F32),
        compiler_params=_cparams(("parallel",)),
        name=name,
    )(ya, yb, x2, w_out_bf16, w_out_bf16, ln_g.reshape(1, d), ln_b.reshape(1, d))


def _conv_silu(x_ref, scr, w_ref, tb):
    scr[SUBLANES:SUBLANES + tb, :] = x_ref[...]
    w = w_ref[...]
    base = SUBLANES - (CONV_W - 1)
    acc = scr[base:base + tb, :] * w[0:1, :]
    for i in range(1, CONV_W):
        acc = acc + scr[base + i:base + i + tb, :] * w[i:i + 1, :]
    scr[0:SUBLANES, :] = scr[tb:tb + SUBLANES, :]
    return _silu(acc)


def _l2norm(x):
    return x * lax.rsqrt(jnp.sum(x * x, axis=-1, keepdims=True) + NORM_EPS)


def _rms_norm(x, g):
    return x * lax.rsqrt(jnp.mean(x * x, axis=-1, keepdims=True) + NORM_EPS) * g


def _unit_lower_inverses(n_mats, length):
    r, c = _iota2((length, length))
    eye = jnp.where(r == c, 1.0, 0.0)
    xs = [eye + n for n in n_mats]
    ps = list(n_mats)
    for _ in range(length.bit_length() - 2):
        pbs = [p.astype(BF16) for p in ps]
        ps = [_dot(pb, pb) for pb in pbs]
        xs = [x + _dot(x.astype(BF16), p.astype(BF16)) for x, p in zip(xs, ps)]
    return xs


def _gdn_kernel(qp_ref, kp_ref, vp_ref, z_ref, gt_ref, c0q_ref, c0k_ref, c0v_ref,
                cwq_ref, cwk_ref, cwv_ref, prm_ref, norm_ref, s0_ref,
                y_ref, sout_ref, s_scr, xq_scr, xk_scr, xv_scr, *, chunk, tb):
    t = pl.program_id(1)
    nh = H_A

    @pl.when(t == 0)
    def _():
        s_scr[...] = s0_ref[...]
        xq_scr[0:SUBLANES, :] = c0q_ref[...]
        xk_scr[0:SUBLANES, :] = c0k_ref[...]
        xv_scr[0:SUBLANES, :] = c0v_ref[...]

    q_all = _conv_silu(qp_ref, xq_scr, cwq_ref, tb)
    k_all = _conv_silu(kp_ref, xk_scr, cwk_ref, tb)
    v_all = _conv_silu(vp_ref, xv_scr, cwv_ref, tb)
    q_h = [_l2norm(_head(q_all, h)) * (DK ** -0.5) for h in range(nh)]
    k_h = [_l2norm(_head(k_all, h)) for h in range(nh)]

    gt = gt_ref[...]
    prm = prm_ref[...]
    beta_all = _sigmoid(gt)
    g_all = -jnp.exp(prm[0:1, :]) * _softplus(gt + prm[1:2, :])

    r, c = _iota2((chunk, chunk))
    incl = r >= c
    strict = r > c
    a_incl = jnp.where(incl, 1.0, 0.0).astype(BF16)
    a_incl_t = jnp.where(r <= c, 1.0, 0.0).astype(BF16)
    norm_g = norm_ref[...]

    def precompute(ci):
        sl = slice(ci * chunk, (ci + 1) * chunk)
        g_c = g_all[sl]
        gc_cols = _dot_mask_lhs(a_incl, g_c)
        gc_rows = _dot_tn_split(g_c, a_incl_t)
        beta_c = beta_all[sl]
        heads = []
        for h in range(nh):
            gcol = gc_cols[:, nh + h:nh + h + 1]
            grow = gc_rows[nh + h:nh + h + 1, :]
            bc = beta_c[:, h:h + 1]
            qc, kc, vc = q_h[h][sl], k_h[h][sl], _head(v_all, h)[sl]
            decay = jnp.where(incl, jnp.exp(gcol - grow), 0.0)
            kb = kc.astype(BF16)
            qkk = _dot_nt(jnp.concatenate([qc, kc], axis=0).astype(BF16), kb)
            gam = jnp.exp(gcol)
            g_last = gcol[chunk - 1:chunk, :]
            heads.append(dict(
                n=jnp.where(strict, -(bc * qkk[chunk:] * decay), 0.0),
                qk=(qkk[:chunk] * decay).astype(BF16),
                rhs=jnp.concatenate([vc * bc, kc * (bc * gam)], axis=-1).astype(BF16),
                q_dec=qc * gam,
                k_dec=(kc * jnp.exp(g_last - gcol)).astype(BF16),
                s_decay=jnp.exp(g_last)))
        t_inv = _unit_lower_inverses([hd["n"] for hd in heads], chunk)
        for hd, ti in zip(heads, t_inv):
            sol = _dot(ti.astype(BF16), hd["rhs"])
            hd["u"] = sol[:, :DK]
            hd["wq"] = jnp.concatenate([sol[:, DK:], hd["q_dec"]], axis=0).astype(BF16)
        return heads

    def advance(ci, heads, states):
        sl = slice(ci * chunk, (ci + 1) * chunk)
        wss = [_dot(hd["wq"], s.astype(BF16)) for hd, s in zip(heads, states)]
        es = [(hd["u"] - ws[:chunk]).astype(BF16) for hd, ws in zip(heads, wss)]
        new_states = [hd["s_decay"] * s + _dot_tn(hd["k_dec"], e) for hd, s, e in zip(heads, states, es)]
        outs = [ws[chunk:] + _dot(hd["qk"], e) for hd, ws, e in zip(heads, wss, es)]
        for h in range(nh):
            y_ref[sl, h * DK:(h + 1) * DK] = (_rms_norm(outs[h], norm_g)
                                              * _silu(z_ref[sl, h * DK:(h + 1) * DK])).astype(y_ref.dtype)
        return new_states

    n_chunks = tb // chunk
    states = [s_scr[h] for h in range(nh)]
    pre = precompute(0)
    for ci in range(n_chunks):
        nxt = precompute(ci + 1) if ci + 1 < n_chunks else None
        states = advance(ci, pre, states)
        pre = nxt
    for h in range(nh):
        s_scr[h] = states[h]

    @pl.when(t == pl.num_programs(1) - 1)
    def _():
        for h in range(nh):
            sout_ref[h] = states[h]


def _gdn(p5, g3, conv0p, conv_w, prm, norm_a, s0, chunk, tb, name):
    _, bx, t, w = p5.shape
    nh = H_A

    def pspec(group):
        return pl.BlockSpec((None, None, tb, w), lambda b, i, g=group: (g, b, i, 0))

    def c0spec(group):
        return pl.BlockSpec((None, SUBLANES, w), lambda b, i, g=group: (b, 0, g))

    def cwspec(group):
        return pl.BlockSpec((CONV_W, w), lambda b, i, g=group: (0, g))

    return pl.pallas_call(
        functools.partial(_gdn_kernel, chunk=chunk, tb=tb),
        grid=(bx, t // tb),
        in_specs=[pspec(0), pspec(1), pspec(2), pspec(3),
                  pl.BlockSpec((None, tb, LANES), lambda b, i: (b, i, 0)),
                  c0spec(0), c0spec(1), c0spec(2),
                  cwspec(0), cwspec(1), cwspec(2),
                  pl.BlockSpec((SUBLANES, LANES), lambda b, i: (0, 0)),
                  pl.BlockSpec((1, DK), lambda b, i: (0, 0)),
                  pl.BlockSpec((None, nh, DK, DK), lambda b, i: (b, 0, 0, 0))],
        out_specs=[pl.BlockSpec((None, tb, w), lambda b, i: (b, i, 0)),
                   pl.BlockSpec((None, nh, DK, DK), lambda b, i: (b, 0, 0, 0))],
        out_shape=[jax.ShapeDtypeStruct((bx, t, w), BF16),
                   jax.ShapeDtypeStruct((bx, nh, DK, DK), F32)],
        scratch_shapes=[pltpu.VMEM((nh, DK, DK), F32),
                        pltpu.VMEM((tb + SUBLANES, w), F32),
                        pltpu.VMEM((tb + SUBLANES, w), F32),
                        pltpu.VMEM((tb + SUBLANES, w), F32)],
        compiler_params=_cparams(("parallel", "arbitrary")),
        name=name,
    )(p5, p5, p5, p5, g3, conv0p, conv0p, conv0p, conv_w, conv_w, conv_w, prm, norm_a, s0)


def _mlstm_kernel(q_ref, k_ref, v_ref, o_ref, z_ref, gt_ref, prm_ref, norm_ref, c0_ref, n0_ref, m0_ref,
                  y_ref, cout_ref, nout_ref, mout_ref, cn_scr, m_scr, *, chunk, tb):
    t = pl.program_id(1)
    nh = H_B
    heads = range(nh)
    ig_lane = 2 * H_A
    lf_lane = 2 * H_A + H_B

    @pl.when(t == 0)
    def _():
        cn_scr[:, :, :DK] = c0_ref[...]
        cn_scr[:, :, DK:] = n0_ref[...]
        m_scr[...] = m0_ref[...]

    gt = gt_ref[...]
    prm = prm_ref[...]
    ig_all = gt + prm[2:3, :]
    lf_all = -_softplus(-(gt + prm[3:4, :]))

    r, c = _iota2((chunk, chunk))
    incl = r >= c
    a_incl = jnp.where(incl, 1.0, 0.0).astype(BF16)
    a_incl_t = jnp.where(r <= c, 1.0, 0.0).astype(BF16)
    eye = jnp.where(r == c, 1.0, 0.0).astype(BF16)
    ones_w = jnp.ones((chunk, DK), F32)
    ones_b = jnp.ones((chunk, DK), BF16)
    norm_g = norm_ref[...]

    def precompute(ci):
        sl = slice(ci * chunk, (ci + 1) * chunk)
        lf_c = lf_all[sl]
        ig_c = ig_all[sl]
        b_cols = _dot_mask_lhs(a_incl, lf_c)
        b_rows = _dot_tn_split(lf_c, a_incl_t)
        ig_rows = _dot_tn_split(ig_c, eye)
        bcols = [b_cols[:, lf_lane + h:lf_lane + h + 1] for h in heads]
        d_logs = [jnp.where(incl, bcols[h] - b_rows[lf_lane + h:lf_lane + h + 1, :]
                            + ig_rows[ig_lane + h:ig_lane + h + 1, :], -jnp.inf) for h in heads]
        m_intras = [jnp.max(d, axis=-1, keepdims=True) for d in d_logs]
        qbs = [q_ref[sl, h * DK:(h + 1) * DK].astype(BF16) for h in heads]
        kcs = [k_ref[sl, h * DK:(h + 1) * DK] * (DK ** -0.5) for h in heads]
        v1s = [jnp.concatenate([v_ref[sl, h * DK:(h + 1) * DK].astype(BF16), ones_b], axis=1) for h in heads]
        qks = [_dot_nt(qb, kc.astype(BF16)) for qb, kc in zip(qbs, kcs)]
        pq0s = [(jnp.exp(d - mi) * qk).astype(BF16) for d, mi, qk in zip(d_logs, m_intras, qks)]
        avs = [_dot(pq0, v1) for pq0, v1 in zip(pq0s, v1s)]
        return [dict(b_w=bcols[h] * ones_w, mi_w=m_intras[h] * ones_w,
                     tail_w=(bcols[h][chunk - 1:chunk, :] - bcols[h] + ig_c[:, ig_lane + h:ig_lane + h + 1]) * ones_w,
                     qb=qbs[h], kc=kcs[h], v1=v1s[h], av=avs[h]) for h in heads]

    def advance(ci, pre, cns, ms):
        sl = slice(ci * chunk, (ci + 1) * chunk)
        qcns = [_dot(pre[h]["qb"], cns[h].astype(BF16)) for h in heads]
        bms = [pre[h]["b_w"] + ms[h] for h in heads]
        m_ts = [jnp.maximum(bms[h], pre[h]["mi_w"]) for h in heads]
        corrs = [jnp.exp(pre[h]["mi_w"] - m_ts[h]) for h in heads]
        inters = [jnp.exp(bms[h] - m_ts[h]) for h in heads]
        m_lasts = [m_t[chunk - 1:chunk, :] for m_t in m_ts]
        i_lasts = [inter[chunk - 1:chunk, :] for inter in inters]
        kps = [(pre[h]["kc"] * jnp.exp(pre[h]["tail_w"] - m_lasts[h])).astype(BF16) for h in heads]
        new_cns = [jnp.concatenate([i_lasts[h], i_lasts[h]], axis=1) * cns[h] + _dot_tn(kps[h], pre[h]["v1"])
                   for h in heads]
        for h in heads:
            num = inters[h] * qcns[h][:, :DK] + corrs[h] * pre[h]["av"][:, :DK]
            den = inters[h] * qcns[h][:, DK:] + corrs[h] * pre[h]["av"][:, DK:]
            hh = num / jnp.maximum(jnp.abs(den), jnp.exp(-m_ts[h]))
            hh = _sigmoid(o_ref[sl, h * DK:(h + 1) * DK]) * hh
            y_ref[sl, h * DK:(h + 1) * DK] = (_rms_norm(hh, norm_g[:, h * DK:(h + 1) * DK])
                                              * _silu(z_ref[sl, h * DK:(h + 1) * DK])).astype(y_ref.dtype)
        return new_cns, m_lasts

    n_chunks = tb // chunk
    cns = [cn_scr[h] for h in heads]
    ms = [m_scr[h] for h in heads]
    pre = precompute(0)
    for ci in range(n_chunks):
        nxt = precompute(ci + 1) if ci + 1 < n_chunks else None
        cns, ms = advance(ci, pre, cns, ms)
        pre = nxt
    for h in heads:
        cn_scr[h] = cns[h]
        m_scr[h] = ms[h]

    @pl.when(t == pl.num_programs(1) - 1)
    def _():
        for h in heads:
            cout_ref[h] = cns[h][:, :DK]
            nout_ref[h] = cns[h][:, DK:]
            mout_ref[h] = ms[h]


def _mlstm(p5, g3, prm, norm_b, c0, n0w, m0w, chunk, tb, name):
    _, bx, t, w = p5.shape
    nh = H_B

    def pspec(group):
        return pl.BlockSpec((None, None, tb, w), lambda b, i, g=group: (g, b, i, 0))

    mat = pl.BlockSpec((None, nh, DK, DK), lambda b, i: (b, 0, 0, 0))
    row = pl.BlockSpec((None, nh, 1, DK), lambda b, i: (b, 0, 0, 0))
    return pl.pallas_call(
        functools.partial(_mlstm_kernel, chunk=chunk, tb=tb),
        grid=(bx, t // tb),
        in_specs=[pspec(4), pspec(5), pspec(6), pspec(7), pspec(8),
                  pl.BlockSpec((None, tb, LANES), lambda b, i: (b, i, 0)),
                  pl.BlockSpec((SUBLANES, LANES), lambda b, i: (0, 0)),
                  pl.BlockSpec((1, w), lambda b, i: (0, 0)),
                  mat, mat, row],
        out_specs=[pl.BlockSpec((None, tb, w), lambda b, i: (b, i, 0)), mat, mat, row],
        out_shape=[jax.ShapeDtypeStruct((bx, t, w), BF16),
                   jax.ShapeDtypeStruct((bx, nh, DK, DK), F32),
                   jax.ShapeDtypeStruct((bx, nh, DK, DK), F32),
                   jax.ShapeDtypeStruct((bx, nh, 1, DK), F32)],
        scratch_shapes=[pltpu.VMEM((nh, DK, 2 * DK), F32), pltpu.VMEM((nh, 1, DK), F32)],
        compiler_params=_cparams(("parallel", "arbitrary")),
        name=name,
    )(p5, p5, p5, p5, p5, g3, prm, norm_b, c0, n0w, m0w)


def _sb_tiles(qbs, kts, vts, runs, u_ext, causal):
    tk = kts[0].shape[0]
    zs = [_dot_nt(qb, kt.astype(BF16)) for qb, kt in zip(qbs, kts)]
    sps = [_softplus(z) for z in zs]
    l1ms = [-sp for sp in sps]
    if causal is not None:
        l1ms = [jnp.where(causal, l1m, 0.0) for l1m in l1ms]
    his = [l1m.astype(BF16) for l1m in l1ms]
    los = [(l1m - hi.astype(F32)).astype(BF16) for l1m, hi in zip(l1ms, his)]
    exts = [_dot(hi, u_ext) + _dot(lo, u_ext) for hi, lo in zip(his, los)]
    atts = [jnp.exp((z - sp) + (ext[:, :tk] - l1m) + run[:, :tk])
            for z, sp, ext, l1m, run in zip(zs, sps, exts, l1ms, runs)]
    if causal is not None:
        atts = [jnp.where(causal, att, 0.0) for att in atts]
    contribs = [_dot(att.astype(BF16), vt.astype(BF16)) for att, vt in zip(atts, vts)]
    return [run + ext[:, tk:] for run, ext in zip(runs, exts)], contribs


def _sb_kernel(q_ref, z_ref, kc_ref, vc_ref, k1_ref, v1_ref, k2_ref, v2_ref, kany_ref, vany_ref,
               y_ref, ko_ref, vo_ref, acc_scr, run_scr, kbuf, vbuf, flag, sem,
               *, tq, tk, prev_blocks_per_step, prev_blocks_fixed, prev_token_rows, kgroup, vgroup):
    nh = H_C
    b = pl.program_id(0)
    i = pl.program_id(1)
    r, c = _iota2((tq, tq))
    causal = c < r
    rd, cd = _iota2((tq, tq + LANES))
    u_diag = jnp.where((rd >= cd) | (cd >= tq), 1.0, 0.0).astype(BF16)
    rp, cp = _iota2((tk, tk + LANES))
    u_prev = jnp.where((rp >= cp) | (cp >= tk), 1.0, 0.0).astype(BF16)

    def q_head(h):
        return (q_ref[:, h * DK:(h + 1) * DK] * (DK ** -0.5)).astype(BF16)

    def set_flag():
        best = run_scr[0]
        for h in range(1, nh):
            best = jnp.maximum(best, run_scr[h])
        flag[0] = (jnp.max(best) > SB_DEAD_LOG_WEIGHT).astype(jnp.int32)

    heads = range(nh)
    kts = [kc_ref[:, h * DK:(h + 1) * DK] for h in heads]
    vts = [vc_ref[:, h * DK:(h + 1) * DK] for h in heads]
    for h in heads:
        ko_ref[pl.ds(h, tq, stride=nh), :] = kts[h]
        vo_ref[pl.ds(h, tq, stride=nh), :] = vts[h]
    runs, contribs = _sb_tiles([q_head(h) for h in heads], kts, vts, [jnp.zeros((tq, LANES), F32)] * nh, u_diag,
                               causal)
    for h in heads:
        run_scr[h] = runs[h]
        acc_scr[:, h * DK:(h + 1) * DK] = contribs[h]
    set_flag()

    def prev_tile(ref, h):
        if prev_token_rows:
            return ref[pl.ds(h, tk, stride=nh), :]
        return ref[:, h * DK:(h + 1) * DK]

    def stage(k_ref, v_ref):
        runs, contribs = _sb_tiles([q_head(h) for h in heads], [prev_tile(k_ref, h) for h in heads],
                                   [prev_tile(v_ref, h) for h in heads], [run_scr[h] for h in heads], u_prev, None)
        for h in heads:
            run_scr[h] = runs[h]
            acc_scr[:, h * DK:(h + 1) * DK] += contribs[h]
        set_flag()

    nprev = i * prev_blocks_per_step + prev_blocks_fixed

    @pl.when(jnp.logical_and(nprev >= 1, flag[0] > 0))
    def _():
        stage(k1_ref, v1_ref)

    @pl.when(jnp.logical_and(nprev >= 2, flag[0] > 0))
    def _():
        stage(k2_ref, v2_ref)

    def far_copy(any_ref, group, buf, j, slot):
        start = (nprev - j) * tk
        if prev_token_rows:
            src = any_ref.at[b, pl.ds(pl.multiple_of(start * nh, tk * nh), tk * nh), :]
        else:
            src = any_ref.at[group, b, pl.ds(pl.multiple_of(start, tk), tk), :]
        return pltpu.make_async_copy(src, buf, sem.at[slot])

    def cond(carry):
        j, go = carry
        return jnp.logical_and(j <= nprev, go > 0)

    def body(carry):
        j, _ = carry
        kcopy = far_copy(kany_ref, kgroup, kbuf, j, 0)
        vcopy = far_copy(vany_ref, vgroup, vbuf, j, 1)
        kcopy.start()
        vcopy.start()
        kcopy.wait()
        vcopy.wait()
        stage(kbuf, vbuf)
        return j + 1, flag[0]

    lax.while_loop(cond, body, (jnp.int32(3), flag[0]))
    y_ref[...] = (acc_scr[...] * _silu(z_ref[...])).astype(y_ref.dtype)


def _sb(p5, kprev, vprev, tq, causal_prev, name):
    _, bx, t, w = p5.shape
    nh = H_C
    tk = SB_TILE if not causal_prev else tq

    def blk(group):
        return pl.BlockSpec((None, None, tq, w), lambda b, i, g=group: (g, b, i, 0))

    if causal_prev:
        def near(group, back):
            return pl.BlockSpec((None, None, tk, w), lambda b, i, g=group: (g, b, jnp.maximum(i - back, 0), 0))
        kprev, vprev = p5, p5
        near_specs = [near(1, 1), near(2, 1), near(1, 2), near(2, 2)]
        buf_shape = (tk, w)
        per_step, fixed = tq // tk, 0
    else:
        nblk = kprev.shape[1] // (tk * nh)

        def near(back):
            return pl.BlockSpec((None, tk * nh, DK), lambda b, i: (b, max(nblk - back, 0), 0))
        near_specs = [near(1), near(1), near(2), near(2)]
        buf_shape = (tk * nh, DK)
        per_step, fixed = 0, nblk
    kern = functools.partial(_sb_kernel, tq=tq, tk=tk, prev_blocks_per_step=per_step, prev_blocks_fixed=fixed,
                             prev_token_rows=not causal_prev, kgroup=1, vgroup=2)
    rows_out = pl.BlockSpec((None, tq * nh, DK), lambda b, i: (b, i, 0))
    return pl.pallas_call(
        kern,
        grid=(bx, t // tq),
        in_specs=[blk(0), blk(3), blk(1), blk(2)] + near_specs
                 + [pl.BlockSpec(memory_space=pl.ANY), pl.BlockSpec(memory_space=pl.ANY)],
        out_specs=[pl.BlockSpec((None, tq, w), lambda b, i: (b, i, 0)), rows_out, rows_out],
        out_shape=[jax.ShapeDtypeStruct((bx, t, w), BF16),
                   jax.ShapeDtypeStruct((bx, t * nh, DK), F32),
                   jax.ShapeDtypeStruct((bx, t * nh, DK), F32)],
        scratch_shapes=[pltpu.VMEM((tq, w), F32), pltpu.VMEM((nh, tq, LANES), F32),
                        pltpu.VMEM(buf_shape, F32), pltpu.VMEM(buf_shape, F32),
                        pltpu.SMEM((1,), jnp.int32), pltpu.SemaphoreType.DMA((2,))],
        compiler_params=_cparams(("parallel", "arbitrary")),
        name=name,
    )(p5, p5, p5, p5, kprev, vprev, kprev, vprev, kprev, vprev)


def _s5_kernel(u_ref, z_ref, bb_ref, cc_ref, ar_ref, ai_ref, d_ref, wg_ref, bg_ref, h0r_ref, h0i_ref,
               y_ref, hr_out, hi_out, hs, ys, hr_scr, hi_scr, *, tt):
    t = pl.program_id(1)
    half = GB_D * P_D
    nl = half // LANES

    @pl.when(t == 0)
    def _():
        hr_scr[...] = h0r_ref[...]
        hi_scr[...] = h0i_ref[...]

    for gb in range(NGB_D):
        rows = pl.ds(gb, tt, stride=NGB_D)
        bu = _dot(u_ref[:, gb * LANES:(gb + 1) * LANES].astype(BF16), bb_ref[gb])
        for j in range(2 * nl):
            hs[j, rows, :] = bu[:, j * LANES:(j + 1) * LANES]

    ar = [ar_ref[:, j * LANES:(j + 1) * LANES] for j in range(nl)]
    ai = [ai_ref[:, j * LANES:(j + 1) * LANES] for j in range(nl)]

    def step(s, carry):
        rows = pl.ds(pl.multiple_of(s * NGB_D, NGB_D), NGB_D)
        new = [None] * (2 * nl)
        for j in range(nl):
            hr, hi = carry[j], carry[nl + j]
            new[j] = ar[j] * hr - ai[j] * hi + hs[j, rows, :]
            new[nl + j] = ar[j] * hi + ai[j] * hr + hs[nl + j, rows, :]
        for j in range(2 * nl):
            hs[j, rows, :] = new[j]
        return tuple(new)

    init = tuple(hr_scr[:, j * LANES:(j + 1) * LANES] for j in range(nl)) + tuple(
        hi_scr[:, j * LANES:(j + 1) * LANES] for j in range(nl))
    fin = lax.fori_loop(0, tt, step, init, unroll=8)
    hr = jnp.concatenate(fin[:nl], axis=1)
    hi = jnp.concatenate(fin[nl:], axis=1)
    hr_scr[...] = hr
    hi_scr[...] = hi

    for gb in range(NGB_D):
        rows = pl.ds(gb, tt, stride=NGB_D)
        hcat = jnp.concatenate([hs[j, rows, :] for j in range(2 * nl)], axis=1)
        lanes = slice(gb * LANES, (gb + 1) * LANES)
        ys[:, lanes] = _dot(hcat.astype(BF16), cc_ref[gb]) + d_ref[:, lanes] * u_ref[:, lanes]

    y = ys[...]
    yg = 0.5 * y * (1.0 + jnp.tanh(0.7978845608028654 * (y + 0.044715 * (y * y * y))))
    gate = _sigmoid(_dot(yg.astype(BF16), wg_ref[...]) + bg_ref[...])
    y_ref[...] = (yg * gate * _silu(z_ref[...])).astype(y_ref.dtype)

    @pl.when(t == pl.num_programs(1) - 1)
    def _():
        hr_out[...] = hr
        hi_out[...] = hi


def _s5(p5, ugroup, zgroup, bb, cc, a_re, a_im, d, w_glu_bf16, b_glu, h0r, h0i, tt, name):
    _, bx, t, w = p5.shape
    half = GB_D * P_D
    st = pl.BlockSpec((None, NGB_D, half), lambda b, i: (b, 0, 0))
    prm = pl.BlockSpec((NGB_D, half), lambda b, i: (0, 0))
    row = pl.BlockSpec((1, w), lambda b, i: (0, 0))
    return pl.pallas_call(
        functools.partial(_s5_kernel, tt=tt),
        grid=(bx, t // tt),
        in_specs=[pl.BlockSpec((None, None, tt, w), lambda b, i: (ugroup, b, i, 0)),
                  pl.BlockSpec((None, None, tt, w), lambda b, i: (zgroup, b, i, 0)),
                  pl.BlockSpec((NGB_D, LANES, 2 * half), lambda b, i: (0, 0, 0)),
                  pl.BlockSpec((NGB_D, 2 * half, LANES), lambda b, i: (0, 0, 0)),
                  prm, prm, row,
                  pl.BlockSpec((w, w), lambda b, i: (0, 0)),
                  row, st, st],
        out_specs=[pl.BlockSpec((None, tt, w), lambda b, i: (b, i, 0)), st, st],
        out_shape=[jax.ShapeDtypeStruct((bx, t, w), BF16),
                   jax.ShapeDtypeStruct((bx, NGB_D, half), F32),
                   jax.ShapeDtypeStruct((bx, NGB_D, half), F32)],
        scratch_shapes=[pltpu.VMEM((2 * half // LANES, tt * NGB_D, LANES), F32),
                        pltpu.VMEM((tt, w), F32),
                        pltpu.VMEM((NGB_D, half), F32), pltpu.VMEM((NGB_D, half), F32)],
        compiler_params=_cparams(("parallel", "arbitrary")),
        name=name,
    )(p5, p5, bb, cc, a_re, a_im, d.reshape(1, w), w_glu_bf16, b_glu.reshape(1, w), h0r, h0i)


def _pack_even_weights(w_in, conv_w, a_log, dt_bias, norm_a, ig_bias, fg_bias, norm_b, w_out, ln_g, ln_b):
    d = w_in.shape[0]
    wa = H_A * DK
    wb = H_B * DK
    o = 0
    qkv_a = w_in[:, o:o + 3 * wa]; o += 3 * wa
    z_a = w_in[:, o:o + wa]; o += wa
    gate_a = w_in[:, o:o + 2 * H_A]; o += 2 * H_A
    main_b = w_in[:, o:o + 5 * wb]; o += 5 * wb
    gate_b = w_in[:, o:o + 2 * H_B]
    w_main = jnp.concatenate([qkv_a, z_a, main_b], axis=1).astype(BF16)
    w_main = w_main.reshape(d, 9, W_HEADS).transpose(1, 0, 2)
    n_gate = 2 * H_A + 2 * H_B
    w_gate = jnp.concatenate([gate_a, gate_b, jnp.zeros((d, LANES - n_gate), F32)], axis=1).astype(BF16)[None]
    prm = jnp.zeros((SUBLANES, LANES), F32)
    prm = prm.at[0, H_A:2 * H_A].set(a_log)
    prm = prm.at[1, H_A:2 * H_A].set(dt_bias)
    prm = prm.at[2, 2 * H_A:2 * H_A + H_B].set(ig_bias)
    prm = prm.at[3, 2 * H_A + H_B:2 * H_A + 2 * H_B].set(fg_bias)
    return dict(w_main=w_main, w_gate=w_gate, conv_w=conv_w, prm=prm, norm_a=norm_a.reshape(1, DK),
                norm_b=norm_b.reshape(1, wb), w_out=w_out.astype(BF16), ln_g=ln_g, ln_b=ln_b)


def _tile(n, preferred):
    tile = min(n, preferred)
    assert n % tile == 0, (n, preferred)
    return tile


def _even_layer(x, s0, conv0, c0, n0, m0, wp, alpha):
    bx, t, d = x.shape
    m = bx * t
    chunk = min(CHUNK, t)
    tm = _tile(m, PROJ_ROWS)
    tb = _tile(t, MIXER_ROWS)
    x2 = x.reshape(m, d)
    p5 = _proj(x2, wp["w_main"], tm, "ev_proj").reshape(9, bx, t, W_HEADS)
    g3 = _proj(x2, wp["w_gate"], tm, "ev_gate_proj").reshape(bx, t, LANES)
    conv0p = jnp.concatenate([jnp.zeros((bx, SUBLANES - (CONV_W - 1), conv0.shape[2]), F32), conv0], axis=1)
    ya, s_new = _gdn(p5, g3, conv0p, wp["conv_w"], wp["prm"], wp["norm_a"], s0, chunk, tb, "gdn")
    m0w = jnp.broadcast_to(m0[:, :, None, None], (bx, H_B, 1, DK))
    n0w = jnp.broadcast_to(n0[:, :, :, None], (bx, H_B, DK, DK))
    yb, c_new, n_new, m_new = _mlstm(p5, g3, wp["prm"], wp["norm_b"], c0, n0w, m0w, chunk, tb, "mlstm")
    xn = _outproj_ln(ya.reshape(m, -1), yb.reshape(m, -1), x2, wp["w_out"], wp["ln_g"], wp["ln_b"], alpha,
                     _tile(m, OUTPROJ_ROWS), "ev_outproj_ln")
    conv_new = jnp.concatenate([p5[0, :, t - (CONV_W - 1):], p5[1, :, t - (CONV_W - 1):], p5[2, :, t - (CONV_W - 1):]],
                               axis=-1)
    return xn.reshape(bx, t, d), s_new, conv_new, c_new, n_new[:, :, :, 0], m_new[:, :, 0, 0]


def _pack_odd_weights(w_in, lam_re, lam_im, b_re, b_im, c_re, c_im, d, log_dt, w_glu, b_glu, w_out, ln_g, ln_b):
    dm = w_in.shape[0]
    w_main = w_in.astype(BF16).reshape(dm, 6, W_HEADS).transpose(1, 0, 2)
    dt = jnp.exp(log_dt)[:, None]
    mag = jnp.exp(lam_re * dt)
    a_re, a_im = mag * jnp.cos(lam_im * dt), mag * jnp.sin(lam_im * dt)
    den = lam_re * lam_re + lam_im * lam_im
    f_re = ((a_re - 1.0) * lam_re + a_im * lam_im) / den
    f_im = (a_im * lam_re - (a_re - 1.0) * lam_im) / den
    bb_re = f_re[..., None] * b_re - f_im[..., None] * b_im
    bb_im = f_re[..., None] * b_im + f_im[..., None] * b_re
    half = GB_D * P_D
    eye = jnp.eye(GB_D, dtype=F32)

    def blockdiag_in(bb):
        bbg = bb.reshape(NGB_D, GB_D, P_D, GC_D)
        return jnp.einsum("bgpc,gh->bgchp", bbg, eye).reshape(NGB_D, GB_D * GC_D, half)

    def blockdiag_out(cm):
        cg = cm.reshape(NGB_D, GB_D, GC_D, P_D)
        return jnp.einsum("bgcp,gh->bgphc", cg, eye).reshape(NGB_D, half, GB_D * GC_D)

    bb = jnp.concatenate([blockdiag_in(bb_re), blockdiag_in(bb_im)], axis=2).astype(BF16)
    cc = jnp.concatenate([blockdiag_out(c_re), -blockdiag_out(c_im)], axis=1).astype(BF16)
    return dict(w_main=w_main, bb=bb, cc=cc, a_re=a_re.reshape(NGB_D, half), a_im=a_im.reshape(NGB_D, half),
                d=d, w_glu=w_glu.astype(BF16), b_glu=b_glu, w_out=w_out.astype(BF16), ln_g=ln_g, ln_b=ln_b)


def _odd_layer(x, k_cache, v_cache, h_re0, h_im0, wp, alpha):
    bx, t, d = x.shape
    m = bx * t
    tm = _tile(m, PROJ_ROWS)
    tq = _tile(t, SB_TILE)
    tt = _tile(t, MIXER_ROWS)
    x2 = x.reshape(m, d)
    p5 = _proj(x2, wp["w_main"], tm, "od_proj").reshape(6, bx, t, W_HEADS)
    if k_cache is None:
        yc, kk, vv = _sb(p5, None, None, tq, True, "sb_prompt")
    else:
        past = k_cache.shape[1]
        yc, kk, vv = _sb(p5, k_cache.reshape(bx, past * H_C, DK), v_cache.reshape(bx, past * H_C, DK), tq, False,
                         "sb_sample")
    half = GB_D * P_D
    yd, hr, hi = _s5(p5, 4, 5, wp["bb"], wp["cc"], wp["a_re"], wp["a_im"], wp["d"], wp["w_glu"], wp["b_glu"],
                     h_re0.reshape(bx, NGB_D, half), h_im0.reshape(bx, NGB_D, half), tt, "s5_glu")
    xn = _outproj_ln(yc.reshape(m, -1), yd.reshape(m, -1), x2, wp["w_out"], wp["ln_g"], wp["ln_b"], alpha,
                     _tile(m, OUTPROJ_ROWS), "od_outproj_ln")
    return (xn.reshape(bx, t, d), kk.reshape(bx, t, H_C, DK), vv.reshape(bx, t, H_C, DK),
            hr.reshape(bx, G_D, P_D), hi.reshape(bx, G_D, P_D))


def kernel(x_prompt, x_sample, state_delta_S, state_delta_conv, state_mlstm_C, state_mlstm_n, state_mlstm_m, cache_sb_k, cache_sb_v, state_s5_re, state_s5_im, ev_w_in, ev_conv_w, ev_a_log, ev_dt_bias, ev_norm_a, ev_ig_bias, ev_fg_bias, ev_norm_b, ev_w_out, ev_ln_g, ev_ln_b, od_w_in, od_lam_re, od_lam_im, od_b_re, od_b_im, od_c_re, od_c_im, od_d, od_log_dt, od_w_glu, od_b_glu, od_w_out, od_ln_g, od_ln_b):
    n_even = ev_w_in.shape[0]
    n_odd = od_w_in.shape[0]
    depth = n_even + n_odd
    alpha = (2 * depth) ** 0.25
    nb, seq, _ = x_prompt.shape
    ns, dseq, _ = x_sample.shape
    xp, xs = x_prompt, x_sample
    outs = [[] for _ in range(18)]
    for layer in range(depth):
        j = layer // 2
        if layer % 2 == 0:
            wp = _pack_even_weights(ev_w_in[j], ev_conv_w[j], ev_a_log[j], ev_dt_bias[j], ev_norm_a[j], ev_ig_bias[j],
                                    ev_fg_bias[j], ev_norm_b[j], ev_w_out[j], ev_ln_g[j], ev_ln_b[j])
            zs = jnp.zeros((nb, H_A, DK, DK), F32)
            xp, s, cv, c, n, m = _even_layer(
                xp, zs, jnp.zeros((nb, CONV_W - 1, 3 * H_A * DK), F32), zs, jnp.zeros((nb, H_B, DK), F32),
                jnp.zeros((nb, H_B), F32), wp, alpha)
            for idx, val in zip((0, 2, 4, 6, 8), (s, cv, c, n, m)):
                outs[idx].append(val)
            xs, s, cv, c, n, m = _even_layer(
                xs, state_delta_S[j], state_delta_conv[j], state_mlstm_C[j], state_mlstm_n[j], state_mlstm_m[j],
                wp, alpha)
            for idx, val in zip((1, 3, 5, 7, 9), (s, cv, c, n, m)):
                outs[idx].append(val)
        else:
            wp = _pack_odd_weights(od_w_in[j], od_lam_re[j], od_lam_im[j], od_b_re[j], od_b_im[j], od_c_re[j],
                                   od_c_im[j], od_d[j], od_log_dt[j], od_w_glu[j], od_b_glu[j], od_w_out[j],
                                   od_ln_g[j], od_ln_b[j])
            zst = jnp.zeros((nb, G_D, P_D), F32)
            xp, k, v, hr, hi = _odd_layer(xp, None, None, zst, zst, wp, alpha)
            for idx, val in zip((10, 12, 14, 16), (k, v, hr, hi)):
                outs[idx].append(val)
            xs, k, v, hr, hi = _odd_layer(xs, cache_sb_k[j], cache_sb_v[j], state_s5_re[j], state_s5_im[j], wp, alpha)
            for idx, val in zip((11, 13, 15, 17), (k, v, hr, hi)):
                outs[idx].append(val)
    return (xp, xs) + tuple(jnp.stack(o) for o in outs)
```

```python
import functools

import jax
import jax.numpy as jnp
from jax import lax
from jax.experimental import pallas as pl
from jax.experimental.pallas import tpu as pltpu

F32 = jnp.float32
BF16 = jnp.bfloat16

LANES = 128
SUBLANES = 8
CHUNK = 64
CONV_W = 4
H_A = 8
H_B = 8
H_C = 8
DK = 128
GC_D = 16
G_D = 64
P_D = 64
GB_D = 8
NGB_D = G_D // GB_D
W_HEADS = 1024
LN_EPS = 1e-5
NORM_EPS = 1e-6
VMEM_LIMIT = 48 * 1024 * 1024
SB_DEAD_LOG_WEIGHT = -88.0
SB_TILE = 128
PROJ_ROWS = 1024
OUTPROJ_ROWS = 512
MIXER_ROWS = 512
GDN_CHUNK_GROUP = 4


def _cparams(sem):
    return pltpu.CompilerParams(dimension_semantics=sem, vmem_limit_bytes=VMEM_LIMIT)


def _sigmoid(x):
    return 1.0 / (1.0 + jnp.exp(-x))


def _silu(x):
    return x * _sigmoid(x)


def _softplus(x):
    return jnp.maximum(x, 0.0) + jnp.log1p(jnp.exp(-jnp.abs(x)))


def _dot(a, b):
    return jnp.dot(a, b, preferred_element_type=F32)


def _dot_nt(a, b):
    return lax.dot_general(a, b, (((1,), (1,)), ((), ())), preferred_element_type=F32)


def _dot_tn(a, b):
    return lax.dot_general(a, b, (((0,), (0,)), ((), ())), preferred_element_type=F32)


def _split3(x):
    hi = x.astype(BF16)
    r1 = x - hi.astype(F32)
    mid = r1.astype(BF16)
    lo = (r1 - mid.astype(F32)).astype(BF16)
    return hi, mid, lo


def _dot_mask_lhs(mask_bf16, x):
    hi, mid, lo = _split3(x)
    return _dot(mask_bf16, hi) + _dot(mask_bf16, mid) + _dot(mask_bf16, lo)


def _dot_tn_split(x, mask_bf16):
    hi, mid, lo = _split3(x)
    return _dot_tn(hi, mask_bf16) + _dot_tn(mid, mask_bf16) + _dot_tn(lo, mask_bf16)


def _iota2(shape):
    return (lax.broadcasted_iota(jnp.int32, shape, 0), lax.broadcasted_iota(jnp.int32, shape, 1))


def _head(x, h):
    return x[:, h * DK:(h + 1) * DK]


def _proj_kernel(x_ref, w_ref, o_ref):
    o_ref[0] = _dot(x_ref[...].astype(BF16), w_ref[0])


def _proj(x2, w, tm, name):
    m, d = x2.shape
    ng, _, wn = w.shape
    return pl.pallas_call(
        _proj_kernel,
        grid=(m // tm, ng),
        in_specs=[pl.BlockSpec((tm, d), lambda i, j: (i, 0)),
                  pl.BlockSpec((1, d, wn), lambda i, j: (j, 0, 0))],
        out_specs=pl.BlockSpec((1, tm, wn), lambda i, j: (j, i, 0)),
        out_shape=jax.ShapeDtypeStruct((ng, m, wn), F32),
        compiler_params=_cparams(("parallel", "arbitrary")),
        name=name,
    )(x2, w)


def _outproj_ln_kernel(ya_ref, yb_ref, x_ref, wa_ref, wb_ref, g_ref, b_ref, o_ref, *, alpha):
    acc = _dot(ya_ref[...].astype(BF16), wa_ref[...]) + _dot(yb_ref[...].astype(BF16), wb_ref[...])
    r = alpha * x_ref[...] + acc
    mu = jnp.mean(r, axis=-1, keepdims=True)
    c = r - mu
    var = jnp.mean(c * c, axis=-1, keepdims=True)
    o_ref[...] = c * lax.rsqrt(var + LN_EPS) * g_ref[...] + b_ref[...]


def _outproj_ln(ya, yb, x2, w_out_bf16, ln_g, ln_b, alpha, tm, name):
    m, d = x2.shape
    wa = ya.shape[1]
    wb = yb.shape[1]
    return pl.pallas_call(
        functools.partial(_outproj_ln_kernel, alpha=alpha),
        grid=(m // tm,),
        in_specs=[pl.BlockSpec((tm, wa), lambda i: (i, 0)),
                  pl.BlockSpec((tm, wb), lambda i: (i, 0)),
                  pl.BlockSpec((tm, d), lambda i: (i, 0)),
                  pl.BlockSpec((wa, d), lambda i: (0, 0)),
                  pl.BlockSpec((wb, d), lambda i: (wa // wb, 0)),
                  pl.BlockSpec((1, d), lambda i: (0, 0)),
                  pl.BlockSpec((1, d), lambda i: (0, 0))],
        out_specs=pl.BlockSpec((tm, d), lambda i: (i, 0)),
        out_shape=jax.ShapeDtypeStruct((m, d), F32),
        compiler_params=_cparams(("parallel",)),
        name=name,
    )(ya, yb, x2, w_out_bf16, w_out_bf16, ln_g.reshape(1, d), ln_b.reshape(1, d))


def _conv_carry_init(carry, c0_ref, w_ref):
    w = w_ref[...]
    xm3, xm2, xm1 = c0_ref[5:6, :], c0_ref[6:7, :], c0_ref[7:8, :]
    carry[0:1, :] = w[0:1, :] * xm1
    carry[1:2, :] = w[1:2, :] * xm1 + w[0:1, :] * xm2
    carry[2:3, :] = w[2:3, :] * xm1 + (w[1:2, :] * xm2 + w[0:1, :] * xm3)


def _conv_silu(x_ref, carry, w_ref, tb):
    x = x_ref[...]
    w = w_ref[...]
    first = lax.broadcasted_iota(jnp.int32, (SUBLANES, x.shape[1]), 0) == 0
    acc = x * w[0:1, :]
    for i in range(1, CONV_W):
        shifted_in = carry[i - 1:i, :]
        carry[i - 1:i, :] = acc[tb - 1:tb, :]
        rolled = pltpu.roll(acc, 1, axis=0)
        head = jnp.where(first, shifted_in, rolled[0:SUBLANES])
        acc = x * w[i:i + 1, :] + jnp.concatenate([head, rolled[SUBLANES:]], axis=0)
    return _silu(acc)


def _l2norm(x):
    return x * lax.rsqrt(jnp.sum(x * x, axis=-1, keepdims=True) + NORM_EPS)


def _rms_norm(x, g):
    return x * lax.rsqrt(jnp.mean(x * x, axis=-1, keepdims=True) + NORM_EPS) * g


def _unit_lower_inverses(n_mats, length):
    r, c = _iota2((length, length))
    eye = jnp.where(r == c, 1.0, 0.0)
    xs = [eye + n for n in n_mats]
    ps = list(n_mats)
    for _ in range(length.bit_length() - 2):
        pbs = [p.astype(BF16) for p in ps]
        ps = [_dot(pb, pb) for pb in pbs]
        xs = [x + _dot(x.astype(BF16), p.astype(BF16)) for x, p in zip(xs, ps)]
    return xs


def _gdn_kernel(qp_ref, kp_ref, vp_ref, z_ref, gt_ref, c0q_ref, c0k_ref, c0v_ref,
                cwq_ref, cwk_ref, cwv_ref, prm_ref, norm_ref, s0_ref,
                y_ref, sout_ref, s_scr, xq_scr, xk_scr, xv_scr, *, chunk, tb):
    t = pl.program_id(1)
    nh = H_A

    @pl.when(t == 0)
    def _():
        s_scr[...] = s0_ref[...]
        _conv_carry_init(xq_scr, c0q_ref, cwq_ref)
        _conv_carry_init(xk_scr, c0k_ref, cwk_ref)
        _conv_carry_init(xv_scr, c0v_ref, cwv_ref)

    q_all = _conv_silu(qp_ref, xq_scr, cwq_ref, tb)
    k_all = _conv_silu(kp_ref, xk_scr, cwk_ref, tb)
    v_all = _conv_silu(vp_ref, xv_scr, cwv_ref, tb)
    q_h = [_l2norm(_head(q_all, h)) * (DK ** -0.5) for h in range(nh)]
    k_h = [_l2norm(_head(k_all, h)) for h in range(nh)]

    gt = gt_ref[...]
    prm = prm_ref[...]
    beta_all = _sigmoid(gt)
    g_all = -jnp.exp(prm[0:1, :]) * _softplus(gt + prm[1:2, :])

    r, c = _iota2((chunk, chunk))
    incl = r >= c
    strict = r > c
    a_incl = jnp.where(incl, 1.0, 0.0).astype(BF16)
    a_incl_t = jnp.where(r <= c, 1.0, 0.0).astype(BF16)
    norm_g = norm_ref[...]

    def precompute(cis):
        items = []
        for ci in cis:
            sl = slice(ci * chunk, (ci + 1) * chunk)
            g_c = g_all[sl]
            gc_cols = _dot_mask_lhs(a_incl, g_c)
            gc_rows = _dot_tn_split(g_c, a_incl_t)
            beta_c = beta_all[sl]
            for h in range(nh):
                items.append(dict(ci=ci, gcol=gc_cols[:, nh + h:nh + h + 1], grow=gc_rows[nh + h:nh + h + 1, :],
                                  bc=beta_c[:, h:h + 1], qc=q_h[h][sl], kc=k_h[h][sl], vc=_head(v_all, h)[sl]))
        qkks = [_dot_nt(jnp.concatenate([it["qc"], it["kc"]], axis=0).astype(BF16), it["kc"].astype(BF16))
                for it in items]
        decays = [jnp.where(incl, jnp.exp(it["gcol"] - it["grow"]), 0.0) for it in items]
        n_mats = [jnp.where(strict, -(it["bc"] * qkk[chunk:] * decay), 0.0)
                  for it, qkk, decay in zip(items, qkks, decays)]
        t_invs = _unit_lower_inverses(n_mats, chunk)
        gams = [jnp.exp(it["gcol"]) for it in items]
        rhss = [jnp.concatenate([it["vc"] * it["bc"], it["kc"] * (it["bc"] * gam)], axis=-1).astype(BF16)
                for it, gam in zip(items, gams)]
        sols = [_dot(ti.astype(BF16), rhs) for ti, rhs in zip(t_invs, rhss)]
        out = {ci: [] for ci in cis}
        for it, qkk, decay, gam, sol in zip(items, qkks, decays, gams, sols):
            g_last = it["gcol"][chunk - 1:chunk, :]
            out[it["ci"]].append(dict(
                u=sol[:, :DK],
                wq=jnp.concatenate([sol[:, DK:], it["qc"] * gam], axis=0).astype(BF16),
                qk=(qkk[:chunk] * decay).astype(BF16),
                k_dec=(it["kc"] * jnp.exp(g_last - it["gcol"])).astype(BF16),
                s_decay=jnp.exp(g_last)))
        return out

    def advance(ci, heads, states):
        sl = slice(ci * chunk, (ci + 1) * chunk)
        wss = [_dot(hd["wq"], s.astype(BF16)) for hd, s in zip(heads, states)]
        es = [(hd["u"] - ws[:chunk]).astype(BF16) for hd, ws in zip(heads, wss)]
        new_states = [hd["s_decay"] * s + _dot_tn(hd["k_dec"], e) for hd, s, e in zip(heads, states, es)]
        outs = [ws[chunk:] + _dot(hd["qk"], e) for hd, ws, e in zip(heads, wss, es)]
        for h in range(nh):
            y_ref[sl, h * DK:(h + 1) * DK] = (_rms_norm(outs[h], norm_g)
                                              * _silu(z_ref[sl, h * DK:(h + 1) * DK])).astype(y_ref.dtype)
        return new_states

    n_chunks = tb // chunk
    groups = [list(range(g0, min(g0 + GDN_CHUNK_GROUP, n_chunks))) for g0 in range(0, n_chunks, GDN_CHUNK_GROUP)]
    states = [s_scr[h] for h in range(nh)]
    pre = precompute(groups[0])
    for gi, group in enumerate(groups):
        nxt = precompute(groups[gi + 1]) if gi + 1 < len(groups) else None
        for ci in group:
            states = advance(ci, pre[ci], states)
        pre = nxt
    for h in range(nh):
        s_scr[h] = states[h]

    @pl.when(t == pl.num_programs(1) - 1)
    def _():
        for h in range(nh):
            sout_ref[h] = states[h]


def _gdn(p5, g3, conv0p, conv_w, prm, norm_a, s0, chunk, tb, name):
    _, bx, t, w = p5.shape
    nh = H_A

    def pspec(group):
        return pl.BlockSpec((None, None, tb, w), lambda b, i, g=group: (g, b, i, 0))

    def c0spec(group):
        return pl.BlockSpec((None, SUBLANES, w), lambda b, i, g=group: (b, 0, g))

    def cwspec(group):
        return pl.BlockSpec((CONV_W, w), lambda b, i, g=group: (0, g))

    return pl.pallas_call(
        functools.partial(_gdn_kernel, chunk=chunk, tb=tb),
        grid=(bx, t // tb),
        in_specs=[pspec(0), pspec(1), pspec(2), pspec(3),
                  pl.BlockSpec((None, tb, LANES), lambda b, i: (b, i, 0)),
                  c0spec(0), c0spec(1), c0spec(2),
                  cwspec(0), cwspec(1), cwspec(2),
                  pl.BlockSpec((SUBLANES, LANES), lambda b, i: (0, 0)),
                  pl.BlockSpec((1, DK), lambda b, i: (0, 0)),
                  pl.BlockSpec((None, nh, DK, DK), lambda b, i: (b, 0, 0, 0))],
        out_specs=[pl.BlockSpec((None, tb, w), lambda b, i: (b, i, 0)),
                   pl.BlockSpec((None, nh, DK, DK), lambda b, i: (b, 0, 0, 0))],
        out_shape=[jax.ShapeDtypeStruct((bx, t, w), BF16),
                   jax.ShapeDtypeStruct((bx, nh, DK, DK), F32)],
        scratch_shapes=[pltpu.VMEM((nh, DK, DK), F32),
                        pltpu.VMEM((SUBLANES, w), F32),
                        pltpu.VMEM((SUBLANES, w), F32),
                        pltpu.VMEM((SUBLANES, w), F32)],
        compiler_params=_cparams(("parallel", "arbitrary")),
        name=name,
    )(p5, p5, p5, p5, g3, conv0p, conv0p, conv0p, conv_w, conv_w, conv_w, prm, norm_a, s0)


def _mlstm_kernel(q_ref, k_ref, v_ref, o_ref, z_ref, gt_ref, prm_ref, norm_ref, c0_ref, n0_ref, m0_ref,
                  y_ref, cout_ref, nout_ref, mout_ref, cn_scr, m_scr, *, chunk, tb):
    t = pl.program_id(1)
    nh = H_B
    heads = range(nh)
    ig_lane = 2 * H_A
    lf_lane = 2 * H_A + H_B

    @pl.when(t == 0)
    def _():
        cn_scr[:, :, :DK] = c0_ref[...]
        cn_scr[:, :, DK:] = n0_ref[...]
        m_scr[...] = m0_ref[...]

    gt = gt_ref[...]
    prm = prm_ref[...]
    ig_all = gt + prm[2:3, :]
    lf_all = -_softplus(-(gt + prm[3:4, :]))

    r, c = _iota2((chunk, chunk))
    incl = r >= c
    a_incl = jnp.where(incl, 1.0, 0.0).astype(BF16)
    a_incl_t = jnp.where(r <= c, 1.0, 0.0).astype(BF16)
    eye = jnp.where(r == c, 1.0, 0.0).astype(BF16)
    ones_w = jnp.ones((chunk, DK), F32)
    ones_b = jnp.ones((chunk, DK), BF16)
    norm_g = norm_ref[...]

    def precompute(ci):
        sl = slice(ci * chunk, (ci + 1) * chunk)
        lf_c = lf_all[sl]
        ig_c = ig_all[sl]
        b_cols = _dot_mask_lhs(a_incl, lf_c)
        b_rows = _dot_tn_split(lf_c, a_incl_t)
        ig_rows = _dot_tn_split(ig_c, eye)
        bcols = [b_cols[:, lf_lane + h:lf_lane + h + 1] for h in heads]
        d_logs = [jnp.where(incl, bcols[h] - b_rows[lf_lane + h:lf_lane + h + 1, :]
                            + ig_rows[ig_lane + h:ig_lane + h + 1, :], -jnp.inf) for h in heads]
        m_intras = [jnp.max(d, axis=-1, keepdims=True) for d in d_logs]
        qbs = [q_ref[sl, h * DK:(h + 1) * DK].astype(BF16) for h in heads]
        kcs = [k_ref[sl, h * DK:(h + 1) * DK] * (DK ** -0.5) for h in heads]
        v1s = [jnp.concatenate([v_ref[sl, h * DK:(h + 1) * DK].astype(BF16), ones_b], axis=1) for h in heads]
        qks = [_dot_nt(qb, kc.astype(BF16)) for qb, kc in zip(qbs, kcs)]
        pq0s = [(jnp.exp(d - mi) * qk).astype(BF16) for d, mi, qk in zip(d_logs, m_intras, qks)]
        avs = [_dot(pq0, v1) for pq0, v1 in zip(pq0s, v1s)]
        return [dict(b_w=bcols[h] * ones_w, mi_w=m_intras[h] * ones_w,
                     tail_w=(bcols[h][chunk - 1:chunk, :] - bcols[h] + ig_c[:, ig_lane + h:ig_lane + h + 1]) * ones_w,
                     qb=qbs[h], kc=kcs[h], v1=v1s[h], av=avs[h]) for h in heads]

    def advance(ci, pre, cns, ms):
        sl = slice(ci * chunk, (ci + 1) * chunk)
        qcns = [_dot(pre[h]["qb"], cns[h].astype(BF16)) for h in heads]
        bms = [pre[h]["b_w"] + ms[h] for h in heads]
        m_ts = [jnp.maximum(bms[h], pre[h]["mi_w"]) for h in heads]
        corrs = [jnp.exp(pre[h]["mi_w"] - m_ts[h]) for h in heads]
        inters = [jnp.exp(bms[h] - m_ts[h]) for h in heads]
        m_lasts = [m_t[chunk - 1:chunk, :] for m_t in m_ts]
        i_lasts = [inter[chunk - 1:chunk, :] for inter in inters]
        kps = [(pre[h]["kc"] * jnp.exp(pre[h]["tail_w"] - m_lasts[h])).astype(BF16) for h in heads]
        new_cns = [jnp.concatenate([i_lasts[h], i_lasts[h]], axis=1) * cns[h] + _dot_tn(kps[h], pre[h]["v1"])
                   for h in heads]
        for h in heads:
            num = inters[h] * qcns[h][:, :DK] + corrs[h] * pre[h]["av"][:, :DK]
            den = inters[h] * qcns[h][:, DK:] + corrs[h] * pre[h]["av"][:, DK:]
            hh = num / jnp.maximum(jnp.abs(den), jnp.exp(-m_ts[h]))
            hh = _sigmoid(o_ref[sl, h * DK:(h + 1) * DK]) * hh
            y_ref[sl, h * DK:(h + 1) * DK] = (_rms_norm(hh, norm_g[:, h * DK:(h + 1) * DK])
                                              * _silu(z_ref[sl, h * DK:(h + 1) * DK])).astype(y_ref.dtype)
        return new_cns, m_lasts

    n_chunks = tb // chunk
    cns = [cn_scr[h] for h in heads]
    ms = [m_scr[h] for h in heads]
    pre = precompute(0)
    for ci in range(n_chunks):
        nxt = precompute(ci + 1) if ci + 1 < n_chunks else None
        cns, ms = advance(ci, pre, cns, ms)
        pre = nxt
    for h in heads:
        cn_scr[h] = cns[h]
        m_scr[h] = ms[h]

    @pl.when(t == pl.num_programs(1) - 1)
    def _():
        for h in heads:
            cout_ref[h] = cns[h][:, :DK]
            nout_ref[h] = cns[h][:, DK:]
            mout_ref[h] = ms[h]


def _mlstm(p5, g3, prm, norm_b, c0, n0w, m0w, chunk, tb, name):
    _, bx, t, w = p5.shape
    nh = H_B

    def pspec(group):
        return pl.BlockSpec((None, None, tb, w), lambda b, i, g=group: (g, b, i, 0))

    mat = pl.BlockSpec((None, nh, DK, DK), lambda b, i: (b, 0, 0, 0))
    row = pl.BlockSpec((None, nh, 1, DK), lambda b, i: (b, 0, 0, 0))
    return pl.pallas_call(
        functools.partial(_mlstm_kernel, chunk=chunk, tb=tb),
        grid=(bx, t // tb),
        in_specs=[pspec(4), pspec(5), pspec(6), pspec(7), pspec(8),
                  pl.BlockSpec((None, tb, LANES), lambda b, i: (b, i, 0)),
                  pl.BlockSpec((SUBLANES, LANES), lambda b, i: (0, 0)),
                  pl.BlockSpec((1, w), lambda b, i: (0, 0)),
                  mat, mat, row],
        out_specs=[pl.BlockSpec((None, tb, w), lambda b, i: (b, i, 0)), mat, mat, row],
        out_shape=[jax.ShapeDtypeStruct((bx, t, w), BF16),
                   jax.ShapeDtypeStruct((bx, nh, DK, DK), F32),
                   jax.ShapeDtypeStruct((bx, nh, DK, DK), F32),
                   jax.ShapeDtypeStruct((bx, nh, 1, DK), F32)],
        scratch_shapes=[pltpu.VMEM((nh, DK, 2 * DK), F32), pltpu.VMEM((nh, 1, DK), F32)],
        compiler_params=_cparams(("parallel", "arbitrary")),
        name=name,
    )(p5, p5, p5, p5, p5, g3, prm, norm_b, c0, n0w, m0w)


def _sb_tiles(qbs, kts, vts, runs, u_neg, causal):
    tk = kts[0].shape[0]
    zs = [_dot_nt(qb, kt.astype(BF16)) for qb, kt in zip(qbs, kts)]
    sps = [jnp.maximum(z, 0.0) + jnp.log(1.0 + jnp.exp(-jnp.abs(z))) for z in zs]
    if causal is not None:
        sps = [jnp.where(causal, sp, 0.0) for sp in sps]
    his = [sp.astype(BF16) for sp in sps]
    los = [(sp - hi.astype(F32)).astype(BF16) for sp, hi in zip(sps, his)]
    exts = [_dot(jnp.concatenate([hi, lo], axis=1), u_neg) for hi, lo in zip(his, los)]
    atts = [jnp.exp(z + ext[:, :tk] + run[:, :tk]) for z, ext, run in zip(zs, exts, runs)]
    if causal is not None:
        atts = [jnp.where(causal, att, 0.0) for att in atts]
    contribs = [_dot(att.astype(BF16), vt.astype(BF16)) for att, vt in zip(atts, vts)]
    return [run + ext[:, tk:] for run, ext in zip(runs, exts)], contribs


def _sb_kernel(q_ref, z_ref, kc_ref, vc_ref, k1_ref, v1_ref, k2_ref, v2_ref, kany_ref, vany_ref,
               y_ref, ko_ref, vo_ref, acc_scr, run_scr, q_scr, kbuf, vbuf, flag, sem,
               *, tq, tk, prev_blocks_per_step, prev_blocks_fixed, prev_token_rows, kgroup, vgroup):
    nh = H_C
    b = pl.program_id(0)
    i = pl.program_id(1)
    r, c = _iota2((tq, tq))
    causal = c < r
    def suffix_sum_matrix(n):
        rr, cc = _iota2((2 * n, n + LANES))
        rr = jnp.where(rr >= n, rr - n, rr)
        return jnp.where((rr >= cc) | (cc >= n), -1.0, 0.0).astype(BF16)

    u_diag = suffix_sum_matrix(tq)
    u_prev = suffix_sum_matrix(tk)
    q_scr[...] = (q_ref[...] * (DK ** -0.5)).astype(BF16)

    def q_head(h):
        return q_scr[:, h * DK:(h + 1) * DK]

    def set_flag():
        best = run_scr[0]
        for h in range(1, nh):
            best = jnp.maximum(best, run_scr[h])
        flag[0] = (jnp.max(best) > SB_DEAD_LOG_WEIGHT).astype(jnp.int32)

    heads = range(nh)
    kts = [kc_ref[:, h * DK:(h + 1) * DK] for h in heads]
    vts = [vc_ref[:, h * DK:(h + 1) * DK] for h in heads]
    for h in heads:
        ko_ref[pl.ds(h, tq, stride=nh), :] = kts[h]
        vo_ref[pl.ds(h, tq, stride=nh), :] = vts[h]
    runs, contribs = _sb_tiles([q_head(h) for h in heads], kts, vts, [jnp.zeros((tq, LANES), F32)] * nh, u_diag,
                               causal)
    for h in heads:
        run_scr[h] = runs[h]
        acc_scr[:, h * DK:(h + 1) * DK] = contribs[h]
    set_flag()

    def prev_tile(ref, h):
        if prev_token_rows:
            return ref[pl.ds(h, tk, stride=nh), :]
        return ref[:, h * DK:(h + 1) * DK]

    def stage(k_ref, v_ref):
        runs, contribs = _sb_tiles([q_head(h) for h in heads], [prev_tile(k_ref, h) for h in heads],
                                   [prev_tile(v_ref, h) for h in heads], [run_scr[h] for h in heads], u_prev, None)
        for h in heads:
            run_scr[h] = runs[h]
            acc_scr[:, h * DK:(h + 1) * DK] += contribs[h]
        set_flag()

    nprev = i * prev_blocks_per_step + prev_blocks_fixed

    @pl.when(jnp.logical_and(nprev >= 1, flag[0] > 0))
    def _():
        stage(k1_ref, v1_ref)

    @pl.when(jnp.logical_and(nprev >= 2, flag[0] > 0))
    def _():
        stage(k2_ref, v2_ref)

    def far_copy(any_ref, group, buf, j, slot):
        start = (nprev - j) * tk
        if prev_token_rows:
            src = any_ref.at[b, pl.ds(pl.multiple_of(start * nh, tk * nh), tk * nh), :]
        else:
            src = any_ref.at[group, b, pl.ds(pl.multiple_of(start, tk), tk), :]
        return pltpu.make_async_copy(src, buf, sem.at[slot])

    def cond(carry):
        j, go = carry
        return jnp.logical_and(j <= nprev, go > 0)

    def body(carry):
        j, _ = carry
        kcopy = far_copy(kany_ref, kgroup, kbuf, j, 0)
        vcopy = far_copy(vany_ref, vgroup, vbuf, j, 1)
        kcopy.start()
        vcopy.start()
        kcopy.wait()
        vcopy.wait()
        stage(kbuf, vbuf)
        return j + 1, flag[0]

    lax.while_loop(cond, body, (jnp.int32(3), flag[0]))
    y_ref[...] = (acc_scr[...] * _silu(z_ref[...])).astype(y_ref.dtype)


def _sb(p5, kprev, vprev, tq, causal_prev, name):
    _, bx, t, w = p5.shape
    nh = H_C
    tk = SB_TILE if not causal_prev else tq

    def blk(group):
        return pl.BlockSpec((None, None, tq, w), lambda b, i, g=group: (g, b, i, 0))

    if causal_prev:
        def near(group, back):
            return pl.BlockSpec((None, None, tk, w), lambda b, i, g=group: (g, b, jnp.maximum(i - back, 0), 0))
        kprev, vprev = p5, p5
        near_specs = [near(1, 1), near(2, 1), near(1, 2), near(2, 2)]
        buf_shape = (tk, w)
        per_step, fixed = tq // tk, 0
    else:
        nblk = kprev.shape[1] // (tk * nh)

        def near(back):
            return pl.BlockSpec((None, tk * nh, DK), lambda b, i: (b, max(nblk - back, 0), 0))
        near_specs = [near(1), near(1), near(2), near(2)]
        buf_shape = (tk * nh, DK)
        per_step, fixed = 0, nblk
    kern = functools.partial(_sb_kernel, tq=tq, tk=tk, prev_blocks_per_step=per_step, prev_blocks_fixed=fixed,
                             prev_token_rows=not causal_prev, kgroup=1, vgroup=2)
    rows_out = pl.BlockSpec((None, tq * nh, DK), lambda b, i: (b, i, 0))
    return pl.pallas_call(
        kern,
        grid=(bx, t // tq),
        in_specs=[blk(0), blk(3), blk(1), blk(2)] + near_specs
                 + [pl.BlockSpec(memory_space=pl.ANY), pl.BlockSpec(memory_space=pl.ANY)],
        out_specs=[pl.BlockSpec((None, tq, w), lambda b, i: (b, i, 0)), rows_out, rows_out],
        out_shape=[jax.ShapeDtypeStruct((bx, t, w), BF16),
                   jax.ShapeDtypeStruct((bx, t * nh, DK), F32),
                   jax.ShapeDtypeStruct((bx, t * nh, DK), F32)],
        scratch_shapes=[pltpu.VMEM((tq, w), F32), pltpu.VMEM((nh, tq, LANES), F32), pltpu.VMEM((tq, w), BF16),
                        pltpu.VMEM(buf_shape, F32), pltpu.VMEM(buf_shape, F32),
                        pltpu.SMEM((1,), jnp.int32), pltpu.SemaphoreType.DMA((2,))],
        compiler_params=_cparams(("parallel", "arbitrary")),
        name=name,
    )(p5, p5, p5, p5, kprev, vprev, kprev, vprev, kprev, vprev)


def _s5_kernel(u_ref, z_ref, bb_ref, cc_ref, ar_ref, ai_ref, d_ref, wg_ref, bg_ref, h0r_ref, h0i_ref,
               y_ref, hr_out, hi_out, hs, ys, hr_scr, hi_scr, *, tt):
    t = pl.program_id(1)
    half = GB_D * P_D
    nl = half // LANES

    @pl.when(t == 0)
    def _():
        hr_scr[...] = h0r_ref[...]
        hi_scr[...] = h0i_ref[...]

    for gb in range(NGB_D):
        rows = pl.ds(gb, tt, stride=NGB_D)
        bu = _dot(u_ref[:, gb * LANES:(gb + 1) * LANES].astype(BF16), bb_ref[gb])
        for j in range(2 * nl):
            hs[j, rows, :] = bu[:, j * LANES:(j + 1) * LANES]

    ar = [ar_ref[:, j * LANES:(j + 1) * LANES] for j in range(nl)]
    ai = [ai_ref[:, j * LANES:(j + 1) * LANES] for j in range(nl)]

    def step(s, carry):
        rows = pl.ds(pl.multiple_of(s * NGB_D, NGB_D), NGB_D)
        new = [None] * (2 * nl)
        for j in range(nl):
            hr, hi = carry[j], carry[nl + j]
            new[j] = ar[j] * hr - ai[j] * hi + hs[j, rows, :]
            new[nl + j] = ar[j] * hi + ai[j] * hr + hs[nl + j, rows, :]
        for j in range(2 * nl):
            hs[j, rows, :] = new[j]
        return tuple(new)

    init = tuple(hr_scr[:, j * LANES:(j + 1) * LANES] for j in range(nl)) + tuple(
        hi_scr[:, j * LANES:(j + 1) * LANES] for j in range(nl))
    fin = lax.fori_loop(0, tt, step, init, unroll=8)
    hr = jnp.concatenate(fin[:nl], axis=1)
    hi = jnp.concatenate(fin[nl:], axis=1)
    hr_scr[...] = hr
    hi_scr[...] = hi

    for gb in range(NGB_D):
        rows = pl.ds(gb, tt, stride=NGB_D)
        hcat = jnp.concatenate([hs[j, rows, :] for j in range(2 * nl)], axis=1)
        lanes = slice(gb * LANES, (gb + 1) * LANES)
        ys[:, lanes] = _dot(hcat.astype(BF16), cc_ref[gb]) + d_ref[:, lanes] * u_ref[:, lanes]

    y = ys[...]
    yg = 0.5 * y * (1.0 + jnp.tanh(0.7978845608028654 * (y + 0.044715 * (y * y * y))))
    gate = _sigmoid(_dot(yg.astype(BF16), wg_ref[...]) + bg_ref[...])
    y_ref[...] = (yg * gate * _silu(z_ref[...])).astype(y_ref.dtype)

    @pl.when(t == pl.num_programs(1) - 1)
    def _():
        hr_out[...] = hr
        hi_out[...] = hi


def _s5(p5, ugroup, zgroup, bb, cc, a_re, a_im, d, w_glu_bf16, b_glu, h0r, h0i, tt, name):
    _, bx, t, w = p5.shape
    half = GB_D * P_D
    st = pl.BlockSpec((None, NGB_D, half), lambda b, i: (b, 0, 0))
    prm = pl.BlockSpec((NGB_D, half), lambda b, i: (0, 0))
    row = pl.BlockSpec((1, w), lambda b, i: (0, 0))
    return pl.pallas_call(
        functools.partial(_s5_kernel, tt=tt),
        grid=(bx, t // tt),
        in_specs=[pl.BlockSpec((None, None, tt, w), lambda b, i: (ugroup, b, i, 0)),
                  pl.BlockSpec((None, None, tt, w), lambda b, i: (zgroup, b, i, 0)),
                  pl.BlockSpec((NGB_D, LANES, 2 * half), lambda b, i: (0, 0, 0)),
                  pl.BlockSpec((NGB_D, 2 * half, LANES), lambda b, i: (0, 0, 0)),
                  prm, prm, row,
                  pl.BlockSpec((w, w), lambda b, i: (0, 0)),
                  row, st, st],
        out_specs=[pl.BlockSpec((None, tt, w), lambda b, i: (b, i, 0)), st, st],
        out_shape=[jax.ShapeDtypeStruct((bx, t, w), BF16),
                   jax.ShapeDtypeStruct((bx, NGB_D, half), F32),
                   jax.ShapeDtypeStruct((bx, NGB_D, half), F32)],
        scratch_shapes=[pltpu.VMEM((2 * half // LANES, tt * NGB_D, LANES), F32),
                        pltpu.VMEM((tt, w), F32),
                        pltpu.VMEM((NGB_D, half), F32), pltpu.VMEM((NGB_D, half), F32)],
        compiler_params=_cparams(("parallel", "arbitrary")),
        name=name,
    )(p5, p5, bb, cc, a_re, a_im, d.reshape(1, w), w_glu_bf16, b_glu.reshape(1, w), h0r, h0i)


def _pack_even_weights(w_in, conv_w, a_log, dt_bias, norm_a, ig_bias, fg_bias, norm_b, w_out, ln_g, ln_b):
    d = w_in.shape[0]
    wa = H_A * DK
    wb = H_B * DK
    o = 0
    qkv_a = w_in[:, o:o + 3 * wa]; o += 3 * wa
    z_a = w_in[:, o:o + wa]; o += wa
    gate_a = w_in[:, o:o + 2 * H_A]; o += 2 * H_A
    main_b = w_in[:, o:o + 5 * wb]; o += 5 * wb
    gate_b = w_in[:, o:o + 2 * H_B]
    w_main = jnp.concatenate([qkv_a, z_a, main_b], axis=1).astype(BF16)
    w_main = w_main.reshape(d, 9, W_HEADS).transpose(1, 0, 2)
    n_gate = 2 * H_A + 2 * H_B
    w_gate = jnp.concatenate([gate_a, gate_b, jnp.zeros((d, LANES - n_gate), F32)], axis=1).astype(BF16)[None]
    prm = jnp.zeros((SUBLANES, LANES), F32)
    prm = prm.at[0, H_A:2 * H_A].set(a_log)
    prm = prm.at[1, H_A:2 * H_A].set(dt_bias)
    prm = prm.at[2, 2 * H_A:2 * H_A + H_B].set(ig_bias)
    prm = prm.at[3, 2 * H_A + H_B:2 * H_A + 2 * H_B].set(fg_bias)
    return dict(w_main=w_main, w_gate=w_gate, conv_w=conv_w, prm=prm, norm_a=norm_a.reshape(1, DK),
                norm_b=norm_b.reshape(1, wb), w_out=w_out.astype(BF16), ln_g=ln_g, ln_b=ln_b)


def _tile(n, preferred):
    tile = min(n, preferred)
    assert n % tile == 0, (n, preferred)
    return tile


def _even_layer(x, s0, conv0, c0, n0, m0, wp, alpha):
    bx, t, d = x.shape
    m = bx * t
    chunk = min(CHUNK, t)
    tm = _tile(m, PROJ_ROWS)
    tb = _tile(t, MIXER_ROWS)
    x2 = x.reshape(m, d)
    p5 = _proj(x2, wp["w_main"], tm, "ev_proj").reshape(9, bx, t, W_HEADS)
    g3 = _proj(x2, wp["w_gate"], tm, "ev_gate_proj").reshape(bx, t, LANES)
    conv0p = jnp.concatenate([jnp.zeros((bx, SUBLANES - (CONV_W - 1), conv0.shape[2]), F32), conv0], axis=1)
    ya, s_new = _gdn(p5, g3, conv0p, wp["conv_w"], wp["prm"], wp["norm_a"], s0, chunk, tb, "gdn")
    m0w = jnp.broadcast_to(m0[:, :, None, None], (bx, H_B, 1, DK))
    n0w = jnp.broadcast_to(n0[:, :, :, None], (bx, H_B, DK, DK))
    yb, c_new, n_new, m_new = _mlstm(p5, g3, wp["prm"], wp["norm_b"], c0, n0w, m0w, chunk, tb, "mlstm")
    xn = _outproj_ln(ya.reshape(m, -1), yb.reshape(m, -1), x2, wp["w_out"], wp["ln_g"], wp["ln_b"], alpha,
                     _tile(m, OUTPROJ_ROWS), "ev_outproj_ln")
    conv_new = jnp.concatenate([p5[0, :, t - (CONV_W - 1):], p5[1, :, t - (CONV_W - 1):], p5[2, :, t - (CONV_W - 1):]],
                               axis=-1)
    return xn.reshape(bx, t, d), s_new, conv_new, c_new, n_new[:, :, :, 0], m_new[:, :, 0, 0]


def _pack_odd_weights(w_in, lam_re, lam_im, b_re, b_im, c_re, c_im, d, log_dt, w_glu, b_glu, w_out, ln_g, ln_b):
    dm = w_in.shape[0]
    w_main = w_in.astype(BF16).reshape(dm, 6, W_HEADS).transpose(1, 0, 2)
    dt = jnp.exp(log_dt)[:, None]
    mag = jnp.exp(lam_re * dt)
    a_re, a_im = mag * jnp.cos(lam_im * dt), mag * jnp.sin(lam_im * dt)
    den = lam_re * lam_re + lam_im * lam_im
    f_re = ((a_re - 1.0) * lam_re + a_im * lam_im) / den
    f_im = (a_im * lam_re - (a_re - 1.0) * lam_im) / den
    bb_re = f_re[..., None] * b_re - f_im[..., None] * b_im
    bb_im = f_re[..., None] * b_im + f_im[..., None] * b_re
    half = GB_D * P_D
    eye = jnp.eye(GB_D, dtype=F32)

    def blockdiag_in(bb):
        bbg = bb.reshape(NGB_D, GB_D, P_D, GC_D)
        return jnp.einsum("bgpc,gh->bgchp", bbg, eye).reshape(NGB_D, GB_D * GC_D, half)

    def blockdiag_out(cm):
        cg = cm.reshape(NGB_D, GB_D, GC_D, P_D)
        return jnp.einsum("bgcp,gh->bgphc", cg, eye).reshape(NGB_D, half, GB_D * GC_D)

    bb = jnp.concatenate([blockdiag_in(bb_re), blockdiag_in(bb_im)], axis=2).astype(BF16)
    cc = jnp.concatenate([blockdiag_out(c_re), -blockdiag_out(c_im)], axis=1).astype(BF16)
    return dict(w_main=w_main, bb=bb, cc=cc, a_re=a_re.reshape(NGB_D, half), a_im=a_im.reshape(NGB_D, half),
                d=d, w_glu=w_glu.astype(BF16), b_glu=b_glu, w_out=w_out.astype(BF16), ln_g=ln_g, ln_b=ln_b)


def _odd_layer(x, k_cache, v_cache, h_re0, h_im0, wp, alpha):
    bx, t, d = x.shape
    m = bx * t
    tm = _tile(m, PROJ_ROWS)
    tq = _tile(t, SB_TILE)
    tt = _tile(t, MIXER_ROWS)
    x2 = x.reshape(m, d)
    p5 = _proj(x2, wp["w_main"], tm, "od_proj").reshape(6, bx, t, W_HEADS)
    if k_cache is None:
        yc, kk, vv = _sb(p5, None, None, tq, True, "sb_prompt")
    else:
        past = k_cache.shape[1]
        yc, kk, vv = _sb(p5, k_cache.reshape(bx, past * H_C, DK), v_cache.reshape(bx, past * H_C, DK), tq, False,
                         "sb_sample")
    half = GB_D * P_D
    yd, hr, hi = _s5(p5, 4, 5, wp["bb"], wp["cc"], wp["a_re"], wp["a_im"], wp["d"], wp["w_glu"], wp["b_glu"],
                     h_re0.reshape(bx, NGB_D, half), h_im0.reshape(bx, NGB_D, half), tt, "s5_glu")
    xn = _outproj_ln(yc.reshape(m, -1), yd.reshape(m, -1), x2, wp["w_out"], wp["ln_g"], wp["ln_b"], alpha,
                     _tile(m, OUTPROJ_ROWS), "od_outproj_ln")
    return (xn.reshape(bx, t, d), kk.reshape(bx, t, H_C, DK), vv.reshape(bx, t, H_C, DK),
            hr.reshape(bx, G_D, P_D), hi.reshape(bx, G_D, P_D))


def kernel(x_prompt, x_sample, state_delta_S, state_delta_conv, state_mlstm_C, state_mlstm_n, state_mlstm_m, cache_sb_k, cache_sb_v, state_s5_re, state_s5_im, ev_w_in, ev_conv_w, ev_a_log, ev_dt_bias, ev_norm_a, ev_ig_bias, ev_fg_bias, ev_norm_b, ev_w_out, ev_ln_g, ev_ln_b, od_w_in, od_lam_re, od_lam_im, od_b_re, od_b_im, od_c_re, od_c_im, od_d, od_log_dt, od_w_glu, od_b_glu, od_w_out, od_ln_g, od_ln_b):
    n_even = ev_w_in.shape[0]
    n_odd = od_w_in.shape[0]
    depth = n_even + n_odd
    alpha = (2 * depth) ** 0.25
    nb, seq, _ = x_prompt.shape
    ns, dseq, _ = x_sample.shape
    xp, xs = x_prompt, x_sample
    outs = [[] for _ in range(18)]
    for layer in range(depth):
        j = layer // 2
        if layer % 2 == 0:
            wp = _pack_even_weights(ev_w_in[j], ev_conv_w[j], ev_a_log[j], ev_dt_bias[j], ev_norm_a[j], ev_ig_bias[j],
                                    ev_fg_bias[j], ev_norm_b[j], ev_w_out[j], ev_ln_g[j], ev_ln_b[j])
            zs = jnp.zeros((nb, H_A, DK, DK), F32)
            xp, s, cv, c, n, m = _even_layer(
                xp, zs, jnp.zeros((nb, CONV_W - 1, 3 * H_A * DK), F32), zs, jnp.zeros((nb, H_B, DK), F32),
                jnp.zeros((nb, H_B), F32), wp, alpha)
            for idx, val in zip((0, 2, 4, 6, 8), (s, cv, c, n, m)):
                outs[idx].append(val)
            xs, s, cv, c, n, m = _even_layer(
                xs, state_delta_S[j], state_delta_conv[j], state_mlstm_C[j], state_mlstm_n[j], state_mlstm_m[j],
                wp, alpha)
            for idx, val in zip((1, 3, 5, 7, 9), (s, cv, c, n, m)):
                outs[idx].append(val)
        else:
            wp = _pack_odd_weights(od_w_in[j], od_lam_re[j], od_lam_im[j], od_b_re[j], od_b_im[j], od_c_re[j],
                                   od_c_im[j], od_d[j], od_log_dt[j], od_w_glu[j], od_b_glu[j], od_w_out[j],
                                   od_ln_g[j], od_ln_b[j])
            zst = jnp.zeros((nb, G_D, P_D), F32)
            xp, k, v, hr, hi = _odd_layer(xp, None, None, zst, zst, wp, alpha)
            for idx, val in zip((10, 12, 14, 16), (k, v, hr, hi)):
                outs[idx].append(val)
            xs, k, v, hr, hi = _odd_layer(xs, cache_sb_k[j], cache_sb_v[j], state_s5_re[j], state_s5_im[j], wp, alpha)
            for idx, val in zip((11, 13, 15, 17), (k, v, hr, hi)):
                outs[idx].append(val)
    return (xp, xs) + tuple(jnp.stack(o) for o in outs)
```

```python
import functools

import jax
import jax.numpy as jnp
from jax import lax
from jax.experimental import pallas as pl
from jax.experimental.pallas import tpu as pltpu

F32 = jnp.float32
BF16 = jnp.bfloat16

LANES = 128
SUBLANES = 8
CHUNK = 64
CONV_W = 4
H_A = 8
H_B = 8
H_C = 8
DK = 128
GC_D = 16
G_D = 64
P_D = 64
GB_D = 8
NGB_D = G_D // GB_D
W_HEADS = 1024
LN_EPS = 1e-5
NORM_EPS = 1e-6
VMEM_LIMIT = 48 * 1024 * 1024
SB_DEAD_LOG_WEIGHT = -88.0
SB_TILE = 128
PROJ_ROWS = 1024
OUTPROJ_ROWS = 512
MIXER_ROWS = 512
S5_ROWS = 512
MLSTM_CHUNK_GROUP = 1
GDN_CHUNK_GROUP = 4


def _cparams(sem):
    return pltpu.CompilerParams(dimension_semantics=sem, vmem_limit_bytes=VMEM_LIMIT)


def _sigmoid(x):
    return 1.0 / (1.0 + jnp.exp(-x))


def _silu(x):
    return x * _sigmoid(x)


def _softplus(x):
    return jnp.maximum(x, 0.0) + jnp.log1p(jnp.exp(-jnp.abs(x)))


def _dot(a, b):
    return jnp.dot(a, b, preferred_element_type=F32)


def _dot_nt(a, b):
    return lax.dot_general(a, b, (((1,), (1,)), ((), ())), preferred_element_type=F32)


def _dot_tn(a, b):
    return lax.dot_general(a, b, (((0,), (0,)), ((), ())), preferred_element_type=F32)


def _split3(x):
    hi = x.astype(BF16)
    r1 = x - hi.astype(F32)
    mid = r1.astype(BF16)
    lo = (r1 - mid.astype(F32)).astype(BF16)
    return hi, mid, lo


def _dot_mask_lhs(mask_bf16, x):
    hi, mid, lo = _split3(x)
    return _dot(mask_bf16, hi) + _dot(mask_bf16, mid) + _dot(mask_bf16, lo)


def _dot_tn_split(x, mask_bf16):
    hi, mid, lo = _split3(x)
    return _dot_tn(hi, mask_bf16) + _dot_tn(mid, mask_bf16) + _dot_tn(lo, mask_bf16)


def _iota2(shape):
    return (lax.broadcasted_iota(jnp.int32, shape, 0), lax.broadcasted_iota(jnp.int32, shape, 1))


def _head(x, h):
    return x[:, h * DK:(h + 1) * DK]


def _proj_kernel(x_ref, w_ref, o_ref):
    o_ref[0] = _dot(x_ref[...].astype(BF16), w_ref[...])


def _proj_gates_kernel(x_ref, w_ref, wg_ref, o_ref, og_ref):
    xb = x_ref[...].astype(BF16)
    o_ref[0] = _dot(xb, w_ref[...])

    @pl.when(pl.program_id(1) == 0)
    def _():
        og_ref[...] = _dot(xb, wg_ref[...])


def _proj(x2, w, w_gate, tm, name):
    m, d = x2.shape
    wn = W_HEADS
    ng = w.shape[1] // wn
    x_spec = pl.BlockSpec((tm, d), lambda i, j: (i, 0))
    w_spec = pl.BlockSpec((d, wn), lambda i, j: (0, j))
    o_spec = pl.BlockSpec((1, tm, wn), lambda i, j: (j, i, 0))
    o_shape = jax.ShapeDtypeStruct((ng, m, wn), F32)
    if w_gate is None:
        return pl.pallas_call(
            _proj_kernel, grid=(m // tm, ng), in_specs=[x_spec, w_spec], out_specs=o_spec, out_shape=o_shape,
            compiler_params=_cparams(("parallel", "arbitrary")), name=name)(x2, w)
    return pl.pallas_call(
        _proj_gates_kernel,
        grid=(m // tm, ng),
        in_specs=[x_spec, w_spec, pl.BlockSpec((d, LANES), lambda i, j: (0, 0))],
        out_specs=[o_spec, pl.BlockSpec((tm, LANES), lambda i, j: (i, 0))],
        out_shape=[o_shape, jax.ShapeDtypeStruct((m, LANES), F32)],
        compiler_params=_cparams(("parallel", "arbitrary")),
        name=name,
    )(x2, w, w_gate)


def _outproj_ln_kernel(ya_ref, yb_ref, x_ref, wa_ref, wb_ref, g_ref, b_ref, o_ref, *, alpha):
    acc = _dot(ya_ref[...].astype(BF16), wa_ref[...]) + _dot(yb_ref[...].astype(BF16), wb_ref[...])
    r = alpha * x_ref[...] + acc
    mu = jnp.mean(r, axis=-1, keepdims=True)
    c = r - mu
    var = jnp.mean(c * c, axis=-1, keepdims=True)
    o_ref[...] = c * lax.rsqrt(var + LN_EPS) * g_ref[...] + b_ref[...]


def _outproj_ln(ya, yb, x2, w_out_bf16, ln_g, ln_b, alpha, tm, name):
    m, d = x2.shape
    wa = ya.shape[1]
    wb = yb.shape[1]
    return pl.pallas_call(
        functools.partial(_outproj_ln_kernel, alpha=alpha),
        grid=(m // tm,),
        in_specs=[pl.BlockSpec((tm, wa), lambda i: (i, 0)),
                  pl.BlockSpec((tm, wb), lambda i: (i, 0)),
                  pl.BlockSpec((tm, d), lambda i: (i, 0)),
                  pl.BlockSpec((wa, d), lambda i: (0, 0)),
                  pl.BlockSpec((wb, d), lambda i: (wa // wb, 0)),
                  pl.BlockSpec((1, d), lambda i: (0, 0)),
                  pl.BlockSpec((1, d), lambda i: (0, 0))],
        out_specs=pl.BlockSpec((tm, d), lambda i: (i, 0)),
        out_shape=jax.ShapeDtypeStruct((m, d), F32),
        compiler_params=_cparams(("parallel",)),
        name=name,
    )(ya, yb, x2, w_out_bf16, w_out_bf16, ln_g.reshape(1, d), ln_b.reshape(1, d))


def _conv_carry_init(carry, c0_ref, w_ref):
    w = w_ref[...]
    xm3, xm2, xm1 = c0_ref[5:6, :], c0_ref[6:7, :], c0_ref[7:8, :]
    carry[0:1, :] = w[0:1, :] * xm1
    carry[1:2, :] = w[1:2, :] * xm1 + w[0:1, :] * xm2
    carry[2:3, :] = w[2:3, :] * xm1 + (w[1:2, :] * xm2 + w[0:1, :] * xm3)


def _conv_silu(x_ref, carry, w_ref, tb):
    x = x_ref[...]
    w = w_ref[...]
    first = lax.broadcasted_iota(jnp.int32, (SUBLANES, x.shape[1]), 0) == 0
    acc = x * w[0:1, :]
    for i in range(1, CONV_W):
        shifted_in = carry[i - 1:i, :]
        carry[i - 1:i, :] = acc[tb - 1:tb, :]
        rolled = pltpu.roll(acc, 1, axis=0)
        head = jnp.where(first, shifted_in, rolled[0:SUBLANES])
        acc = x * w[i:i + 1, :] + jnp.concatenate([head, rolled[SUBLANES:]], axis=0)
    return _silu(acc)


def _l2norm(x):
    return x * lax.rsqrt(jnp.sum(x * x, axis=-1, keepdims=True) + NORM_EPS)


def _rms_norm(x, g):
    return x * lax.rsqrt(jnp.mean(x * x, axis=-1, keepdims=True) + NORM_EPS) * g


def _unit_lower_inverses(n_mats, length):
    r, c = _iota2((length, length))
    eye = jnp.where(r == c, 1.0, 0.0)
    xs = [eye + n for n in n_mats]
    ps = list(n_mats)
    for _ in range(length.bit_length() - 2):
        pbs = [p.astype(BF16) for p in ps]
        ps = [_dot(pb, pb) for pb in pbs]
        xs = [x + _dot(x.astype(BF16), p.astype(BF16)) for x, p in zip(xs, ps)]
    return xs


def _gdn_kernel(qp_ref, kp_ref, vp_ref, z_ref, gt_ref, c0q_ref, c0k_ref, c0v_ref,
                cwq_ref, cwk_ref, cwv_ref, prm_ref, norm_ref, s0_ref,
                y_ref, sout_ref, s_scr, xq_scr, xk_scr, xv_scr, *, chunk, tb):
    t = pl.program_id(1)
    nh = H_A

    @pl.when(t == 0)
    def _():
        s_scr[...] = s0_ref[...]
        _conv_carry_init(xq_scr, c0q_ref, cwq_ref)
        _conv_carry_init(xk_scr, c0k_ref, cwk_ref)
        _conv_carry_init(xv_scr, c0v_ref, cwv_ref)

    q_all = _conv_silu(qp_ref, xq_scr, cwq_ref, tb)
    k_all = _conv_silu(kp_ref, xk_scr, cwk_ref, tb)
    v_all = _conv_silu(vp_ref, xv_scr, cwv_ref, tb)
    q_h = [_l2norm(_head(q_all, h)) * (DK ** -0.5) for h in range(nh)]
    k_h = [_l2norm(_head(k_all, h)) for h in range(nh)]

    gt = gt_ref[...]
    prm = prm_ref[...]
    beta_all = _sigmoid(gt)
    g_all = -jnp.exp(prm[0:1, :]) * _softplus(gt + prm[1:2, :])

    r, c = _iota2((chunk, chunk))
    incl = r >= c
    strict = r > c
    a_incl = jnp.where(incl, 1.0, 0.0).astype(BF16)
    a_incl_t = jnp.where(r <= c, 1.0, 0.0).astype(BF16)
    norm_g = norm_ref[...]

    def precompute(cis):
        items = []
        for ci in cis:
            sl = slice(ci * chunk, (ci + 1) * chunk)
            g_c = g_all[sl]
            gc_cols = _dot_mask_lhs(a_incl, g_c)
            gc_rows = _dot_tn_split(g_c, a_incl_t)
            beta_c = beta_all[sl]
            for h in range(nh):
                items.append(dict(ci=ci, gcol=gc_cols[:, nh + h:nh + h + 1], grow=gc_rows[nh + h:nh + h + 1, :],
                                  bc=beta_c[:, h:h + 1], qc=q_h[h][sl], kc=k_h[h][sl], vc=_head(v_all, h)[sl]))
        qkks = [_dot_nt(jnp.concatenate([it["qc"], it["kc"]], axis=0).astype(BF16), it["kc"].astype(BF16))
                for it in items]
        decays = [jnp.where(incl, jnp.exp(it["gcol"] - it["grow"]), 0.0) for it in items]
        n_mats = [jnp.where(strict, -(it["bc"] * qkk[chunk:] * decay), 0.0)
                  for it, qkk, decay in zip(items, qkks, decays)]
        t_invs = _unit_lower_inverses(n_mats, chunk)
        gams = [jnp.exp(it["gcol"]) for it in items]
        rhss = [jnp.concatenate([it["vc"] * it["bc"], it["kc"] * (it["bc"] * gam)], axis=-1).astype(BF16)
                for it, gam in zip(items, gams)]
        sols = [_dot(ti.astype(BF16), rhs) for ti, rhs in zip(t_invs, rhss)]
        out = {ci: [] for ci in cis}
        for it, qkk, decay, gam, sol in zip(items, qkks, decays, gams, sols):
            g_last = it["gcol"][chunk - 1:chunk, :]
            out[it["ci"]].append(dict(
                u=sol[:, :DK],
                wq=jnp.concatenate([sol[:, DK:], it["qc"] * gam], axis=0).astype(BF16),
                qk=(qkk[:chunk] * decay).astype(BF16),
                k_dec=(it["kc"] * jnp.exp(g_last - it["gcol"])).astype(BF16),
                s_decay=jnp.exp(g_last)))
        return out

    def advance(ci, heads, states):
        sl = slice(ci * chunk, (ci + 1) * chunk)
        wss = [_dot(hd["wq"], s.astype(BF16)) for hd, s in zip(heads, states)]
        es = [(hd["u"] - ws[:chunk]).astype(BF16) for hd, ws in zip(heads, wss)]
        new_states = [hd["s_decay"] * s + _dot_tn(hd["k_dec"], e) for hd, s, e in zip(heads, states, es)]
        outs = [ws[chunk:] + _dot(hd["qk"], e) for hd, ws, e in zip(heads, wss, es)]
        for h in range(nh):
            y_ref[sl, h * DK:(h + 1) * DK] = (_rms_norm(outs[h], norm_g)
                                              * _silu(z_ref[sl, h * DK:(h + 1) * DK])).astype(y_ref.dtype)
        return new_states

    n_chunks = tb // chunk
    groups = [list(range(g0, min(g0 + GDN_CHUNK_GROUP, n_chunks))) for g0 in range(0, n_chunks, GDN_CHUNK_GROUP)]
    states = [s_scr[h] for h in range(nh)]
    pre = precompute(groups[0])
    for gi, group in enumerate(groups):
        nxt = precompute(groups[gi + 1]) if gi + 1 < len(groups) else None
        for ci in group:
            states = advance(ci, pre[ci], states)
        pre = nxt
    for h in range(nh):
        s_scr[h] = states[h]

    @pl.when(t == pl.num_programs(1) - 1)
    def _():
        for h in range(nh):
            sout_ref[h] = states[h]


def _gdn(p5, g3, conv0p, conv_w, prm, norm_a, s0, chunk, tb, name):
    _, bx, t, w = p5.shape
    nh = H_A

    def pspec(group):
        return pl.BlockSpec((None, None, tb, w), lambda b, i, g=group: (g, b, i, 0))

    def c0spec(group):
        return pl.BlockSpec((None, SUBLANES, w), lambda b, i, g=group: (b, 0, g))

    def cwspec(group):
        return pl.BlockSpec((CONV_W, w), lambda b, i, g=group: (0, g))

    return pl.pallas_call(
        functools.partial(_gdn_kernel, chunk=chunk, tb=tb),
        grid=(bx, t // tb),
        in_specs=[pspec(0), pspec(1), pspec(2), pspec(3),
                  pl.BlockSpec((None, tb, LANES), lambda b, i: (b, i, 0)),
                  c0spec(0), c0spec(1), c0spec(2),
                  cwspec(0), cwspec(1), cwspec(2),
                  pl.BlockSpec((SUBLANES, LANES), lambda b, i: (0, 0)),
                  pl.BlockSpec((1, DK), lambda b, i: (0, 0)),
                  pl.BlockSpec((None, nh, DK, DK), lambda b, i: (b, 0, 0, 0))],
        out_specs=[pl.BlockSpec((None, tb, w), lambda b, i: (b, i, 0)),
                   pl.BlockSpec((None, nh, DK, DK), lambda b, i: (b, 0, 0, 0))],
        out_shape=[jax.ShapeDtypeStruct((bx, t, w), BF16),
                   jax.ShapeDtypeStruct((bx, nh, DK, DK), F32)],
        scratch_shapes=[pltpu.VMEM((nh, DK, DK), F32),
                        pltpu.VMEM((SUBLANES, w), F32),
                        pltpu.VMEM((SUBLANES, w), F32),
                        pltpu.VMEM((SUBLANES, w), F32)],
        compiler_params=_cparams(("parallel", "arbitrary")),
        name=name,
    )(p5, p5, p5, p5, g3, conv0p, conv0p, conv0p, conv_w, conv_w, conv_w, prm, norm_a, s0)


def _mlstm_kernel(q_ref, k_ref, v_ref, o_ref, z_ref, gt_ref, prm_ref, norm_ref, c0_ref, n0_ref, m0_ref,
                  y_ref, cout_ref, nout_ref, mout_ref, cn_scr, m_scr, *, chunk, tb):
    t = pl.program_id(1)
    nh = H_B
    heads = range(nh)
    ig_lane = 2 * H_A
    lf_lane = 2 * H_A + H_B

    @pl.when(t == 0)
    def _():
        cn_scr[:, :, :DK] = c0_ref[...]
        cn_scr[:, :, DK:] = n0_ref[...]
        m_scr[...] = m0_ref[...]

    gt = gt_ref[...]
    prm = prm_ref[...]
    ig_all = gt + prm[2:3, :]
    lf_all = -_softplus(-(gt + prm[3:4, :]))

    r, c = _iota2((chunk, chunk))
    incl = r >= c
    a_incl = jnp.where(incl, 1.0, 0.0).astype(BF16)
    a_incl_t = jnp.where(r <= c, 1.0, 0.0).astype(BF16)
    eye = jnp.where(r == c, 1.0, 0.0).astype(BF16)
    ones_w = jnp.ones((chunk, DK), F32)
    ones_b = jnp.ones((chunk, DK), BF16)
    norm_g = norm_ref[...]

    def precompute(cis):
        items = []
        for ci in cis:
            sl = slice(ci * chunk, (ci + 1) * chunk)
            lf_c = lf_all[sl]
            ig_c = ig_all[sl]
            b_cols = _dot_mask_lhs(a_incl, lf_c)
            b_rows = _dot_tn_split(lf_c, a_incl_t)
            ig_rows = _dot_tn_split(ig_c, eye)
            for h in heads:
                items.append(dict(ci=ci, sl=sl, h=h, bcol=b_cols[:, lf_lane + h:lf_lane + h + 1],
                                  brow=b_rows[lf_lane + h:lf_lane + h + 1, :],
                                  igrow=ig_rows[ig_lane + h:ig_lane + h + 1, :],
                                  igcol=ig_c[:, ig_lane + h:ig_lane + h + 1]))
        d_logs = [jnp.where(incl, it["bcol"] - it["brow"] + it["igrow"], -jnp.inf) for it in items]
        m_intras = [jnp.max(d, axis=-1, keepdims=True) for d in d_logs]
        qbs = [q_ref[it["sl"], it["h"] * DK:(it["h"] + 1) * DK].astype(BF16) for it in items]
        kcs = [k_ref[it["sl"], it["h"] * DK:(it["h"] + 1) * DK] * (DK ** -0.5) for it in items]
        v1s = [jnp.concatenate([v_ref[it["sl"], it["h"] * DK:(it["h"] + 1) * DK].astype(BF16), ones_b], axis=1)
               for it in items]
        qks = [_dot_nt(qb, kc.astype(BF16)) for qb, kc in zip(qbs, kcs)]
        pq0s = [(jnp.exp(d - mi) * qk).astype(BF16) for d, mi, qk in zip(d_logs, m_intras, qks)]
        avs = [_dot(pq0, v1) for pq0, v1 in zip(pq0s, v1s)]
        out = {ci: [] for ci in cis}
        for it, mi, qb, kc, v1, av in zip(items, m_intras, qbs, kcs, v1s, avs):
            bcol = it["bcol"]
            out[it["ci"]].append(dict(
                b_w=bcol * ones_w, mi_w=mi * ones_w,
                tail_w=(bcol[chunk - 1:chunk, :] - bcol + it["igcol"]) * ones_w,
                qb=qb, kc=kc, v1=v1, av=av))
        return out

    def advance(ci, pre, cns, ms):
        sl = slice(ci * chunk, (ci + 1) * chunk)
        qcns = [_dot(pre[h]["qb"], cns[h].astype(BF16)) for h in heads]
        bms = [pre[h]["b_w"] + ms[h] for h in heads]
        m_ts = [jnp.maximum(bms[h], pre[h]["mi_w"]) for h in heads]
        corrs = [jnp.exp(pre[h]["mi_w"] - m_ts[h]) for h in heads]
        inters = [jnp.exp(bms[h] - m_ts[h]) for h in heads]
        m_lasts = [m_t[chunk - 1:chunk, :] for m_t in m_ts]
        i_lasts = [inter[chunk - 1:chunk, :] for inter in inters]
        kps = [(pre[h]["kc"] * jnp.exp(pre[h]["tail_w"] - m_lasts[h])).astype(BF16) for h in heads]
        new_cns = [jnp.concatenate([i_lasts[h], i_lasts[h]], axis=1) * cns[h] + _dot_tn(kps[h], pre[h]["v1"])
                   for h in heads]
        for h in heads:
            num = inters[h] * qcns[h][:, :DK] + corrs[h] * pre[h]["av"][:, :DK]
            den = inters[h] * qcns[h][:, DK:] + corrs[h] * pre[h]["av"][:, DK:]
            hh = num / jnp.maximum(jnp.abs(den), jnp.exp(-m_ts[h]))
            hh = _sigmoid(o_ref[sl, h * DK:(h + 1) * DK]) * hh
            y_ref[sl, h * DK:(h + 1) * DK] = (_rms_norm(hh, norm_g[:, h * DK:(h + 1) * DK])
                                              * _silu(z_ref[sl, h * DK:(h + 1) * DK])).astype(y_ref.dtype)
        return new_cns, m_lasts

    n_chunks = tb // chunk
    cns = [cn_scr[h] for h in heads]
    ms = [m_scr[h] for h in heads]
    groups = [list(range(g0, min(g0 + MLSTM_CHUNK_GROUP, n_chunks)))
              for g0 in range(0, n_chunks, MLSTM_CHUNK_GROUP)]
    pre = precompute(groups[0])
    for gi, group in enumerate(groups):
        nxt = precompute(groups[gi + 1]) if gi + 1 < len(groups) else None
        for ci in group:
            cns, ms = advance(ci, pre[ci], cns, ms)
        pre = nxt
    for h in heads:
        cn_scr[h] = cns[h]
        m_scr[h] = ms[h]

    @pl.when(t == pl.num_programs(1) - 1)
    def _():
        for h in heads:
            cout_ref[h] = cns[h][:, :DK]
            nout_ref[h] = cns[h][:, DK:]
            mout_ref[h] = ms[h]


def _mlstm(p5, g3, prm, norm_b, c0, n0w, m0w, chunk, tb, name):
    _, bx, t, w = p5.shape
    nh = H_B

    def pspec(group):
        return pl.BlockSpec((None, None, tb, w), lambda b, i, g=group: (g, b, i, 0))

    mat = pl.BlockSpec((None, nh, DK, DK), lambda b, i: (b, 0, 0, 0))
    row = pl.BlockSpec((None, nh, 1, DK), lambda b, i: (b, 0, 0, 0))
    return pl.pallas_call(
        functools.partial(_mlstm_kernel, chunk=chunk, tb=tb),
        grid=(bx, t // tb),
        in_specs=[pspec(4), pspec(5), pspec(6), pspec(7), pspec(8),
                  pl.BlockSpec((None, tb, LANES), lambda b, i: (b, i, 0)),
                  pl.BlockSpec((SUBLANES, LANES), lambda b, i: (0, 0)),
                  pl.BlockSpec((1, w), lambda b, i: (0, 0)),
                  mat, mat, row],
        out_specs=[pl.BlockSpec((None, tb, w), lambda b, i: (b, i, 0)), mat, mat, row],
        out_shape=[jax.ShapeDtypeStruct((bx, t, w), BF16),
                   jax.ShapeDtypeStruct((bx, nh, DK, DK), F32),
                   jax.ShapeDtypeStruct((bx, nh, DK, DK), F32),
                   jax.ShapeDtypeStruct((bx, nh, 1, DK), F32)],
        scratch_shapes=[pltpu.VMEM((nh, DK, 2 * DK), F32), pltpu.VMEM((nh, 1, DK), F32)],
        compiler_params=_cparams(("parallel", "arbitrary")),
        name=name,
    )(p5, p5, p5, p5, p5, g3, prm, norm_b, c0, n0w, m0w)


def _sb_tiles(qbs, kts, vts, runs, u_neg, causal):
    tk = kts[0].shape[0]
    zs = [_dot_nt(qb, kt.astype(BF16)) for qb, kt in zip(qbs, kts)]
    sps = [jnp.maximum(z, 0.0) + jnp.log(1.0 + jnp.exp(-jnp.abs(z))) for z in zs]
    if causal is not None:
        sps = [jnp.where(causal, sp, 0.0) for sp in sps]
    his = [sp.astype(BF16) for sp in sps]
    los = [(sp - hi.astype(F32)).astype(BF16) for sp, hi in zip(sps, his)]
    exts = [_dot(jnp.concatenate([hi, lo], axis=1), u_neg) for hi, lo in zip(his, los)]
    atts = [jnp.exp(z + ext[:, :tk] + run[:, :tk]) for z, ext, run in zip(zs, exts, runs)]
    if causal is not None:
        atts = [jnp.where(causal, att, 0.0) for att in atts]
    contribs = [_dot(att.astype(BF16), vt.astype(BF16)) for att, vt in zip(atts, vts)]
    return [run + ext[:, tk:] for run, ext in zip(runs, exts)], contribs


def _sb_kernel(q_ref, z_ref, kc_ref, vc_ref, k1_ref, v1_ref, kany_ref, vany_ref,
               y_ref, ko_ref, vo_ref, acc_scr, run_scr, q_scr, kbuf, vbuf, flag, sem,
               *, tq, tk, prev_blocks_per_step, prev_blocks_fixed, prev_token_rows, kgroup, vgroup):
    nh = H_C
    b = pl.program_id(0)
    i = pl.program_id(1)
    r, c = _iota2((tq, tq))
    causal = c < r
    def suffix_sum_matrix(n):
        rr, cc = _iota2((2 * n, n + LANES))
        rr = jnp.where(rr >= n, rr - n, rr)
        return jnp.where((rr >= cc) | (cc >= n), -1.0, 0.0).astype(BF16)

    u_diag = suffix_sum_matrix(tq)
    u_prev = suffix_sum_matrix(tk)
    q_scr[...] = (q_ref[...] * (DK ** -0.5)).astype(BF16)

    def q_head(h):
        return q_scr[:, h * DK:(h + 1) * DK]

    def set_flag():
        best = run_scr[0]
        for h in range(1, nh):
            best = jnp.maximum(best, run_scr[h])
        flag[0] = (jnp.max(best) > SB_DEAD_LOG_WEIGHT).astype(jnp.int32)

    heads = range(nh)
    kts = [kc_ref[:, h * DK:(h + 1) * DK] for h in heads]
    vts = [vc_ref[:, h * DK:(h + 1) * DK] for h in heads]
    for h in heads:
        ko_ref[pl.ds(h, tq, stride=nh), :] = kts[h]
        vo_ref[pl.ds(h, tq, stride=nh), :] = vts[h]
    runs, contribs = _sb_tiles([q_head(h) for h in heads], kts, vts, [jnp.zeros((tq, LANES), F32)] * nh, u_diag,
                               causal)
    for h in heads:
        run_scr[h] = runs[h]
        acc_scr[:, h * DK:(h + 1) * DK] = contribs[h]
    set_flag()

    def prev_tile(ref, h):
        if prev_token_rows:
            return ref[pl.ds(h, tk, stride=nh), :]
        return ref[:, h * DK:(h + 1) * DK]

    def stage(k_ref, v_ref):
        runs, contribs = _sb_tiles([q_head(h) for h in heads], [prev_tile(k_ref, h) for h in heads],
                                   [prev_tile(v_ref, h) for h in heads], [run_scr[h] for h in heads], u_prev, None)
        for h in heads:
            run_scr[h] = runs[h]
            acc_scr[:, h * DK:(h + 1) * DK] += contribs[h]
        set_flag()

    nprev = i * prev_blocks_per_step + prev_blocks_fixed

    @pl.when(jnp.logical_and(nprev >= 1, flag[0] > 0))
    def _():
        stage(k1_ref, v1_ref)

    def far_copy(any_ref, group, buf, j, slot):
        start = (nprev - j) * tk
        if prev_token_rows:
            src = any_ref.at[b, pl.ds(pl.multiple_of(start * nh, tk * nh), tk * nh), :]
        else:
            src = any_ref.at[group, b, pl.ds(pl.multiple_of(start, tk), tk), :]
        return pltpu.make_async_copy(src, buf, sem.at[slot])

    def cond(carry):
        j, go = carry
        return jnp.logical_and(j <= nprev, go > 0)

    def body(carry):
        j, _ = carry
        kcopy = far_copy(kany_ref, kgroup, kbuf, j, 0)
        vcopy = far_copy(vany_ref, vgroup, vbuf, j, 1)
        kcopy.start()
        vcopy.start()
        kcopy.wait()
        vcopy.wait()
        stage(kbuf, vbuf)
        return j + 1, flag[0]

    lax.while_loop(cond, body, (jnp.int32(2), flag[0]))
    y_ref[...] = (acc_scr[...] * _silu(z_ref[...])).astype(y_ref.dtype)


def _sb(p5, kprev, vprev, tq, causal_prev, name):
    _, bx, t, w = p5.shape
    nh = H_C
    tk = SB_TILE if not causal_prev else tq

    def blk(group):
        return pl.BlockSpec((None, None, tq, w), lambda b, i, g=group: (g, b, i, 0))

    if causal_prev:
        def near(group):
            return pl.BlockSpec((None, None, tk, w), lambda b, i, g=group: (g, b, jnp.maximum(i - 1, 0), 0))
        kprev, vprev = p5, p5
        near_specs = [near(1), near(2)]
        buf_shape = (tk, w)
        per_step, fixed = tq // tk, 0
    else:
        nblk = kprev.shape[1] // (tk * nh)
        last = pl.BlockSpec((None, tk * nh, DK), lambda b, i: (b, nblk - 1, 0))
        near_specs = [last, last]
        buf_shape = (tk * nh, DK)
        per_step, fixed = 0, nblk
    kern = functools.partial(_sb_kernel, tq=tq, tk=tk, prev_blocks_per_step=per_step, prev_blocks_fixed=fixed,
                             prev_token_rows=not causal_prev, kgroup=1, vgroup=2)
    rows_out = pl.BlockSpec((None, tq * nh, DK), lambda b, i: (b, i, 0))
    return pl.pallas_call(
        kern,
        grid=(bx, t // tq),
        in_specs=[blk(0), blk(3), blk(1), blk(2)] + near_specs
                 + [pl.BlockSpec(memory_space=pl.ANY), pl.BlockSpec(memory_space=pl.ANY)],
        out_specs=[pl.BlockSpec((None, tq, w), lambda b, i: (b, i, 0)), rows_out, rows_out],
        out_shape=[jax.ShapeDtypeStruct((bx, t, w), BF16),
                   jax.ShapeDtypeStruct((bx, t * nh, DK), F32),
                   jax.ShapeDtypeStruct((bx, t * nh, DK), F32)],
        scratch_shapes=[pltpu.VMEM((tq, w), F32), pltpu.VMEM((nh, tq, LANES), F32), pltpu.VMEM((tq, w), BF16),
                        pltpu.VMEM(buf_shape, F32), pltpu.VMEM(buf_shape, F32),
                        pltpu.SMEM((1,), jnp.int32), pltpu.SemaphoreType.DMA((2,))],
        compiler_params=_cparams(("parallel", "arbitrary")),
        name=name,
    )(p5, p5, p5, p5, kprev, vprev, kprev, vprev)


def _s5_kernel(u_ref, z_ref, bb_ref, cc_ref, ar_ref, ai_ref, d_ref, wg_ref, bg_ref, h0r_ref, h0i_ref,
               y_ref, hr_out, hi_out, hs, ys, hr_scr, hi_scr, *, tt):
    t = pl.program_id(1)
    half = GB_D * P_D
    nl = half // LANES

    @pl.when(t == 0)
    def _():
        hr_scr[...] = h0r_ref[...]
        hi_scr[...] = h0i_ref[...]

    for gb in range(NGB_D):
        rows = pl.ds(gb, tt, stride=NGB_D)
        bu = _dot(u_ref[:, gb * LANES:(gb + 1) * LANES].astype(BF16), bb_ref[gb])
        for j in range(2 * nl):
            hs[j, rows, :] = bu[:, j * LANES:(j + 1) * LANES]

    ar = [ar_ref[:, j * LANES:(j + 1) * LANES] for j in range(nl)]
    ai = [ai_ref[:, j * LANES:(j + 1) * LANES] for j in range(nl)]

    def step(s, carry):
        rows = pl.ds(pl.multiple_of(s * NGB_D, NGB_D), NGB_D)
        new = [None] * (2 * nl)
        for j in range(nl):
            hr, hi = carry[j], carry[nl + j]
            new[j] = ar[j] * hr - ai[j] * hi + hs[j, rows, :]
            new[nl + j] = ar[j] * hi + ai[j] * hr + hs[nl + j, rows, :]
        for j in range(2 * nl):
            hs[j, rows, :] = new[j]
        return tuple(new)

    init = tuple(hr_scr[:, j * LANES:(j + 1) * LANES] for j in range(nl)) + tuple(
        hi_scr[:, j * LANES:(j + 1) * LANES] for j in range(nl))
    fin = lax.fori_loop(0, tt, step, init, unroll=8)
    hr = jnp.concatenate(fin[:nl], axis=1)
    hi = jnp.concatenate(fin[nl:], axis=1)
    hr_scr[...] = hr
    hi_scr[...] = hi

    for gb in range(NGB_D):
        rows = pl.ds(gb, tt, stride=NGB_D)
        hcat = jnp.concatenate([hs[j, rows, :] for j in range(2 * nl)], axis=1)
        lanes = slice(gb * LANES, (gb + 1) * LANES)
        ys[:, lanes] = _dot(hcat.astype(BF16), cc_ref[gb]) + d_ref[:, lanes] * u_ref[:, lanes]

    y = ys[...]
    yg = 0.5 * y * (1.0 + jnp.tanh(0.7978845608028654 * (y + 0.044715 * (y * y * y))))
    gate = _sigmoid(_dot(yg.astype(BF16), wg_ref[...]) + bg_ref[...])
    y_ref[...] = (yg * gate * _silu(z_ref[...])).astype(y_ref.dtype)

    @pl.when(t == pl.num_programs(1) - 1)
    def _():
        hr_out[...] = hr
        hi_out[...] = hi


def _s5(p5, ugroup, zgroup, bb, cc, a_re, a_im, d, w_glu_bf16, b_glu, h0r, h0i, tt, name):
    _, bx, t, w = p5.shape
    half = GB_D * P_D
    st = pl.BlockSpec((None, NGB_D, half), lambda b, i: (b, 0, 0))
    prm = pl.BlockSpec((NGB_D, half), lambda b, i: (0, 0))
    row = pl.BlockSpec((1, w), lambda b, i: (0, 0))
    return pl.pallas_call(
        functools.partial(_s5_kernel, tt=tt),
        grid=(bx, t // tt),
        in_specs=[pl.BlockSpec((None, None, tt, w), lambda b, i: (ugroup, b, i, 0)),
                  pl.BlockSpec((None, None, tt, w), lambda b, i: (zgroup, b, i, 0)),
                  pl.BlockSpec((NGB_D, LANES, 2 * half), lambda b, i: (0, 0, 0)),
                  pl.BlockSpec((NGB_D, 2 * half, LANES), lambda b, i: (0, 0, 0)),
                  prm, prm, row,
                  pl.BlockSpec((w, w), lambda b, i: (0, 0)),
                  row, st, st],
        out_specs=[pl.BlockSpec((None, tt, w), lambda b, i: (b, i, 0)), st, st],
        out_shape=[jax.ShapeDtypeStruct((bx, t, w), BF16),
                   jax.ShapeDtypeStruct((bx, NGB_D, half), F32),
                   jax.ShapeDtypeStruct((bx, NGB_D, half), F32)],
        scratch_shapes=[pltpu.VMEM((2 * half // LANES, tt * NGB_D, LANES), F32),
                        pltpu.VMEM((tt, w), F32),
                        pltpu.VMEM((NGB_D, half), F32), pltpu.VMEM((NGB_D, half), F32)],
        compiler_params=_cparams(("parallel", "arbitrary")),
        name=name,
    )(p5, p5, bb, cc, a_re, a_im, d.reshape(1, w), w_glu_bf16, b_glu.reshape(1, w), h0r, h0i)


def _pack_even_weights(w_in, conv_w, a_log, dt_bias, norm_a, ig_bias, fg_bias, norm_b, w_out, ln_g, ln_b):
    d = w_in.shape[0]
    wa = H_A * DK
    wb = H_B * DK
    o = 0
    qkv_a = w_in[:, o:o + 3 * wa]; o += 3 * wa
    z_a = w_in[:, o:o + wa]; o += wa
    gate_a = w_in[:, o:o + 2 * H_A]; o += 2 * H_A
    main_b = w_in[:, o:o + 5 * wb]; o += 5 * wb
    gate_b = w_in[:, o:o + 2 * H_B]
    w_main = jnp.concatenate([qkv_a, z_a, main_b], axis=1).astype(BF16)
    n_gate = 2 * H_A + 2 * H_B
    w_gate = jnp.concatenate([gate_a, gate_b, jnp.zeros((d, LANES - n_gate), F32)], axis=1).astype(BF16)
    prm = jnp.zeros((SUBLANES, LANES), F32)
    prm = prm.at[0, H_A:2 * H_A].set(a_log)
    prm = prm.at[1, H_A:2 * H_A].set(dt_bias)
    prm = prm.at[2, 2 * H_A:2 * H_A + H_B].set(ig_bias)
    prm = prm.at[3, 2 * H_A + H_B:2 * H_A + 2 * H_B].set(fg_bias)
    return dict(w_main=w_main, w_gate=w_gate, conv_w=conv_w, prm=prm, norm_a=norm_a.reshape(1, DK),
                norm_b=norm_b.reshape(1, wb), w_out=w_out.astype(BF16), ln_g=ln_g, ln_b=ln_b)


def _tile(n, preferred):
    tile = min(n, preferred)
    assert n % tile == 0, (n, preferred)
    return tile


def _even_layer(x, s0, conv0, c0, n0, m0, wp, alpha):
    bx, t, d = x.shape
    m = bx * t
    chunk = min(CHUNK, t)
    tm = _tile(m, PROJ_ROWS)
    tb = _tile(t, MIXER_ROWS)
    x2 = x.reshape(m, d)
    p3, g2 = _proj(x2, wp["w_main"], wp["w_gate"], tm, "ev_proj")
    p5 = p3.reshape(9, bx, t, W_HEADS)
    g3 = g2.reshape(bx, t, LANES)
    conv0p = jnp.concatenate([jnp.zeros((bx, SUBLANES - (CONV_W - 1), conv0.shape[2]), F32), conv0], axis=1)
    ya, s_new = _gdn(p5, g3, conv0p, wp["conv_w"], wp["prm"], wp["norm_a"], s0, chunk, tb, "gdn")
    m0w = jnp.broadcast_to(m0[:, :, None, None], (bx, H_B, 1, DK))
    n0w = jnp.broadcast_to(n0[:, :, :, None], (bx, H_B, DK, DK))
    yb, c_new, n_new, m_new = _mlstm(p5, g3, wp["prm"], wp["norm_b"], c0, n0w, m0w, chunk, tb, "mlstm")
    xn = _outproj_ln(ya.reshape(m, -1), yb.reshape(m, -1), x2, wp["w_out"], wp["ln_g"], wp["ln_b"], alpha,
                     _tile(m, OUTPROJ_ROWS), "ev_outproj_ln")
    conv_new = jnp.concatenate([p5[0, :, t - (CONV_W - 1):], p5[1, :, t - (CONV_W - 1):], p5[2, :, t - (CONV_W - 1):]],
                               axis=-1)
    return xn.reshape(bx, t, d), s_new, conv_new, c_new, n_new[:, :, :, 0], m_new[:, :, 0, 0]


def _pack_odd_weights(w_in, lam_re, lam_im, b_re, b_im, c_re, c_im, d, log_dt, w_glu, b_glu, w_out, ln_g, ln_b):
    dm = w_in.shape[0]
    w_main = w_in.astype(BF16)
    dt = jnp.exp(log_dt)[:, None]
    mag = jnp.exp(lam_re * dt)
    a_re, a_im = mag * jnp.cos(lam_im * dt), mag * jnp.sin(lam_im * dt)
    den = lam_re * lam_re + lam_im * lam_im
    f_re = ((a_re - 1.0) * lam_re + a_im * lam_im) / den
    f_im = (a_im * lam_re - (a_re - 1.0) * lam_im) / den
    bb_re = f_re[..., None] * b_re - f_im[..., None] * b_im
    bb_im = f_re[..., None] * b_im + f_im[..., None] * b_re
    half = GB_D * P_D
    eye = jnp.eye(GB_D, dtype=F32)

    def blockdiag_in(bb):
        bbg = bb.reshape(NGB_D, GB_D, P_D, GC_D)
        return jnp.einsum("bgpc,gh->bgchp", bbg, eye).reshape(NGB_D, GB_D * GC_D, half)

    def blockdiag_out(cm):
        cg = cm.reshape(NGB_D, GB_D, GC_D, P_D)
        return jnp.einsum("bgcp,gh->bgphc", cg, eye).reshape(NGB_D, half, GB_D * GC_D)

    bb = jnp.concatenate([blockdiag_in(bb_re), blockdiag_in(bb_im)], axis=2).astype(BF16)
    cc = jnp.concatenate([blockdiag_out(c_re), -blockdiag_out(c_im)], axis=1).astype(BF16)
    return dict(w_main=w_main, bb=bb, cc=cc, a_re=a_re.reshape(NGB_D, half), a_im=a_im.reshape(NGB_D, half),
                d=d, w_glu=w_glu.astype(BF16), b_glu=b_glu, w_out=w_out.astype(BF16), ln_g=ln_g, ln_b=ln_b)


def _odd_layer(x, k_cache, v_cache, h_re0, h_im0, wp, alpha):
    bx, t, d = x.shape
    m = bx * t
    tm = _tile(m, PROJ_ROWS)
    tq = _tile(t, SB_TILE)
    tt = _tile(t, S5_ROWS)
    x2 = x.reshape(m, d)
    p5 = _proj(x2, wp["w_main"], None, tm, "od_proj").reshape(6, bx, t, W_HEADS)
    if k_cache is None:
        yc, kk, vv = _sb(p5, None, None, tq, True, "sb_prompt")
    else:
        past = k_cache.shape[1]
        yc, kk, vv = _sb(p5, k_cache.reshape(bx, past * H_C, DK), v_cache.reshape(bx, past * H_C, DK), tq, False,
                         "sb_sample")
    half = GB_D * P_D
    yd, hr, hi = _s5(p5, 4, 5, wp["bb"], wp["cc"], wp["a_re"], wp["a_im"], wp["d"], wp["w_glu"], wp["b_glu"],
                     h_re0.reshape(bx, NGB_D, half), h_im0.reshape(bx, NGB_D, half), tt, "s5_glu")
    xn = _outproj_ln(yc.reshape(m, -1), yd.reshape(m, -1), x2, wp["w_out"], wp["ln_g"], wp["ln_b"], alpha,
                     _tile(m, OUTPROJ_ROWS), "od_outproj_ln")
    return (xn.reshape(bx, t, d), kk.reshape(bx, t, H_C, DK), vv.reshape(bx, t, H_C, DK),
            hr.reshape(bx, G_D, P_D), hi.reshape(bx, G_D, P_D))


def kernel(x_prompt, x_sample, state_delta_S, state_delta_conv, state_mlstm_C, state_mlstm_n, state_mlstm_m, cache_sb_k, cache_sb_v, state_s5_re, state_s5_im, ev_w_in, ev_conv_w, ev_a_log, ev_dt_bias, ev_norm_a, ev_ig_bias, ev_fg_bias, ev_norm_b, ev_w_out, ev_ln_g, ev_ln_b, od_w_in, od_lam_re, od_lam_im, od_b_re, od_b_im, od_c_re, od_c_im, od_d, od_log_dt, od_w_glu, od_b_glu, od_w_out, od_ln_g, od_ln_b):
    n_even = ev_w_in.shape[0]
    n_odd = od_w_in.shape[0]
    depth = n_even + n_odd
    alpha = (2 * depth) ** 0.25
    nb, seq, _ = x_prompt.shape
    ns, dseq, _ = x_sample.shape
    xp, xs = x_prompt, x_sample
    outs = [[] for _ in range(18)]
    for layer in range(depth):
        j = layer // 2
        if layer % 2 == 0:
            wp = _pack_even_weights(ev_w_in[j], ev_conv_w[j], ev_a_log[j], ev_dt_bias[j], ev_norm_a[j], ev_ig_bias[j],
                                    ev_fg_bias[j], ev_norm_b[j], ev_w_out[j], ev_ln_g[j], ev_ln_b[j])
            zs = jnp.zeros((nb, H_A, DK, DK), F32)
            xp, s, cv, c, n, m = _even_layer(
                xp, zs, jnp.zeros((nb, CONV_W - 1, 3 * H_A * DK), F32), zs, jnp.zeros((nb, H_B, DK), F32),
                jnp.zeros((nb, H_B), F32), wp, alpha)
            for idx, val in zip((0, 2, 4, 6, 8), (s, cv, c, n, m)):
                outs[idx].append(val)
            xs, s, cv, c, n, m = _even_layer(
                xs, state_delta_S[j], state_delta_conv[j], state_mlstm_C[j], state_mlstm_n[j], state_mlstm_m[j],
                wp, alpha)
            for idx, val in zip((1, 3, 5, 7, 9), (s, cv, c, n, m)):
                outs[idx].append(val)
        else:
            wp = _pack_odd_weights(od_w_in[j], od_lam_re[j], od_lam_im[j], od_b_re[j], od_b_im[j], od_c_re[j],
                                   od_c_im[j], od_d[j], od_log_dt[j], od_w_glu[j], od_b_glu[j], od_w_out[j],
                                   od_ln_g[j], od_ln_b[j])
            zst = jnp.zeros((nb, G_D, P_D), F32)
            xp, k, v, hr, hi = _odd_layer(xp, None, None, zst, zst, wp, alpha)
            for idx, val in zip((10, 12, 14, 16), (k, v, hr, hi)):
                outs[idx].append(val)
            xs, k, v, hr, hi = _odd_layer(xs, cache_sb_k[j], cache_sb_v[j], state_s5_re[j], state_s5_im[j], wp, alpha)
            for idx, val in zip((11, 13, 15, 17), (k, v, hr, hi)):
                outs[idx].append(val)
    return (xp, xs) + tuple(jnp.stack(o) for o in outs)
```

```python
import functools

import jax
import jax.numpy as jnp
from jax import lax
from jax.experimental import pallas as pl
from jax.experimental.pallas import tpu as pltpu

F32 = jnp.float32
BF16 = jnp.bfloat16

LANES = 128
SUBLANES = 8
CHUNK = 64
CONV_W = 4
H_A = 8
H_B = 8
H_C = 8
DK = 128
GC_D = 16
G_D = 64
P_D = 64
GB_D = 8
NGB_D = G_D // GB_D
W_HEADS = 1024
LN_EPS = 1e-5
NORM_EPS = 1e-6
VMEM_LIMIT = 48 * 1024 * 1024
SB_DEAD_LOG_WEIGHT = -88.0
SB_TILE = 128
PROJ_ROWS = 1024
OUTPROJ_ROWS = 512
MIXER_ROWS = 512
S5_ROWS = 512
MLSTM_CHUNK_GROUP = 1
GDN_CHUNK_GROUP = 4


def _cparams(sem):
    return pltpu.CompilerParams(dimension_semantics=sem, vmem_limit_bytes=VMEM_LIMIT)


def _sigmoid(x):
    return 1.0 / (1.0 + jnp.exp(-x))


def _silu(x):
    return x * _sigmoid(x)


def _softplus(x):
    return jnp.maximum(x, 0.0) + jnp.log1p(jnp.exp(-jnp.abs(x)))


def _dot(a, b):
    return jnp.dot(a, b, preferred_element_type=F32)


def _dot_nt(a, b):
    return lax.dot_general(a, b, (((1,), (1,)), ((), ())), preferred_element_type=F32)


def _dot_tn(a, b):
    return lax.dot_general(a, b, (((0,), (0,)), ((), ())), preferred_element_type=F32)


def _split3(x):
    hi = x.astype(BF16)
    r1 = x - hi.astype(F32)
    mid = r1.astype(BF16)
    lo = (r1 - mid.astype(F32)).astype(BF16)
    return hi, mid, lo


def _dot_mask_lhs(mask_bf16, x):
    hi, mid, lo = _split3(x)
    return _dot(mask_bf16, hi) + _dot(mask_bf16, mid) + _dot(mask_bf16, lo)


def _dot_tn_split(x, mask_bf16):
    hi, mid, lo = _split3(x)
    return _dot_tn(hi, mask_bf16) + _dot_tn(mid, mask_bf16) + _dot_tn(lo, mask_bf16)


def _iota2(shape):
    return (lax.broadcasted_iota(jnp.int32, shape, 0), lax.broadcasted_iota(jnp.int32, shape, 1))


def _head(x, h):
    return x[:, h * DK:(h + 1) * DK]


def _proj_kernel(x_ref, w_ref, o_ref):
    o_ref[0] = _dot(x_ref[...].astype(BF16), w_ref[...])


def _proj_gates_kernel(x_ref, wa_ref, wb_ref, wg_ref, o_ref, og_ref, *, na):
    j = pl.program_id(1)
    xb = x_ref[...].astype(BF16)

    @pl.when(j < na)
    def _():
        o_ref[0] = _dot(xb, wa_ref[...])

    @pl.when(j >= na)
    def _():
        o_ref[0] = _dot(xb, wb_ref[...])

    @pl.when(j == 0)
    def _():
        og_ref[...] = _dot(xb, wg_ref[...])


def _proj(x2, w, tm, name):
    m, d = x2.shape
    wn = W_HEADS
    ng = w.shape[1] // wn
    return pl.pallas_call(
        _proj_kernel,
        grid=(m // tm, ng),
        in_specs=[pl.BlockSpec((tm, d), lambda i, j: (i, 0)),
                  pl.BlockSpec((d, wn), lambda i, j: (0, j))],
        out_specs=pl.BlockSpec((1, tm, wn), lambda i, j: (j, i, 0)),
        out_shape=jax.ShapeDtypeStruct((ng, m, wn), F32),
        compiler_params=_cparams(("parallel", "arbitrary")),
        name=name,
    )(x2, w)


def _proj_gates(x2, w_a, na, w_b, w_gate, tm, name):
    m, d = x2.shape
    wn = W_HEADS
    ng = na + w_b.shape[1] // wn
    return pl.pallas_call(
        functools.partial(_proj_gates_kernel, na=na),
        grid=(m // tm, ng),
        in_specs=[pl.BlockSpec((tm, d), lambda i, j: (i, 0)),
                  pl.BlockSpec((d, wn), lambda i, j: (0, jnp.minimum(j, na - 1))),
                  pl.BlockSpec((d, wn), lambda i, j: (0, jnp.maximum(j - na, 0))),
                  pl.BlockSpec((d, LANES), lambda i, j: (0, 0))],
        out_specs=[pl.BlockSpec((1, tm, wn), lambda i, j: (j, i, 0)),
                   pl.BlockSpec((tm, LANES), lambda i, j: (i, 0))],
        out_shape=[jax.ShapeDtypeStruct((ng, m, wn), F32), jax.ShapeDtypeStruct((m, LANES), F32)],
        compiler_params=_cparams(("parallel", "arbitrary")),
        name=name,
    )(x2, w_a, w_b, w_gate)


def _outproj_ln_kernel(ya_ref, yb_ref, x_ref, wa_ref, wb_ref, g_ref, b_ref, o_ref, *, alpha):
    acc = _dot(ya_ref[...].astype(BF16), wa_ref[...]) + _dot(yb_ref[...].astype(BF16), wb_ref[...])
    r = alpha * x_ref[...] + acc
    mu = jnp.mean(r, axis=-1, keepdims=True)
    c = r - mu
    var = jnp.mean(c * c, axis=-1, keepdims=True)
    o_ref[...] = c * lax.rsqrt(var + LN_EPS) * g_ref[...] + b_ref[...]


def _outproj_ln(ya, yb, x2, w_out_bf16, ln_g, ln_b, alpha, tm, name):
    m, d = x2.shape
    wa = ya.shape[1]
    wb = yb.shape[1]
    return pl.pallas_call(
        functools.partial(_outproj_ln_kernel, alpha=alpha),
        grid=(m // tm,),
        in_specs=[pl.BlockSpec((tm, wa), lambda i: (i, 0)),
                  pl.BlockSpec((tm, wb), lambda i: (i, 0)),
                  pl.BlockSpec((tm, d), lambda i: (i, 0)),
                  pl.BlockSpec((wa, d), lambda i: (0, 0)),
                  pl.BlockSpec((wb, d), lambda i: (wa // wb, 0)),
                  pl.BlockSpec((1, d), lambda i: (0, 0)),
                  pl.BlockSpec((1, d), lambda i: (0, 0))],
        out_specs=pl.BlockSpec((tm, d), lambda i: (i, 0)),
        out_shape=jax.ShapeDtypeStruct((m, d), F32),
        compiler_params=_cparams(("parallel",)),
        name=name,
    )(ya, yb, x2, w_out_bf16, w_out_bf16, ln_g.reshape(1, d), ln_b.reshape(1, d))


def _conv_carry_init(carry, c0_ref, w_ref):
    w = w_ref[...]
    xm3, xm2, xm1 = c0_ref[5:6, :], c0_ref[6:7, :], c0_ref[7:8, :]
    carry[0:1, :] = w[0:1, :] * xm1
    carry[1:2, :] = w[1:2, :] * xm1 + w[0:1, :] * xm2
    carry[2:3, :] = w[2:3, :] * xm1 + (w[1:2, :] * xm2 + w[0:1, :] * xm3)


def _conv_silu(x_ref, carry, w_ref, tb):
    x = x_ref[...]
    w = w_ref[...]
    first = lax.broadcasted_iota(jnp.int32, (SUBLANES, x.shape[1]), 0) == 0
    acc = x * w[0:1, :]
    for i in range(1, CONV_W):
        shifted_in = carry[i - 1:i, :]
        carry[i - 1:i, :] = acc[tb - 1:tb, :]
        rolled = pltpu.roll(acc, 1, axis=0)
        head = jnp.where(first, shifted_in, rolled[0:SUBLANES])
        acc = x * w[i:i + 1, :] + jnp.concatenate([head, rolled[SUBLANES:]], axis=0)
    return _silu(acc)


def _l2norm(x):
    return x * lax.rsqrt(jnp.sum(x * x, axis=-1, keepdims=True) + NORM_EPS)


def _rms_norm(x, g):
    return x * lax.rsqrt(jnp.mean(x * x, axis=-1, keepdims=True) + NORM_EPS) * g


def _unit_lower_inverses(n_mats, length):
    r, c = _iota2((length, length))
    eye = jnp.where(r == c, 1.0, 0.0)
    xs = [eye + n for n in n_mats]
    ps = list(n_mats)
    for _ in range(length.bit_length() - 2):
        pbs = [p.astype(BF16) for p in ps]
        ps = [_dot(pb, pb) for pb in pbs]
        xs = [x + _dot(x.astype(BF16), p.astype(BF16)) for x, p in zip(xs, ps)]
    return xs


def _gdn_kernel(qp_ref, kp_ref, vp_ref, z_ref, gt_ref, c0q_ref, c0k_ref, c0v_ref,
                cwq_ref, cwk_ref, cwv_ref, prm_ref, norm_ref, s0_ref,
                y_ref, sout_ref, s_scr, xq_scr, xk_scr, xv_scr, *, chunk, tb):
    t = pl.program_id(1)
    nh = H_A

    @pl.when(t == 0)
    def _():
        s_scr[...] = s0_ref[...]
        _conv_carry_init(xq_scr, c0q_ref, cwq_ref)
        _conv_carry_init(xk_scr, c0k_ref, cwk_ref)
        _conv_carry_init(xv_scr, c0v_ref, cwv_ref)

    q_all = _conv_silu(qp_ref, xq_scr, cwq_ref, tb)
    k_all = _conv_silu(kp_ref, xk_scr, cwk_ref, tb)
    v_all = _conv_silu(vp_ref, xv_scr, cwv_ref, tb)
    q_h = [_l2norm(_head(q_all, h)) * (DK ** -0.5) for h in range(nh)]
    k_h = [_l2norm(_head(k_all, h)) for h in range(nh)]

    gt = gt_ref[...]
    prm = prm_ref[...]
    beta_all = _sigmoid(gt)
    g_all = -jnp.exp(prm[0:1, :]) * _softplus(gt + prm[1:2, :])

    r, c = _iota2((chunk, chunk))
    incl = r >= c
    strict = r > c
    a_incl = jnp.where(incl, 1.0, 0.0).astype(BF16)
    a_incl_t = jnp.where(r <= c, 1.0, 0.0).astype(BF16)
    norm_g = norm_ref[...]

    def precompute(cis):
        items = []
        for ci in cis:
            sl = slice(ci * chunk, (ci + 1) * chunk)
            g_c = g_all[sl]
            gc_cols = _dot_mask_lhs(a_incl, g_c)
            gc_rows = _dot_tn_split(g_c, a_incl_t)
            beta_c = beta_all[sl]
            for h in range(nh):
                items.append(dict(ci=ci, gcol=gc_cols[:, nh + h:nh + h + 1], grow=gc_rows[nh + h:nh + h + 1, :],
                                  bc=beta_c[:, h:h + 1], qc=q_h[h][sl], kc=k_h[h][sl], vc=_head(v_all, h)[sl]))
        qkks = [_dot_nt(jnp.concatenate([it["qc"], it["kc"]], axis=0).astype(BF16), it["kc"].astype(BF16))
                for it in items]
        decays = [jnp.where(incl, jnp.exp(it["gcol"] - it["grow"]), 0.0) for it in items]
        n_mats = [jnp.where(strict, -(it["bc"] * qkk[chunk:] * decay), 0.0)
                  for it, qkk, decay in zip(items, qkks, decays)]
        t_invs = _unit_lower_inverses(n_mats, chunk)
        gams = [jnp.exp(it["gcol"]) for it in items]
        rhss = [jnp.concatenate([it["vc"] * it["bc"], it["kc"] * (it["bc"] * gam)], axis=-1).astype(BF16)
                for it, gam in zip(items, gams)]
        sols = [_dot(ti.astype(BF16), rhs) for ti, rhs in zip(t_invs, rhss)]
        out = {ci: [] for ci in cis}
        for it, qkk, decay, gam, sol in zip(items, qkks, decays, gams, sols):
            g_last = it["gcol"][chunk - 1:chunk, :]
            out[it["ci"]].append(dict(
                u=sol[:, :DK],
                wq=jnp.concatenate([sol[:, DK:], it["qc"] * gam], axis=0).astype(BF16),
                qk=(qkk[:chunk] * decay).astype(BF16),
                k_dec=(it["kc"] * jnp.exp(g_last - it["gcol"])).astype(BF16),
                s_decay=jnp.exp(g_last)))
        return out

    def advance(ci, heads, states):
        sl = slice(ci * chunk, (ci + 1) * chunk)
        wss = [_dot(hd["wq"], s.astype(BF16)) for hd, s in zip(heads, states)]
        es = [(hd["u"] - ws[:chunk]).astype(BF16) for hd, ws in zip(heads, wss)]
        new_states = [hd["s_decay"] * s + _dot_tn(hd["k_dec"], e) for hd, s, e in zip(heads, states, es)]
        outs = [ws[chunk:] + _dot(hd["qk"], e) for hd, ws, e in zip(heads, wss, es)]
        for h in range(nh):
            y_ref[sl, h * DK:(h + 1) * DK] = (_rms_norm(outs[h], norm_g)
                                              * _silu(z_ref[sl, h * DK:(h + 1) * DK])).astype(y_ref.dtype)
        return new_states

    n_chunks = tb // chunk
    groups = [list(range(g0, min(g0 + GDN_CHUNK_GROUP, n_chunks))) for g0 in range(0, n_chunks, GDN_CHUNK_GROUP)]
    states = [s_scr[h] for h in range(nh)]
    pre = precompute(groups[0])
    for gi, group in enumerate(groups):
        nxt = precompute(groups[gi + 1]) if gi + 1 < len(groups) else None
        for ci in group:
            states = advance(ci, pre[ci], states)
        pre = nxt
    for h in range(nh):
        s_scr[h] = states[h]

    @pl.when(t == pl.num_programs(1) - 1)
    def _():
        for h in range(nh):
            sout_ref[h] = states[h]


def _gdn(p5, g3, conv0p, conv_w, prm, norm_a, s0, chunk, tb, name):
    _, bx, t, w = p5.shape
    nh = H_A

    def pspec(group):
        return pl.BlockSpec((None, None, tb, w), lambda b, i, g=group: (g, b, i, 0))

    def c0spec(group):
        return pl.BlockSpec((None, SUBLANES, w), lambda b, i, g=group: (b, 0, g))

    def cwspec(group):
        return pl.BlockSpec((CONV_W, w), lambda b, i, g=group: (0, g))

    return pl.pallas_call(
        functools.partial(_gdn_kernel, chunk=chunk, tb=tb),
        grid=(bx, t // tb),
        in_specs=[pspec(0), pspec(1), pspec(2), pspec(3),
                  pl.BlockSpec((None, tb, LANES), lambda b, i: (b, i, 0)),
                  c0spec(0), c0spec(1), c0spec(2),
                  cwspec(0), cwspec(1), cwspec(2),
                  pl.BlockSpec((SUBLANES, LANES), lambda b, i: (0, 0)),
                  pl.BlockSpec((1, DK), lambda b, i: (0, 0)),
                  pl.BlockSpec((None, nh, DK, DK), lambda b, i: (b, 0, 0, 0))],
        out_specs=[pl.BlockSpec((None, tb, w), lambda b, i: (b, i, 0)),
                   pl.BlockSpec((None, nh, DK, DK), lambda b, i: (b, 0, 0, 0))],
        out_shape=[jax.ShapeDtypeStruct((bx, t, w), BF16),
                   jax.ShapeDtypeStruct((bx, nh, DK, DK), F32)],
        scratch_shapes=[pltpu.VMEM((nh, DK, DK), F32),
                        pltpu.VMEM((SUBLANES, w), F32),
                        pltpu.VMEM((SUBLANES, w), F32),
                        pltpu.VMEM((SUBLANES, w), F32)],
        compiler_params=_cparams(("parallel", "arbitrary")),
        name=name,
    )(p5, p5, p5, p5, g3, conv0p, conv0p, conv0p, conv_w, conv_w, conv_w, prm, norm_a, s0)


def _mlstm_kernel(q_ref, k_ref, v_ref, o_ref, z_ref, gt_ref, prm_ref, norm_ref, c0_ref, n0_ref, m0_ref,
                  y_ref, cout_ref, nout_ref, mout_ref, cn_scr, m_scr, *, chunk, tb):
    t = pl.program_id(1)
    nh = H_B
    heads = range(nh)
    ig_lane = 2 * H_A
    lf_lane = 2 * H_A + H_B

    @pl.when(t == 0)
    def _():
        cn_scr[:, :, :DK] = c0_ref[...]
        cn_scr[:, :, DK:] = n0_ref[...]
        m_scr[...] = m0_ref[...]

    gt = gt_ref[...]
    prm = prm_ref[...]
    ig_all = gt + prm[2:3, :]
    lf_all = -_softplus(-(gt + prm[3:4, :]))

    r, c = _iota2((chunk, chunk))
    incl = r >= c
    a_incl = jnp.where(incl, 1.0, 0.0).astype(BF16)
    a_incl_t = jnp.where(r <= c, 1.0, 0.0).astype(BF16)
    eye = jnp.where(r == c, 1.0, 0.0).astype(BF16)
    ones_w = jnp.ones((chunk, DK), F32)
    ones_b = jnp.ones((chunk, DK), BF16)
    norm_g = norm_ref[...]

    def precompute(cis):
        items = []
        for ci in cis:
            sl = slice(ci * chunk, (ci + 1) * chunk)
            lf_c = lf_all[sl]
            ig_c = ig_all[sl]
            b_cols = _dot_mask_lhs(a_incl, lf_c)
            b_rows = _dot_tn_split(lf_c, a_incl_t)
            ig_rows = _dot_tn_split(ig_c, eye)
            for h in heads:
                items.append(dict(ci=ci, sl=sl, h=h, bcol=b_cols[:, lf_lane + h:lf_lane + h + 1],
                                  brow=b_rows[lf_lane + h:lf_lane + h + 1, :],
                                  igrow=ig_rows[ig_lane + h:ig_lane + h + 1, :],
                                  igcol=ig_c[:, ig_lane + h:ig_lane + h + 1]))
        d_logs = [jnp.where(incl, it["bcol"] - it["brow"] + it["igrow"], -jnp.inf) for it in items]
        m_intras = [jnp.max(d, axis=-1, keepdims=True) for d in d_logs]
        qbs = [q_ref[it["sl"], it["h"] * DK:(it["h"] + 1) * DK].astype(BF16) for it in items]
        kcs = [k_ref[it["sl"], it["h"] * DK:(it["h"] + 1) * DK] * (DK ** -0.5) for it in items]
        v1s = [jnp.concatenate([v_ref[it["sl"], it["h"] * DK:(it["h"] + 1) * DK].astype(BF16), ones_b], axis=1)
               for it in items]
        qks = [_dot_nt(qb, kc.astype(BF16)) for qb, kc in zip(qbs, kcs)]
        pq0s = [(jnp.exp(d - mi) * qk).astype(BF16) for d, mi, qk in zip(d_logs, m_intras, qks)]
        avs = [_dot(pq0, v1) for pq0, v1 in zip(pq0s, v1s)]
        out = {ci: [] for ci in cis}
        for it, mi, qb, kc, v1, av in zip(items, m_intras, qbs, kcs, v1s, avs):
            bcol = it["bcol"]
            out[it["ci"]].append(dict(
                b_w=bcol * ones_w, mi_w=mi * ones_w,
                tail_w=(bcol[chunk - 1:chunk, :] - bcol + it["igcol"]) * ones_w,
                qb=qb, kc=kc, v1=v1, av=av))
        return out

    def advance(ci, pre, cns, ms):
        sl = slice(ci * chunk, (ci + 1) * chunk)
        qcns = [_dot(pre[h]["qb"], cns[h].astype(BF16)) for h in heads]
        bms = [pre[h]["b_w"] + ms[h] for h in heads]
        m_ts = [jnp.maximum(bms[h], pre[h]["mi_w"]) for h in heads]
        corrs = [jnp.exp(pre[h]["mi_w"] - m_ts[h]) for h in heads]
        inters = [jnp.exp(bms[h] - m_ts[h]) for h in heads]
        m_lasts = [m_t[chunk - 1:chunk, :] for m_t in m_ts]
        i_lasts = [inter[chunk - 1:chunk, :] for inter in inters]
        kps = [(pre[h]["kc"] * jnp.exp(pre[h]["tail_w"] - m_lasts[h])).astype(BF16) for h in heads]
        new_cns = [jnp.concatenate([i_lasts[h], i_lasts[h]], axis=1) * cns[h] + _dot_tn(kps[h], pre[h]["v1"])
                   for h in heads]
        for h in heads:
            num = inters[h] * qcns[h][:, :DK] + corrs[h] * pre[h]["av"][:, :DK]
            den = inters[h] * qcns[h][:, DK:] + corrs[h] * pre[h]["av"][:, DK:]
            hh = num / jnp.maximum(jnp.abs(den), jnp.exp(-m_ts[h]))
            hh = _sigmoid(o_ref[sl, h * DK:(h + 1) * DK]) * hh
            y_ref[sl, h * DK:(h + 1) * DK] = (_rms_norm(hh, norm_g[:, h * DK:(h + 1) * DK])
                                              * _silu(z_ref[sl, h * DK:(h + 1) * DK])).astype(y_ref.dtype)
        return new_cns, m_lasts

    n_chunks = tb // chunk
    cns = [cn_scr[h] for h in heads]
    ms = [m_scr[h] for h in heads]
    groups = [list(range(g0, min(g0 + MLSTM_CHUNK_GROUP, n_chunks)))
              for g0 in range(0, n_chunks, MLSTM_CHUNK_GROUP)]
    pre = precompute(groups[0])
    for gi, group in enumerate(groups):
        nxt = precompute(groups[gi + 1]) if gi + 1 < len(groups) else None
        for ci in group:
            cns, ms = advance(ci, pre[ci], cns, ms)
        pre = nxt
    for h in heads:
        cn_scr[h] = cns[h]
        m_scr[h] = ms[h]

    @pl.when(t == pl.num_programs(1) - 1)
    def _():
        for h in heads:
            cout_ref[h] = cns[h][:, :DK]
            nout_ref[h] = cns[h][:, DK:]
            mout_ref[h] = ms[h]


def _mlstm(p5, g3, prm, norm_b, c0, n0w, m0w, chunk, tb, name):
    _, bx, t, w = p5.shape
    nh = H_B

    def pspec(group):
        return pl.BlockSpec((None, None, tb, w), lambda b, i, g=group: (g, b, i, 0))

    mat = pl.BlockSpec((None, nh, DK, DK), lambda b, i: (b, 0, 0, 0))
    row = pl.BlockSpec((None, nh, 1, DK), lambda b, i: (b, 0, 0, 0))
    return pl.pallas_call(
        functools.partial(_mlstm_kernel, chunk=chunk, tb=tb),
        grid=(bx, t // tb),
        in_specs=[pspec(4), pspec(5), pspec(6), pspec(7), pspec(8),
                  pl.BlockSpec((None, tb, LANES), lambda b, i: (b, i, 0)),
                  pl.BlockSpec((SUBLANES, LANES), lambda b, i: (0, 0)),
                  pl.BlockSpec((1, w), lambda b, i: (0, 0)),
                  mat, mat, row],
        out_specs=[pl.BlockSpec((None, tb, w), lambda b, i: (b, i, 0)), mat, mat, row],
        out_shape=[jax.ShapeDtypeStruct((bx, t, w), BF16),
                   jax.ShapeDtypeStruct((bx, nh, DK, DK), F32),
                   jax.ShapeDtypeStruct((bx, nh, DK, DK), F32),
                   jax.ShapeDtypeStruct((bx, nh, 1, DK), F32)],
        scratch_shapes=[pltpu.VMEM((nh, DK, 2 * DK), F32), pltpu.VMEM((nh, 1, DK), F32)],
        compiler_params=_cparams(("parallel", "arbitrary")),
        name=name,
    )(p5, p5, p5, p5, p5, g3, prm, norm_b, c0, n0w, m0w)


def _sb_tiles(qbs, kts, vts, runs, u_neg, causal):
    tk = kts[0].shape[0]
    zs = [_dot_nt(qb, kt.astype(BF16)) for qb, kt in zip(qbs, kts)]
    sps = [jnp.maximum(z, 0.0) + jnp.log(1.0 + jnp.exp(-jnp.abs(z))) for z in zs]
    if causal is not None:
        sps = [jnp.where(causal, sp, 0.0) for sp in sps]
    his = [sp.astype(BF16) for sp in sps]
    los = [(sp - hi.astype(F32)).astype(BF16) for sp, hi in zip(sps, his)]
    exts = [_dot(jnp.concatenate([hi, lo], axis=1), u_neg) for hi, lo in zip(his, los)]
    atts = [jnp.exp(z + ext[:, :tk] + run[:, :tk]) for z, ext, run in zip(zs, exts, runs)]
    if causal is not None:
        atts = [jnp.where(causal, att, 0.0) for att in atts]
    contribs = [_dot(att.astype(BF16), vt.astype(BF16)) for att, vt in zip(atts, vts)]
    return [run + ext[:, tk:] for run, ext in zip(runs, exts)], contribs


def _sb_kernel(q_ref, z_ref, kc_ref, vc_ref, k1_ref, v1_ref, kany_ref, vany_ref,
               y_ref, ko_ref, vo_ref, acc_scr, run_scr, q_scr, kbuf, vbuf, flag, sem,
               *, tq, tk, prev_blocks_per_step, prev_blocks_fixed, prev_token_rows, kgroup, vgroup):
    nh = H_C
    b = pl.program_id(0)
    i = pl.program_id(1)
    r, c = _iota2((tq, tq))
    causal = c < r
    def suffix_sum_matrix(n):
        rr, cc = _iota2((2 * n, n + LANES))
        rr = jnp.where(rr >= n, rr - n, rr)
        return jnp.where((rr >= cc) | (cc >= n), -1.0, 0.0).astype(BF16)

    u_diag = suffix_sum_matrix(tq)
    u_prev = suffix_sum_matrix(tk)
    q_scr[...] = (q_ref[...] * (DK ** -0.5)).astype(BF16)

    def q_head(h):
        return q_scr[:, h * DK:(h + 1) * DK]

    def set_flag():
        best = run_scr[0]
        for h in range(1, nh):
            best = jnp.maximum(best, run_scr[h])
        flag[0] = (jnp.max(best) > SB_DEAD_LOG_WEIGHT).astype(jnp.int32)

    heads = range(nh)
    kts = [kc_ref[:, h * DK:(h + 1) * DK] for h in heads]
    vts = [vc_ref[:, h * DK:(h + 1) * DK] for h in heads]
    for h in heads:
        ko_ref[pl.ds(h, tq, stride=nh), :] = kts[h]
        vo_ref[pl.ds(h, tq, stride=nh), :] = vts[h]
    runs, contribs = _sb_tiles([q_head(h) for h in heads], kts, vts, [jnp.zeros((tq, LANES), F32)] * nh, u_diag,
                               causal)
    for h in heads:
        run_scr[h] = runs[h]
        acc_scr[:, h * DK:(h + 1) * DK] = contribs[h]
    flag[0] = 1

    def prev_tile(ref, h):
        if prev_token_rows:
            return ref[pl.ds(h, tk, stride=nh), :]
        return ref[:, h * DK:(h + 1) * DK]

    def stage(k_ref, v_ref):
        runs, contribs = _sb_tiles([q_head(h) for h in heads], [prev_tile(k_ref, h) for h in heads],
                                   [prev_tile(v_ref, h) for h in heads], [run_scr[h] for h in heads], u_prev, None)
        for h in heads:
            run_scr[h] = runs[h]
            acc_scr[:, h * DK:(h + 1) * DK] += contribs[h]
        set_flag()

    nprev = i * prev_blocks_per_step + prev_blocks_fixed

    @pl.when(jnp.logical_and(nprev >= 1, flag[0] > 0))
    def _():
        stage(k1_ref, v1_ref)

    def far_copy(any_ref, group, buf, j, slot):
        start = (nprev - j) * tk
        if prev_token_rows:
            src = any_ref.at[b, pl.ds(pl.multiple_of(start * nh, tk * nh), tk * nh), :]
        else:
            src = any_ref.at[group, b, pl.ds(pl.multiple_of(start, tk), tk), :]
        return pltpu.make_async_copy(src, buf, sem.at[slot])

    def cond(carry):
        j, go = carry
        return jnp.logical_and(j <= nprev, go > 0)

    def body(carry):
        j, _ = carry
        kcopy = far_copy(kany_ref, kgroup, kbuf, j, 0)
        vcopy = far_copy(vany_ref, vgroup, vbuf, j, 1)
        kcopy.start()
        vcopy.start()
        kcopy.wait()
        vcopy.wait()
        stage(kbuf, vbuf)
        return j + 1, flag[0]

    lax.while_loop(cond, body, (jnp.int32(2), flag[0]))
    y_ref[...] = (acc_scr[...] * _silu(z_ref[...])).astype(y_ref.dtype)


def _sb(p5, kprev, vprev, tq, causal_prev, name):
    _, bx, t, w = p5.shape
    nh = H_C
    tk = SB_TILE if not causal_prev else tq

    def blk(group):
        return pl.BlockSpec((None, None, tq, w), lambda b, i, g=group: (g, b, i, 0))

    if causal_prev:
        def near(group):
            return pl.BlockSpec((None, None, tk, w), lambda b, i, g=group: (g, b, jnp.maximum(i - 1, 0), 0))
        kprev, vprev = p5, p5
        near_specs = [near(1), near(2)]
        buf_shape = (tk, w)
        per_step, fixed = tq // tk, 0
    else:
        nblk = kprev.shape[1] // (tk * nh)
        last = pl.BlockSpec((None, tk * nh, DK), lambda b, i: (b, nblk - 1, 0))
        near_specs = [last, last]
        buf_shape = (tk * nh, DK)
        per_step, fixed = 0, nblk
    kern = functools.partial(_sb_kernel, tq=tq, tk=tk, prev_blocks_per_step=per_step, prev_blocks_fixed=fixed,
                             prev_token_rows=not causal_prev, kgroup=1, vgroup=2)
    rows_out = pl.BlockSpec((None, tq * nh, DK), lambda b, i: (b, i, 0))
    return pl.pallas_call(
        kern,
        grid=(bx, t // tq),
        in_specs=[blk(0), blk(3), blk(1), blk(2)] + near_specs
                 + [pl.BlockSpec(memory_space=pl.ANY), pl.BlockSpec(memory_space=pl.ANY)],
        out_specs=[pl.BlockSpec((None, tq, w), lambda b, i: (b, i, 0)), rows_out, rows_out],
        out_shape=[jax.ShapeDtypeStruct((bx, t, w), BF16),
                   jax.ShapeDtypeStruct((bx, t * nh, DK), F32),
                   jax.ShapeDtypeStruct((bx, t * nh, DK), F32)],
        scratch_shapes=[pltpu.VMEM((tq, w), F32), pltpu.VMEM((nh, tq, LANES), F32), pltpu.VMEM((tq, w), BF16),
                        pltpu.VMEM(buf_shape, F32), pltpu.VMEM(buf_shape, F32),
                        pltpu.SMEM((1,), jnp.int32), pltpu.SemaphoreType.DMA((2,))],
        compiler_params=_cparams(("parallel", "arbitrary")),
        name=name,
    )(p5, p5, p5, p5, kprev, vprev, kprev, vprev)


def _s5_kernel(u_ref, z_ref, bb_ref, cc_ref, ar_ref, ai_ref, d_ref, wg_ref, bg_ref, h0r_ref, h0i_ref,
               y_ref, hr_out, hi_out, hs, ys, hr_scr, hi_scr, *, tt):
    t = pl.program_id(1)
    half = GB_D * P_D
    nl = half // LANES

    @pl.when(t == 0)
    def _():
        hr_scr[...] = h0r_ref[...]
        hi_scr[...] = h0i_ref[...]

    for gb in range(NGB_D):
        rows = pl.ds(gb, tt, stride=NGB_D)
        bu = _dot(u_ref[:, gb * LANES:(gb + 1) * LANES].astype(BF16), bb_ref[gb])
        for j in range(2 * nl):
            hs[j, rows, :] = bu[:, j * LANES:(j + 1) * LANES]

    ar = [ar_ref[:, j * LANES:(j + 1) * LANES] for j in range(nl)]
    ai = [ai_ref[:, j * LANES:(j + 1) * LANES] for j in range(nl)]

    def step(s, carry):
        rows = pl.ds(pl.multiple_of(s * NGB_D, NGB_D), NGB_D)
        new = [None] * (2 * nl)
        for j in range(nl):
            hr, hi = carry[j], carry[nl + j]
            new[j] = ar[j] * hr - ai[j] * hi + hs[j, rows, :]
            new[nl + j] = ar[j] * hi + ai[j] * hr + hs[nl + j, rows, :]
        for j in range(2 * nl):
            hs[j, rows, :] = new[j]
        return tuple(new)

    init = tuple(hr_scr[:, j * LANES:(j + 1) * LANES] for j in range(nl)) + tuple(
        hi_scr[:, j * LANES:(j + 1) * LANES] for j in range(nl))
    fin = lax.fori_loop(0, tt, step, init, unroll=8)
    hr = jnp.concatenate(fin[:nl], axis=1)
    hi = jnp.concatenate(fin[nl:], axis=1)
    hr_scr[...] = hr
    hi_scr[...] = hi

    for gb in range(NGB_D):
        rows = pl.ds(gb, tt, stride=NGB_D)
        hcat = jnp.concatenate([hs[j, rows, :] for j in range(2 * nl)], axis=1)
        lanes = slice(gb * LANES, (gb + 1) * LANES)
        ys[:, lanes] = _dot(hcat.astype(BF16), cc_ref[gb]) + d_ref[:, lanes] * u_ref[:, lanes]

    y = ys[...]
    yg = 0.5 * y * (1.0 + jnp.tanh(0.7978845608028654 * (y + 0.044715 * (y * y * y))))
    gate = _sigmoid(_dot(yg.astype(BF16), wg_ref[...]) + bg_ref[...])
    y_ref[...] = (yg * gate * _silu(z_ref[...])).astype(y_ref.dtype)

    @pl.when(t == pl.num_programs(1) - 1)
    def _():
        hr_out[...] = hr
        hi_out[...] = hi


def _s5(p5, ugroup, zgroup, bb, cc, a_re, a_im, d, w_glu_bf16, b_glu, h0r, h0i, tt, name):
    _, bx, t, w = p5.shape
    half = GB_D * P_D
    st = pl.BlockSpec((None, NGB_D, half), lambda b, i: (b, 0, 0))
    prm = pl.BlockSpec((NGB_D, half), lambda b, i: (0, 0))
    row = pl.BlockSpec((1, w), lambda b, i: (0, 0))
    return pl.pallas_call(
        functools.partial(_s5_kernel, tt=tt),
        grid=(bx, t // tt),
        in_specs=[pl.BlockSpec((None, None, tt, w), lambda b, i: (ugroup, b, i, 0)),
                  pl.BlockSpec((None, None, tt, w), lambda b, i: (zgroup, b, i, 0)),
                  pl.BlockSpec((NGB_D, LANES, 2 * half), lambda b, i: (0, 0, 0)),
                  pl.BlockSpec((NGB_D, 2 * half, LANES), lambda b, i: (0, 0, 0)),
                  prm, prm, row,
                  pl.BlockSpec((w, w), lambda b, i: (0, 0)),
                  row, st, st],
        out_specs=[pl.BlockSpec((None, tt, w), lambda b, i: (b, i, 0)), st, st],
        out_shape=[jax.ShapeDtypeStruct((bx, t, w), BF16),
                   jax.ShapeDtypeStruct((bx, NGB_D, half), F32),
                   jax.ShapeDtypeStruct((bx, NGB_D, half), F32)],
        scratch_shapes=[pltpu.VMEM((2 * half // LANES, tt * NGB_D, LANES), F32),
                        pltpu.VMEM((tt, w), F32),
                        pltpu.VMEM((NGB_D, half), F32), pltpu.VMEM((NGB_D, half), F32)],
        compiler_params=_cparams(("parallel", "arbitrary")),
        name=name,
    )(p5, p5, bb, cc, a_re, a_im, d.reshape(1, w), w_glu_bf16, b_glu.reshape(1, w), h0r, h0i)


def _pack_even_weights(w_in, conv_w, a_log, dt_bias, norm_a, ig_bias, fg_bias, norm_b, w_out, ln_g, ln_b):
    d = w_in.shape[0]
    wa = H_A * DK
    wb = H_B * DK
    w_all = w_in.astype(BF16)
    n_a = 4 * wa
    o = n_a
    gate_a = w_all[:, o:o + 2 * H_A]; o += 2 * H_A
    w_b = w_all[:, o:o + 5 * wb]; o += 5 * wb
    gate_b = w_all[:, o:o + 2 * H_B]
    n_gate = 2 * H_A + 2 * H_B
    w_gate = jnp.concatenate([gate_a, gate_b, jnp.zeros((d, LANES - n_gate), BF16)], axis=1)
    prm = jnp.zeros((SUBLANES, LANES), F32)
    prm = prm.at[0, H_A:2 * H_A].set(a_log)
    prm = prm.at[1, H_A:2 * H_A].set(dt_bias)
    prm = prm.at[2, 2 * H_A:2 * H_A + H_B].set(ig_bias)
    prm = prm.at[3, 2 * H_A + H_B:2 * H_A + 2 * H_B].set(fg_bias)
    return dict(w_all=w_all, n_a_groups=n_a // W_HEADS, w_b=w_b, w_gate=w_gate, conv_w=conv_w, prm=prm,
                norm_a=norm_a.reshape(1, DK),
                norm_b=norm_b.reshape(1, wb), w_out=w_out.astype(BF16), ln_g=ln_g, ln_b=ln_b)


def _tile(n, preferred):
    tile = min(n, preferred)
    assert n % tile == 0, (n, preferred)
    return tile


def _even_layer(x, s0, conv0, c0, n0, m0, wp, alpha):
    bx, t, d = x.shape
    m = bx * t
    chunk = min(CHUNK, t)
    tm = _tile(m, PROJ_ROWS)
    tb = _tile(t, MIXER_ROWS)
    x2 = x.reshape(m, d)
    p3, g2 = _proj_gates(x2, wp["w_all"], wp["n_a_groups"], wp["w_b"], wp["w_gate"], tm, "ev_proj")
    p5 = p3.reshape(9, bx, t, W_HEADS)
    g3 = g2.reshape(bx, t, LANES)
    conv0p = jnp.concatenate([jnp.zeros((bx, SUBLANES - (CONV_W - 1), conv0.shape[2]), F32), conv0], axis=1)
    ya, s_new = _gdn(p5, g3, conv0p, wp["conv_w"], wp["prm"], wp["norm_a"], s0, chunk, tb, "gdn")
    m0w = jnp.broadcast_to(m0[:, :, None, None], (bx, H_B, 1, DK))
    n0w = jnp.broadcast_to(n0[:, :, :, None], (bx, H_B, DK, DK))
    yb, c_new, n_new, m_new = _mlstm(p5, g3, wp["prm"], wp["norm_b"], c0, n0w, m0w, chunk, tb, "mlstm")
    xn = _outproj_ln(ya.reshape(m, -1), yb.reshape(m, -1), x2, wp["w_out"], wp["ln_g"], wp["ln_b"], alpha,
                     _tile(m, OUTPROJ_ROWS), "ev_outproj_ln")
    conv_new = jnp.concatenate([p5[0, :, t - (CONV_W - 1):], p5[1, :, t - (CONV_W - 1):], p5[2, :, t - (CONV_W - 1):]],
                               axis=-1)
    return xn.reshape(bx, t, d), s_new, conv_new, c_new, n_new[:, :, :, 0], m_new[:, :, 0, 0]


def _pack_odd_weights(w_in, lam_re, lam_im, b_re, b_im, c_re, c_im, d, log_dt, w_glu, b_glu, w_out, ln_g, ln_b):
    dm = w_in.shape[0]
    w_main = w_in.astype(BF16)
    dt = jnp.exp(log_dt)[:, None]
    mag = jnp.exp(lam_re * dt)
    a_re, a_im = mag * jnp.cos(lam_im * dt), mag * jnp.sin(lam_im * dt)
    den = lam_re * lam_re + lam_im * lam_im
    f_re = ((a_re - 1.0) * lam_re + a_im * lam_im) / den
    f_im = (a_im * lam_re - (a_re - 1.0) * lam_im) / den
    bb_re = f_re[..., None] * b_re - f_im[..., None] * b_im
    bb_im = f_re[..., None] * b_im + f_im[..., None] * b_re
    half = GB_D * P_D
    eye = jnp.eye(GB_D, dtype=F32)

    def blockdiag_in(bb):
        bbg = bb.reshape(NGB_D, GB_D, P_D, GC_D)
        return jnp.einsum("bgpc,gh->bgchp", bbg, eye).reshape(NGB_D, GB_D * GC_D, half)

    def blockdiag_out(cm):
        cg = cm.reshape(NGB_D, GB_D, GC_D, P_D)
        return jnp.einsum("bgcp,gh->bgphc", cg, eye).reshape(NGB_D, half, GB_D * GC_D)

    bb = jnp.concatenate([blockdiag_in(bb_re), blockdiag_in(bb_im)], axis=2).astype(BF16)
    cc = jnp.concatenate([blockdiag_out(c_re), -blockdiag_out(c_im)], axis=1).astype(BF16)
    return dict(w_main=w_main, bb=bb, cc=cc, a_re=a_re.reshape(NGB_D, half), a_im=a_im.reshape(NGB_D, half),
                d=d, w_glu=w_glu.astype(BF16), b_glu=b_glu, w_out=w_out.astype(BF16), ln_g=ln_g, ln_b=ln_b)


def _odd_layer(x, k_cache, v_cache, h_re0, h_im0, wp, alpha):
    bx, t, d = x.shape
    m = bx * t
    tm = _tile(m, PROJ_ROWS)
    tq = _tile(t, SB_TILE)
    tt = _tile(t, S5_ROWS)
    x2 = x.reshape(m, d)
    p5 = _proj(x2, wp["w_main"], tm, "od_proj").reshape(6, bx, t, W_HEADS)
    if k_cache is None:
        yc, kk, vv = _sb(p5, None, None, tq, True, "sb_prompt")
    else:
        past = k_cache.shape[1]
        yc, kk, vv = _sb(p5, k_cache.reshape(bx, past * H_C, DK), v_cache.reshape(bx, past * H_C, DK), tq, False,
                         "sb_sample")
    half = GB_D * P_D
    yd, hr, hi = _s5(p5, 4, 5, wp["bb"], wp["cc"], wp["a_re"], wp["a_im"], wp["d"], wp["w_glu"], wp["b_glu"],
                     h_re0.reshape(bx, NGB_D, half), h_im0.reshape(bx, NGB_D, half), tt, "s5_glu")
    xn = _outproj_ln(yc.reshape(m, -1), yd.reshape(m, -1), x2, wp["w_out"], wp["ln_g"], wp["ln_b"], alpha,
                     _tile(m, OUTPROJ_ROWS), "od_outproj_ln")
    return (xn.reshape(bx, t, d), kk.reshape(bx, t, H_C, DK), vv.reshape(bx, t, H_C, DK),
            hr.reshape(bx, G_D, P_D), hi.reshape(bx, G_D, P_D))


def kernel(x_prompt, x_sample, state_delta_S, state_delta_conv, state_mlstm_C, state_mlstm_n, state_mlstm_m, cache_sb_k, cache_sb_v, state_s5_re, state_s5_im, ev_w_in, ev_conv_w, ev_a_log, ev_dt_bias, ev_norm_a, ev_ig_bias, ev_fg_bias, ev_norm_b, ev_w_out, ev_ln_g, ev_ln_b, od_w_in, od_lam_re, od_lam_im, od_b_re, od_b_im, od_c_re, od_c_im, od_d, od_log_dt, od_w_glu, od_b_glu, od_w_out, od_ln_g, od_ln_b):
    n_even = ev_w_in.shape[0]
    n_odd = od_w_in.shape[0]
    depth = n_even + n_odd
    alpha = (2 * depth) ** 0.25
    nb, seq, _ = x_prompt.shape
    ns, dseq, _ = x_sample.shape
    xp, xs = x_prompt, x_sample
    outs = [[] for _ in range(18)]
    for layer in range(depth):
        j = layer // 2
        if layer % 2 == 0:
            wp = _pack_even_weights(ev_w_in[j], ev_conv_w[j], ev_a_log[j], ev_dt_bias[j], ev_norm_a[j], ev_ig_bias[j],
                                    ev_fg_bias[j], ev_norm_b[j], ev_w_out[j], ev_ln_g[j], ev_ln_b[j])
            zs = jnp.zeros((nb, H_A, DK, DK), F32)
            xp, s, cv, c, n, m = _even_layer(
                xp, zs, jnp.zeros((nb, CONV_W - 1, 3 * H_A * DK), F32), zs, jnp.zeros((nb, H_B, DK), F32),
                jnp.zeros((nb, H_B), F32), wp, alpha)
            for idx, val in zip((0, 2, 4, 6, 8), (s, cv, c, n, m)):
                outs[idx].append(val)
            xs, s, cv, c, n, m = _even_layer(
                xs, state_delta_S[j], state_delta_conv[j], state_mlstm_C[j], state_mlstm_n[j], state_mlstm_m[j],
                wp, alpha)
            for idx, val in zip((1, 3, 5, 7, 9), (s, cv, c, n, m)):
                outs[idx].append(val)
        else:
            wp = _pack_odd_weights(od_w_in[j], od_lam_re[j], od_lam_im[j], od_b_re[j], od_b_im[j], od_c_re[j],
                                   od_c_im[j], od_d[j], od_log_dt[j], od_w_glu[j], od_b_glu[j], od_w_out[j],
                                   od_ln_g[j], od_ln_b[j])
            zst = jnp.zeros((nb, G_D, P_D), F32)
            xp, k, v, hr, hi = _odd_layer(xp, None, None, zst, zst, wp, alpha)
            for idx, val in zip((10, 12, 14, 16), (k, v, hr, hi)):
                outs[idx].append(val)
            xs, k, v, hr, hi = _odd_layer(xs, cache_sb_k[j], cache_sb_v[j], state_s5_re[j], state_s5_im[j], wp, alpha)
            for idx, val in zip((11, 13, 15, 17), (k, v, hr, hi)):
                outs[idx].append(val)
    return (xp, xs) + tuple(jnp.stack(o) for o in outs)
```

```python
import functools

import jax
import jax.numpy as jnp
from jax import lax
from jax.experimental import pallas as pl
from jax.experimental.pallas import tpu as pltpu

F32 = jnp.float32
BF16 = jnp.bfloat16

LANES = 128
SUBLANES = 8
CHUNK = 64
CONV_W = 4
H_A = 8
H_B = 8
H_C = 8
DK = 128
GC_D = 16
G_D = 64
P_D = 64
GB_D = 8
NGB_D = G_D // GB_D
W_HEADS = 1024
LN_EPS = 1e-5
NORM_EPS = 1e-6
VMEM_LIMIT = 48 * 1024 * 1024
SB_DEAD_LOG_WEIGHT = -88.0
SB_TILE = 128
PROJ_ROWS = 1024
OUTPROJ_ROWS = 512
MIXER_ROWS = 512
S5_ROWS = 512
MLSTM_CHUNK_GROUP = 1
GDN_CHUNK_GROUP = 4


def _cparams(sem):
    return pltpu.CompilerParams(dimension_semantics=sem, vmem_limit_bytes=VMEM_LIMIT)


def _sigmoid(x):
    return 1.0 / (1.0 + jnp.exp(-x))


def _silu(x):
    return x * _sigmoid(x)


def _softplus(x):
    return jnp.maximum(x, 0.0) + jnp.log1p(jnp.exp(-jnp.abs(x)))


def _dot(a, b):
    return jnp.dot(a, b, preferred_element_type=F32)


def _dot_nt(a, b):
    return lax.dot_general(a, b, (((1,), (1,)), ((), ())), preferred_element_type=F32)


def _dot_tn(a, b):
    return lax.dot_general(a, b, (((0,), (0,)), ((), ())), preferred_element_type=F32)


def _split3(x):
    hi = x.astype(BF16)
    r1 = x - hi.astype(F32)
    mid = r1.astype(BF16)
    lo = (r1 - mid.astype(F32)).astype(BF16)
    return hi, mid, lo


def _dot_mask_lhs(mask_bf16, x):
    hi, mid, lo = _split3(x)
    return _dot(mask_bf16, hi) + _dot(mask_bf16, mid) + _dot(mask_bf16, lo)


def _dot_tn_split(x, mask_bf16):
    hi, mid, lo = _split3(x)
    return _dot_tn(hi, mask_bf16) + _dot_tn(mid, mask_bf16) + _dot_tn(lo, mask_bf16)


def _iota2(shape):
    return (lax.broadcasted_iota(jnp.int32, shape, 0), lax.broadcasted_iota(jnp.int32, shape, 1))


def _head(x, h):
    return x[:, h * DK:(h + 1) * DK]


def _proj_kernel(x_ref, w_ref, o_ref):
    o_ref[0] = _dot(x_ref[...].astype(BF16), w_ref[...])


def _proj_gates_kernel(x_ref, wa_ref, wb_ref, wg_ref, o_ref, og_ref, *, na):
    j = pl.program_id(1)
    xb = x_ref[...].astype(BF16)

    @pl.when(j < na)
    def _():
        o_ref[0] = _dot(xb, wa_ref[...])

    @pl.when(j >= na)
    def _():
        o_ref[0] = _dot(xb, wb_ref[...])

    @pl.when(j == 0)
    def _():
        og_ref[...] = _dot(xb, wg_ref[...])


def _proj(x2, w, tm, name):
    m, d = x2.shape
    wn = W_HEADS
    ng = w.shape[1] // wn
    return pl.pallas_call(
        _proj_kernel,
        grid=(m // tm, ng),
        in_specs=[pl.BlockSpec((tm, d), lambda i, j: (i, 0)),
                  pl.BlockSpec((d, wn), lambda i, j: (0, j))],
        out_specs=pl.BlockSpec((1, tm, wn), lambda i, j: (j, i, 0)),
        out_shape=jax.ShapeDtypeStruct((ng, m, wn), F32),
        compiler_params=_cparams(("parallel", "arbitrary")),
        name=name,
    )(x2, w)


def _proj_gates(x2, w_a, w_b, w_gate, tm, name):
    m, d = x2.shape
    wn = W_HEADS
    na = w_a.shape[1] // wn
    ng = na + w_b.shape[1] // wn
    return pl.pallas_call(
        functools.partial(_proj_gates_kernel, na=na),
        grid=(m // tm, ng),
        in_specs=[pl.BlockSpec((tm, d), lambda i, j: (i, 0)),
                  pl.BlockSpec((d, wn), lambda i, j: (0, jnp.minimum(j, na - 1))),
                  pl.BlockSpec((d, wn), lambda i, j: (0, jnp.maximum(j - na, 0))),
                  pl.BlockSpec((d, LANES), lambda i, j: (0, 0))],
        out_specs=[pl.BlockSpec((1, tm, wn), lambda i, j: (j, i, 0)),
                   pl.BlockSpec((tm, LANES), lambda i, j: (i, 0))],
        out_shape=[jax.ShapeDtypeStruct((ng, m, wn), F32), jax.ShapeDtypeStruct((m, LANES), F32)],
        compiler_params=_cparams(("parallel", "arbitrary")),
        name=name,
    )(x2, w_a, w_b, w_gate)


def _outproj_ln_kernel(ya_ref, yb_ref, x_ref, wa_ref, wb_ref, g_ref, b_ref, o_ref, *, alpha):
    acc = _dot(ya_ref[...].astype(BF16), wa_ref[...]) + _dot(yb_ref[...].astype(BF16), wb_ref[...])
    r = alpha * x_ref[...] + acc
    mu = jnp.mean(r, axis=-1, keepdims=True)
    c = r - mu
    var = jnp.mean(c * c, axis=-1, keepdims=True)
    o_ref[...] = c * lax.rsqrt(var + LN_EPS) * g_ref[...] + b_ref[...]


def _outproj_ln(ya, yb, x2, w_out_bf16, ln_g, ln_b, alpha, tm, name):
    m, d = x2.shape
    wa = ya.shape[1]
    wb = yb.shape[1]
    return pl.pallas_call(
        functools.partial(_outproj_ln_kernel, alpha=alpha),
        grid=(m // tm,),
        in_specs=[pl.BlockSpec((tm, wa), lambda i: (i, 0)),
                  pl.BlockSpec((tm, wb), lambda i: (i, 0)),
                  pl.BlockSpec((tm, d), lambda i: (i, 0)),
                  pl.BlockSpec((wa, d), lambda i: (0, 0)),
                  pl.BlockSpec((wb, d), lambda i: (wa // wb, 0)),
                  pl.BlockSpec((1, d), lambda i: (0, 0)),
                  pl.BlockSpec((1, d), lambda i: (0, 0))],
        out_specs=pl.BlockSpec((tm, d), lambda i: (i, 0)),
        out_shape=jax.ShapeDtypeStruct((m, d), F32),
        compiler_params=_cparams(("parallel",)),
        name=name,
    )(ya, yb, x2, w_out_bf16, w_out_bf16, ln_g.reshape(1, d), ln_b.reshape(1, d))


def _conv_carry_init(carry, c0_ref, w_ref):
    w = w_ref[...]
    xm3, xm2, xm1 = c0_ref[5:6, :], c0_ref[6:7, :], c0_ref[7:8, :]
    carry[0:1, :] = w[0:1, :] * xm1
    carry[1:2, :] = w[1:2, :] * xm1 + w[0:1, :] * xm2
    carry[2:3, :] = w[2:3, :] * xm1 + (w[1:2, :] * xm2 + w[0:1, :] * xm3)


def _conv_silu(x_ref, carry, w_ref, tb):
    x = x_ref[...]
    w = w_ref[...]
    first = lax.broadcasted_iota(jnp.int32, (SUBLANES, x.shape[1]), 0) == 0
    acc = x * w[0:1, :]
    for i in range(1, CONV_W):
        shifted_in = carry[i - 1:i, :]
        carry[i - 1:i, :] = acc[tb - 1:tb, :]
        rolled = pltpu.roll(acc, 1, axis=0)
        head = jnp.where(first, shifted_in, rolled[0:SUBLANES])
        acc = x * w[i:i + 1, :] + jnp.concatenate([head, rolled[SUBLANES:]], axis=0)
    return _silu(acc)


def _l2norm(x):
    return x * lax.rsqrt(jnp.sum(x * x, axis=-1, keepdims=True) + NORM_EPS)


def _rms_norm(x, g):
    return x * lax.rsqrt(jnp.mean(x * x, axis=-1, keepdims=True) + NORM_EPS) * g


def _unit_lower_inverses(n_mats, length):
    r, c = _iota2((length, length))
    eye = jnp.where(r == c, 1.0, 0.0)
    xs = [eye + n for n in n_mats]
    ps = list(n_mats)
    for _ in range(length.bit_length() - 2):
        pbs = [p.astype(BF16) for p in ps]
        ps = [_dot(pb, pb) for pb in pbs]
        xs = [x + _dot(x.astype(BF16), p.astype(BF16)) for x, p in zip(xs, ps)]
    return xs


def _gdn_kernel(qp_ref, kp_ref, vp_ref, z_ref, gt_ref, c0q_ref, c0k_ref, c0v_ref,
                cwq_ref, cwk_ref, cwv_ref, prm_ref, norm_ref, s0_ref,
                y_ref, sout_ref, s_scr, xq_scr, xk_scr, xv_scr, *, chunk, tb):
    t = pl.program_id(1)
    nh = H_A

    @pl.when(t == 0)
    def _():
        s_scr[...] = s0_ref[...]
        _conv_carry_init(xq_scr, c0q_ref, cwq_ref)
        _conv_carry_init(xk_scr, c0k_ref, cwk_ref)
        _conv_carry_init(xv_scr, c0v_ref, cwv_ref)

    q_all = _conv_silu(qp_ref, xq_scr, cwq_ref, tb)
    k_all = _conv_silu(kp_ref, xk_scr, cwk_ref, tb)
    v_all = _conv_silu(vp_ref, xv_scr, cwv_ref, tb)
    q_h = [_l2norm(_head(q_all, h)) * (DK ** -0.5) for h in range(nh)]
    k_h = [_l2norm(_head(k_all, h)) for h in range(nh)]

    gt = gt_ref[...]
    prm = prm_ref[...]
    beta_all = _sigmoid(gt)
    g_all = -jnp.exp(prm[0:1, :]) * _softplus(gt + prm[1:2, :])

    r, c = _iota2((chunk, chunk))
    incl = r >= c
    strict = r > c
    a_incl = jnp.where(incl, 1.0, 0.0).astype(BF16)
    a_incl_t = jnp.where(r <= c, 1.0, 0.0).astype(BF16)
    norm_g = norm_ref[...]

    def precompute(cis):
        items = []
        for ci in cis:
            sl = slice(ci * chunk, (ci + 1) * chunk)
            g_c = g_all[sl]
            gc_cols = _dot_mask_lhs(a_incl, g_c)
            gc_rows = _dot_tn_split(g_c, a_incl_t)
            beta_c = beta_all[sl]
            for h in range(nh):
                items.append(dict(ci=ci, gcol=gc_cols[:, nh + h:nh + h + 1], grow=gc_rows[nh + h:nh + h + 1, :],
                                  bc=beta_c[:, h:h + 1], qc=q_h[h][sl], kc=k_h[h][sl], vc=_head(v_all, h)[sl]))
        qkks = [_dot_nt(jnp.concatenate([it["qc"], it["kc"]], axis=0).astype(BF16), it["kc"].astype(BF16))
                for it in items]
        decays = [jnp.where(incl, jnp.exp(it["gcol"] - it["grow"]), 0.0) for it in items]
        n_mats = [jnp.where(strict, -(it["bc"] * qkk[chunk:] * decay), 0.0)
                  for it, qkk, decay in zip(items, qkks, decays)]
        t_invs = _unit_lower_inverses(n_mats, chunk)
        gams = [jnp.exp(it["gcol"]) for it in items]
        rhss = [jnp.concatenate([it["vc"] * it["bc"], it["kc"] * (it["bc"] * gam)], axis=-1).astype(BF16)
                for it, gam in zip(items, gams)]
        sols = [_dot(ti.astype(BF16), rhs) for ti, rhs in zip(t_invs, rhss)]
        out = {ci: [] for ci in cis}
        for it, qkk, decay, gam, sol in zip(items, qkks, decays, gams, sols):
            g_last = it["gcol"][chunk - 1:chunk, :]
            out[it["ci"]].append(dict(
                u=sol[:, :DK],
                wq=jnp.concatenate([sol[:, DK:], it["qc"] * gam], axis=0).astype(BF16),
                qk=(qkk[:chunk] * decay).astype(BF16),
                k_dec=(it["kc"] * jnp.exp(g_last - it["gcol"])).astype(BF16),
                s_decay=jnp.exp(g_last)))
        return out

    def advance(ci, heads, states):
        sl = slice(ci * chunk, (ci + 1) * chunk)
        wss = [_dot(hd["wq"], s.astype(BF16)) for hd, s in zip(heads, states)]
        es = [(hd["u"] - ws[:chunk]).astype(BF16) for hd, ws in zip(heads, wss)]
        new_states = [hd["s_decay"] * s + _dot_tn(hd["k_dec"], e) for hd, s, e in zip(heads, states, es)]
        outs = [ws[chunk:] + _dot(hd["qk"], e) for hd, ws, e in zip(heads, wss, es)]
        for h in range(nh):
            y_ref[sl, h * DK:(h + 1) * DK] = (_rms_norm(outs[h], norm_g)
                                              * _silu(z_ref[sl, h * DK:(h + 1) * DK])).astype(y_ref.dtype)
        return new_states

    n_chunks = tb // chunk
    groups = [list(range(g0, min(g0 + GDN_CHUNK_GROUP, n_chunks))) for g0 in range(0, n_chunks, GDN_CHUNK_GROUP)]
    states = [s_scr[h] for h in range(nh)]
    pre = precompute(groups[0])
    for gi, group in enumerate(groups):
        nxt = precompute(groups[gi + 1]) if gi + 1 < len(groups) else None
        for ci in group:
            states = advance(ci, pre[ci], states)
        pre = nxt
    for h in range(nh):
        s_scr[h] = states[h]

    @pl.when(t == pl.num_programs(1) - 1)
    def _():
        for h in range(nh):
            sout_ref[h] = states[h]


def _gdn(p5, g3, conv0p, conv_w, prm, norm_a, s0, chunk, tb, name):
    _, bx, t, w = p5.shape
    nh = H_A

    def pspec(group):
        return pl.BlockSpec((None, None, tb, w), lambda b, i, g=group: (g, b, i, 0))

    def c0spec(group):
        return pl.BlockSpec((None, SUBLANES, w), lambda b, i, g=group: (b, 0, g))

    def cwspec(group):
        return pl.BlockSpec((CONV_W, w), lambda b, i, g=group: (0, g))

    return pl.pallas_call(
        functools.partial(_gdn_kernel, chunk=chunk, tb=tb),
        grid=(bx, t // tb),
        in_specs=[pspec(0), pspec(1), pspec(2), pspec(3),
                  pl.BlockSpec((None, tb, LANES), lambda b, i: (b, i, 0)),
                  c0spec(0), c0spec(1), c0spec(2),
                  cwspec(0), cwspec(1), cwspec(2),
                  pl.BlockSpec((SUBLANES, LANES), lambda b, i: (0, 0)),
                  pl.BlockSpec((1, DK), lambda b, i: (0, 0)),
                  pl.BlockSpec((None, nh, DK, DK), lambda b, i: (b, 0, 0, 0))],
        out_specs=[pl.BlockSpec((None, tb, w), lambda b, i: (b, i, 0)),
                   pl.BlockSpec((None, nh, DK, DK), lambda b, i: (b, 0, 0, 0))],
        out_shape=[jax.ShapeDtypeStruct((bx, t, w), BF16),
                   jax.ShapeDtypeStruct((bx, nh, DK, DK), F32)],
        scratch_shapes=[pltpu.VMEM((nh, DK, DK), F32),
                        pltpu.VMEM((SUBLANES, w), F32),
                        pltpu.VMEM((SUBLANES, w), F32),
                        pltpu.VMEM((SUBLANES, w), F32)],
        compiler_params=_cparams(("parallel", "arbitrary")),
        name=name,
    )(p5, p5, p5, p5, g3, conv0p, conv0p, conv0p, conv_w, conv_w, conv_w, prm, norm_a, s0)


def _mlstm_kernel(q_ref, k_ref, v_ref, o_ref, z_ref, gt_ref, prm_ref, norm_ref, c0_ref, n0_ref, m0_ref,
                  y_ref, cout_ref, nout_ref, mout_ref, cn_scr, m_scr, *, chunk, tb):
    t = pl.program_id(1)
    nh = H_B
    heads = range(nh)
    ig_lane = 2 * H_A
    lf_lane = 2 * H_A + H_B

    @pl.when(t == 0)
    def _():
        cn_scr[:, :, :DK] = c0_ref[...]
        cn_scr[:, :, DK:] = n0_ref[...]
        m_scr[...] = m0_ref[...]

    gt = gt_ref[...]
    prm = prm_ref[...]
    ig_all = gt + prm[2:3, :]
    lf_all = -_softplus(-(gt + prm[3:4, :]))

    r, c = _iota2((chunk, chunk))
    incl = r >= c
    a_incl = jnp.where(incl, 1.0, 0.0).astype(BF16)
    a_incl_t = jnp.where(r <= c, 1.0, 0.0).astype(BF16)
    eye = jnp.where(r == c, 1.0, 0.0).astype(BF16)
    ones_w = jnp.ones((chunk, DK), F32)
    ones_b = jnp.ones((chunk, DK), BF16)
    norm_g = norm_ref[...]

    def precompute(cis):
        items = []
        for ci in cis:
            sl = slice(ci * chunk, (ci + 1) * chunk)
            lf_c = lf_all[sl]
            ig_c = ig_all[sl]
            b_cols = _dot_mask_lhs(a_incl, lf_c)
            b_rows = _dot_tn_split(lf_c, a_incl_t)
            ig_rows = _dot_tn_split(ig_c, eye)
            for h in heads:
                items.append(dict(ci=ci, sl=sl, h=h, bcol=b_cols[:, lf_lane + h:lf_lane + h + 1],
                                  brow=b_rows[lf_lane + h:lf_lane + h + 1, :],
                                  igrow=ig_rows[ig_lane + h:ig_lane + h + 1, :],
                                  igcol=ig_c[:, ig_lane + h:ig_lane + h + 1]))
        d_logs = [jnp.where(incl, it["bcol"] - it["brow"] + it["igrow"], -jnp.inf) for it in items]
        m_intras = [jnp.max(d, axis=-1, keepdims=True) for d in d_logs]
        qbs = [q_ref[it["sl"], it["h"] * DK:(it["h"] + 1) * DK].astype(BF16) for it in items]
        kcs = [k_ref[it["sl"], it["h"] * DK:(it["h"] + 1) * DK] * (DK ** -0.5) for it in items]
        v1s = [jnp.concatenate([v_ref[it["sl"], it["h"] * DK:(it["h"] + 1) * DK].astype(BF16), ones_b], axis=1)
               for it in items]
        qks = [_dot_nt(qb, kc.astype(BF16)) for qb, kc in zip(qbs, kcs)]
        pq0s = [(jnp.exp(d - mi) * qk).astype(BF16) for d, mi, qk in zip(d_logs, m_intras, qks)]
        avs = [_dot(pq0, v1) for pq0, v1 in zip(pq0s, v1s)]
        out = {ci: [] for ci in cis}
        for it, mi, qb, kc, v1, av in zip(items, m_intras, qbs, kcs, v1s, avs):
            bcol = it["bcol"]
            out[it["ci"]].append(dict(
                b_w=bcol * ones_w, mi_w=mi * ones_w,
                tail_w=(bcol[chunk - 1:chunk, :] - bcol + it["igcol"]) * ones_w,
                qb=qb, kc=kc, v1=v1, av=av))
        return out

    def advance(ci, pre, cns, ms):
        sl = slice(ci * chunk, (ci + 1) * chunk)
        qcns = [_dot(pre[h]["qb"], cns[h].astype(BF16)) for h in heads]
        bms = [pre[h]["b_w"] + ms[h] for h in heads]
        m_ts = [jnp.maximum(bms[h], pre[h]["mi_w"]) for h in heads]
        corrs = [jnp.exp(pre[h]["mi_w"] - m_ts[h]) for h in heads]
        inters = [jnp.exp(bms[h] - m_ts[h]) for h in heads]
        m_lasts = [m_t[chunk - 1:chunk, :] for m_t in m_ts]
        i_lasts = [inter[chunk - 1:chunk, :] for inter in inters]
        kps = [(pre[h]["kc"] * jnp.exp(pre[h]["tail_w"] - m_lasts[h])).astype(BF16) for h in heads]
        new_cns = [jnp.concatenate([i_lasts[h], i_lasts[h]], axis=1) * cns[h] + _dot_tn(kps[h], pre[h]["v1"])
                   for h in heads]
        for h in heads:
            num = inters[h] * qcns[h][:, :DK] + corrs[h] * pre[h]["av"][:, :DK]
            den = inters[h] * qcns[h][:, DK:] + corrs[h] * pre[h]["av"][:, DK:]
            hh = num / jnp.maximum(jnp.abs(den), jnp.exp(-m_ts[h]))
            hh = _sigmoid(o_ref[sl, h * DK:(h + 1) * DK]) * hh
            y_ref[sl, h * DK:(h + 1) * DK] = (_rms_norm(hh, norm_g[:, h * DK:(h + 1) * DK])
                                              * _silu(z_ref[sl, h * DK:(h + 1) * DK])).astype(y_ref.dtype)
        return new_cns, m_lasts

    n_chunks = tb // chunk
    cns = [cn_scr[h] for h in heads]
    ms = [m_scr[h] for h in heads]
    groups = [list(range(g0, min(g0 + MLSTM_CHUNK_GROUP, n_chunks)))
              for g0 in range(0, n_chunks, MLSTM_CHUNK_GROUP)]
    pre = precompute(groups[0])
    for gi, group in enumerate(groups):
        nxt = precompute(groups[gi + 1]) if gi + 1 < len(groups) else None
        for ci in group:
            cns, ms = advance(ci, pre[ci], cns, ms)
        pre = nxt
    for h in heads:
        cn_scr[h] = cns[h]
        m_scr[h] = ms[h]

    @pl.when(t == pl.num_programs(1) - 1)
    def _():
        for h in heads:
            cout_ref[h] = cns[h][:, :DK]
            nout_ref[h] = cns[h][:, DK:]
            mout_ref[h] = ms[h]


def _mlstm(p5, g3, prm, norm_b, c0, n0w, m0w, chunk, tb, name):
    _, bx, t, w = p5.shape
    nh = H_B

    def pspec(group):
        return pl.BlockSpec((None, None, tb, w), lambda b, i, g=group: (g, b, i, 0))

    mat = pl.BlockSpec((None, nh, DK, DK), lambda b, i: (b, 0, 0, 0))
    row = pl.BlockSpec((None, nh, 1, DK), lambda b, i: (b, 0, 0, 0))
    return pl.pallas_call(
        functools.partial(_mlstm_kernel, chunk=chunk, tb=tb),
        grid=(bx, t // tb),
        in_specs=[pspec(4), pspec(5), pspec(6), pspec(7), pspec(8),
                  pl.BlockSpec((None, tb, LANES), lambda b, i: (b, i, 0)),
                  pl.BlockSpec((SUBLANES, LANES), lambda b, i: (0, 0)),
                  pl.BlockSpec((1, w), lambda b, i: (0, 0)),
                  mat, mat, row],
        out_specs=[pl.BlockSpec((None, tb, w), lambda b, i: (b, i, 0)), mat, mat, row],
        out_shape=[jax.ShapeDtypeStruct((bx, t, w), BF16),
                   jax.ShapeDtypeStruct((bx, nh, DK, DK), F32),
                   jax.ShapeDtypeStruct((bx, nh, DK, DK), F32),
                   jax.ShapeDtypeStruct((bx, nh, 1, DK), F32)],
        scratch_shapes=[pltpu.VMEM((nh, DK, 2 * DK), F32), pltpu.VMEM((nh, 1, DK), F32)],
        compiler_params=_cparams(("parallel", "arbitrary")),
        name=name,
    )(p5, p5, p5, p5, p5, g3, prm, norm_b, c0, n0w, m0w)


def _sb_tiles(qbs, kts, vts, runs, u_neg, causal):
    tk = kts[0].shape[0]
    zs = [_dot_nt(qb, kt.astype(BF16)) for qb, kt in zip(qbs, kts)]
    sps = [jnp.maximum(z, 0.0) + jnp.log(1.0 + jnp.exp(-jnp.abs(z))) for z in zs]
    if causal is not None:
        sps = [jnp.where(causal, sp, 0.0) for sp in sps]
    his = [sp.astype(BF16) for sp in sps]
    los = [(sp - hi.astype(F32)).astype(BF16) for sp, hi in zip(sps, his)]
    exts = [_dot(jnp.concatenate([hi, lo], axis=1), u_neg) for hi, lo in zip(his, los)]
    atts = [jnp.exp(z + ext[:, :tk] + run[:, :tk]) for z, ext, run in zip(zs, exts, runs)]
    if causal is not None:
        atts = [jnp.where(causal, att, 0.0) for att in atts]
    contribs = [_dot(att.astype(BF16), vt.astype(BF16)) for att, vt in zip(atts, vts)]
    return [run + ext[:, tk:] for run, ext in zip(runs, exts)], contribs


def _sb_kernel(q_ref, z_ref, kc_ref, vc_ref, k1_ref, v1_ref, kany_ref, vany_ref,
               y_ref, ko_ref, vo_ref, acc_scr, run_scr, q_scr, kbuf, vbuf, flag, sem,
               *, tq, tk, prev_blocks_per_step, prev_blocks_fixed, prev_token_rows, kgroup, vgroup):
    nh = H_C
    b = pl.program_id(0)
    i = pl.program_id(1)
    r, c = _iota2((tq, tq))
    causal = c < r
    def suffix_sum_matrix(n):
        rr, cc = _iota2((2 * n, n + LANES))
        rr = jnp.where(rr >= n, rr - n, rr)
        return jnp.where((rr >= cc) | (cc >= n), -1.0, 0.0).astype(BF16)

    u_diag = suffix_sum_matrix(tq)
    u_prev = suffix_sum_matrix(tk)
    q_scr[...] = (q_ref[...] * (DK ** -0.5)).astype(BF16)

    def q_head(h):
        return q_scr[:, h * DK:(h + 1) * DK]

    def set_flag():
        best = run_scr[0]
        for h in range(1, nh):
            best = jnp.maximum(best, run_scr[h])
        flag[0] = (jnp.max(best) > SB_DEAD_LOG_WEIGHT).astype(jnp.int32)

    heads = range(nh)
    kts = [kc_ref[:, h * DK:(h + 1) * DK] for h in heads]
    vts = [vc_ref[:, h * DK:(h + 1) * DK] for h in heads]
    for h in heads:
        ko_ref[pl.ds(h, tq, stride=nh), :] = kts[h]
        vo_ref[pl.ds(h, tq, stride=nh), :] = vts[h]
    runs, contribs = _sb_tiles([q_head(h) for h in heads], kts, vts, [jnp.zeros((tq, LANES), F32)] * nh, u_diag,
                               causal)
    for h in heads:
        run_scr[h] = runs[h]
        acc_scr[:, h * DK:(h + 1) * DK] = contribs[h]
    flag[0] = 1

    def prev_tile(ref, h):
        if prev_token_rows:
            return ref[pl.ds(h, tk, stride=nh), :]
        return ref[:, h * DK:(h + 1) * DK]

    def stage(k_ref, v_ref):
        runs, contribs = _sb_tiles([q_head(h) for h in heads], [prev_tile(k_ref, h) for h in heads],
                                   [prev_tile(v_ref, h) for h in heads], [run_scr[h] for h in heads], u_prev, None)
        for h in heads:
            run_scr[h] = runs[h]
            acc_scr[:, h * DK:(h + 1) * DK] += contribs[h]
        set_flag()

    nprev = i * prev_blocks_per_step + prev_blocks_fixed

    @pl.when(jnp.logical_and(nprev >= 1, flag[0] > 0))
    def _():
        stage(k1_ref, v1_ref)

    def far_copy(any_ref, group, buf, j, slot):
        start = (nprev - j) * tk
        if prev_token_rows:
            src = any_ref.at[b, pl.ds(pl.multiple_of(start * nh, tk * nh), tk * nh), :]
        else:
            src = any_ref.at[group, b, pl.ds(pl.multiple_of(start, tk), tk), :]
        return pltpu.make_async_copy(src, buf, sem.at[slot])

    def cond(carry):
        j, go = carry
        return jnp.logical_and(j <= nprev, go > 0)

    def body(carry):
        j, _ = carry
        kcopy = far_copy(kany_ref, kgroup, kbuf, j, 0)
        vcopy = far_copy(vany_ref, vgroup, vbuf, j, 1)
        kcopy.start()
        vcopy.start()
        kcopy.wait()
        vcopy.wait()
        stage(kbuf, vbuf)
        return j + 1, flag[0]

    lax.while_loop(cond, body, (jnp.int32(2), flag[0]))
    y_ref[...] = (acc_scr[...] * _silu(z_ref[...])).astype(y_ref.dtype)


def _sb(p5, kprev, vprev, tq, causal_prev, name):
    _, bx, t, w = p5.shape
    nh = H_C
    tk = SB_TILE if not causal_prev else tq

    def blk(group):
        return pl.BlockSpec((None, None, tq, w), lambda b, i, g=group: (g, b, i, 0))

    if causal_prev:
        def near(group):
            return pl.BlockSpec((None, None, tk, w), lambda b, i, g=group: (g, b, jnp.maximum(i - 1, 0), 0))
        kprev, vprev = p5, p5
        near_specs = [near(1), near(2)]
        buf_shape = (tk, w)
        per_step, fixed = tq // tk, 0
    else:
        nblk = kprev.shape[1] // (tk * nh)
        last = pl.BlockSpec((None, tk * nh, DK), lambda b, i: (b, nblk - 1, 0))
        near_specs = [last, last]
        buf_shape = (tk * nh, DK)
        per_step, fixed = 0, nblk
    kern = functools.partial(_sb_kernel, tq=tq, tk=tk, prev_blocks_per_step=per_step, prev_blocks_fixed=fixed,
                             prev_token_rows=not causal_prev, kgroup=1, vgroup=2)
    rows_out = pl.BlockSpec((None, tq * nh, DK), lambda b, i: (b, i, 0))
    return pl.pallas_call(
        kern,
        grid=(bx, t // tq),
        in_specs=[blk(0), blk(3), blk(1), blk(2)] + near_specs
                 + [pl.BlockSpec(memory_space=pl.ANY), pl.BlockSpec(memory_space=pl.ANY)],
        out_specs=[pl.BlockSpec((None, tq, w), lambda b, i: (b, i, 0)), rows_out, rows_out],
        out_shape=[jax.ShapeDtypeStruct((bx, t, w), BF16),
                   jax.ShapeDtypeStruct((bx, t * nh, DK), F32),
                   jax.ShapeDtypeStruct((bx, t * nh, DK), F32)],
        scratch_shapes=[pltpu.VMEM((tq, w), F32), pltpu.VMEM((nh, tq, LANES), F32), pltpu.VMEM((tq, w), BF16),
                        pltpu.VMEM(buf_shape, F32), pltpu.VMEM(buf_shape, F32),
                        pltpu.SMEM((1,), jnp.int32), pltpu.SemaphoreType.DMA((2,))],
        compiler_params=_cparams(("parallel", "arbitrary")),
        name=name,
    )(p5, p5, p5, p5, kprev, vprev, kprev, vprev)


def _s5_kernel(u_ref, z_ref, bb_ref, cc_ref, ar_ref, ai_ref, d_ref, wg_ref, bg_ref, h0r_ref, h0i_ref,
               y_ref, hr_out, hi_out, hs, ys, hr_scr, hi_scr, *, tt):
    t = pl.program_id(1)
    half = GB_D * P_D
    nl = half // LANES

    @pl.when(t == 0)
    def _():
        hr_scr[...] = h0r_ref[...]
        hi_scr[...] = h0i_ref[...]

    for gb in range(NGB_D):
        rows = pl.ds(gb, tt, stride=NGB_D)
        bu = _dot(u_ref[:, gb * LANES:(gb + 1) * LANES].astype(BF16), bb_ref[gb])
        for j in range(2 * nl):
            hs[j, rows, :] = bu[:, j * LANES:(j + 1) * LANES]

    ar = [ar_ref[:, j * LANES:(j + 1) * LANES] for j in range(nl)]
    ai = [ai_ref[:, j * LANES:(j + 1) * LANES] for j in range(nl)]

    def step(s, carry):
        rows = pl.ds(pl.multiple_of(s * NGB_D, NGB_D), NGB_D)
        new = [None] * (2 * nl)
        for j in range(nl):
            hr, hi = carry[j], carry[nl + j]
            new[j] = ar[j] * hr - ai[j] * hi + hs[j, rows, :]
            new[nl + j] = ar[j] * hi + ai[j] * hr + hs[nl + j, rows, :]
        for j in range(2 * nl):
            hs[j, rows, :] = new[j]
        return tuple(new)

    init = tuple(hr_scr[:, j * LANES:(j + 1) * LANES] for j in range(nl)) + tuple(
        hi_scr[:, j * LANES:(j + 1) * LANES] for j in range(nl))
    fin = lax.fori_loop(0, tt, step, init, unroll=8)
    hr = jnp.concatenate(fin[:nl], axis=1)
    hi = jnp.concatenate(fin[nl:], axis=1)
    hr_scr[...] = hr
    hi_scr[...] = hi

    for gb in range(NGB_D):
        rows = pl.ds(gb, tt, stride=NGB_D)
        hcat = jnp.concatenate([hs[j, rows, :] for j in range(2 * nl)], axis=1)
        lanes = slice(gb * LANES, (gb + 1) * LANES)
        ys[:, lanes] = _dot(hcat.astype(BF16), cc_ref[gb]) + d_ref[:, lanes] * u_ref[:, lanes]

    y = ys[...]
    yg = 0.5 * y * (1.0 + jnp.tanh(0.7978845608028654 * (y + 0.044715 * (y * y * y))))
    gate = _sigmoid(_dot(yg.astype(BF16), wg_ref[...]) + bg_ref[...])
    y_ref[...] = (yg * gate * _silu(z_ref[...])).astype(y_ref.dtype)

    @pl.when(t == pl.num_programs(1) - 1)
    def _():
        hr_out[...] = hr
        hi_out[...] = hi


def _s5(p5, ugroup, zgroup, bb, cc, a_re, a_im, d, w_glu_bf16, b_glu, h0r, h0i, tt, name):
    _, bx, t, w = p5.shape
    half = GB_D * P_D
    st = pl.BlockSpec((None, NGB_D, half), lambda b, i: (b, 0, 0))
    prm = pl.BlockSpec((NGB_D, half), lambda b, i: (0, 0))
    row = pl.BlockSpec((1, w), lambda b, i: (0, 0))
    return pl.pallas_call(
        functools.partial(_s5_kernel, tt=tt),
        grid=(bx, t // tt),
        in_specs=[pl.BlockSpec((None, None, tt, w), lambda b, i: (ugroup, b, i, 0)),
                  pl.BlockSpec((None, None, tt, w), lambda b, i: (zgroup, b, i, 0)),
                  pl.BlockSpec((NGB_D, LANES, 2 * half), lambda b, i: (0, 0, 0)),
                  pl.BlockSpec((NGB_D, 2 * half, LANES), lambda b, i: (0, 0, 0)),
                  prm, prm, row,
                  pl.BlockSpec((w, w), lambda b, i: (0, 0)),
                  row, st, st],
        out_specs=[pl.BlockSpec((None, tt, w), lambda b, i: (b, i, 0)), st, st],
        out_shape=[jax.ShapeDtypeStruct((bx, t, w), BF16),
                   jax.ShapeDtypeStruct((bx, NGB_D, half), F32),
                   jax.ShapeDtypeStruct((bx, NGB_D, half), F32)],
        scratch_shapes=[pltpu.VMEM((2 * half // LANES, tt * NGB_D, LANES), F32),
                        pltpu.VMEM((tt, w), F32),
                        pltpu.VMEM((NGB_D, half), F32), pltpu.VMEM((NGB_D, half), F32)],
        compiler_params=_cparams(("parallel", "arbitrary")),
        name=name,
    )(p5, p5, bb, cc, a_re, a_im, d.reshape(1, w), w_glu_bf16, b_glu.reshape(1, w), h0r, h0i)


def _pack_even_weights(w_in, conv_w, a_log, dt_bias, norm_a, ig_bias, fg_bias, norm_b, w_out, ln_g, ln_b):
    d = w_in.shape[0]
    wa = H_A * DK
    wb = H_B * DK
    w_all = w_in.astype(BF16)
    n_a = 4 * wa
    w_a = w_all[:, :n_a]
    o = n_a
    gate_a = w_all[:, o:o + 2 * H_A]; o += 2 * H_A
    w_b = w_all[:, o:o + 5 * wb]; o += 5 * wb
    gate_b = w_all[:, o:o + 2 * H_B]
    n_gate = 2 * H_A + 2 * H_B
    w_gate = jnp.concatenate([gate_a, gate_b, jnp.zeros((d, LANES - n_gate), BF16)], axis=1)
    prm = jnp.zeros((SUBLANES, LANES), F32)
    prm = prm.at[0, H_A:2 * H_A].set(a_log)
    prm = prm.at[1, H_A:2 * H_A].set(dt_bias)
    prm = prm.at[2, 2 * H_A:2 * H_A + H_B].set(ig_bias)
    prm = prm.at[3, 2 * H_A + H_B:2 * H_A + 2 * H_B].set(fg_bias)
    return dict(w_a=w_a, w_b=w_b, w_gate=w_gate, conv_w=conv_w, prm=prm,
                norm_a=norm_a.reshape(1, DK),
                norm_b=norm_b.reshape(1, wb), w_out=w_out.astype(BF16), ln_g=ln_g, ln_b=ln_b)


def _tile(n, preferred):
    tile = min(n, preferred)
    assert n % tile == 0, (n, preferred)
    return tile


def _even_layer(x, s0, conv0, c0, n0, m0, wp, alpha):
    bx, t, d = x.shape
    m = bx * t
    chunk = min(CHUNK, t)
    tm = _tile(m, PROJ_ROWS)
    tb = _tile(t, MIXER_ROWS)
    x2 = x.reshape(m, d)
    p3, g2 = _proj_gates(x2, wp["w_a"], wp["w_b"], wp["w_gate"], tm, "ev_proj")
    p5 = p3.reshape(9, bx, t, W_HEADS)
    g3 = g2.reshape(bx, t, LANES)
    conv0p = jnp.concatenate([jnp.zeros((bx, SUBLANES - (CONV_W - 1), conv0.shape[2]), F32), conv0], axis=1)
    ya, s_new = _gdn(p5, g3, conv0p, wp["conv_w"], wp["prm"], wp["norm_a"], s0, chunk, tb, "gdn")
    m0w = jnp.broadcast_to(m0[:, :, None, None], (bx, H_B, 1, DK))
    n0w = jnp.broadcast_to(n0[:, :, :, None], (bx, H_B, DK, DK))
    yb, c_new, n_new, m_new = _mlstm(p5, g3, wp["prm"], wp["norm_b"], c0, n0w, m0w, chunk, tb, "mlstm")
    xn = _outproj_ln(ya.reshape(m, -1), yb.reshape(m, -1), x2, wp["w_out"], wp["ln_g"], wp["ln_b"], alpha,
                     _tile(m, OUTPROJ_ROWS), "ev_outproj_ln")
    conv_new = jnp.concatenate([p5[0, :, t - (CONV_W - 1):], p5[1, :, t - (CONV_W - 1):], p5[2, :, t - (CONV_W - 1):]],
                               axis=-1)
    return xn.reshape(bx, t, d), s_new, conv_new, c_new, n_new[:, :, :, 0], m_new[:, :, 0, 0]


def _pack_odd_weights(w_in, lam_re, lam_im, b_re, b_im, c_re, c_im, d, log_dt, w_glu, b_glu, w_out, ln_g, ln_b):
    dm = w_in.shape[0]
    w_main = w_in.astype(BF16)
    dt = jnp.exp(log_dt)[:, None]
    mag = jnp.exp(lam_re * dt)
    a_re, a_im = mag * jnp.cos(lam_im * dt), mag * jnp.sin(lam_im * dt)
    den = lam_re * lam_re + lam_im * lam_im
    f_re = ((a_re - 1.0) * lam_re + a_im * lam_im) / den
    f_im = (a_im * lam_re - (a_re - 1.0) * lam_im) / den
    bb_re = f_re[..., None] * b_re - f_im[..., None] * b_im
    bb_im = f_re[..., None] * b_im + f_im[..., None] * b_re
    half = GB_D * P_D
    eye = jnp.eye(GB_D, dtype=F32)

    def blockdiag_in(bb):
        bbg = bb.reshape(NGB_D, GB_D, P_D, GC_D)
        return jnp.einsum("bgpc,gh->bgchp", bbg, eye).reshape(NGB_D, GB_D * GC_D, half)

    def blockdiag_out(cm):
        cg = cm.reshape(NGB_D, GB_D, GC_D, P_D)
        return jnp.einsum("bgcp,gh->bgphc", cg, eye).reshape(NGB_D, half, GB_D * GC_D)

    bb = jnp.concatenate([blockdiag_in(bb_re), blockdiag_in(bb_im)], axis=2).astype(BF16)
    cc = jnp.concatenate([blockdiag_out(c_re), -blockdiag_out(c_im)], axis=1).astype(BF16)
    return dict(w_main=w_main, bb=bb, cc=cc, a_re=a_re.reshape(NGB_D, half), a_im=a_im.reshape(NGB_D, half),
                d=d, w_glu=w_glu.astype(BF16), b_glu=b_glu, w_out=w_out.astype(BF16), ln_g=ln_g, ln_b=ln_b)


def _odd_layer(x, k_cache, v_cache, h_re0, h_im0, wp, alpha):
    bx, t, d = x.shape
    m = bx * t
    tm = _tile(m, PROJ_ROWS)
    tq = _tile(t, SB_TILE)
    tt = _tile(t, S5_ROWS)
    x2 = x.reshape(m, d)
    p5 = _proj(x2, wp["w_main"], tm, "od_proj").reshape(6, bx, t, W_HEADS)
    if k_cache is None:
        yc, kk, vv = _sb(p5, None, None, tq, True, "sb_prompt")
    else:
        past = k_cache.shape[1]
        yc, kk, vv = _sb(p5, k_cache.reshape(bx, past * H_C, DK), v_cache.reshape(bx, past * H_C, DK), tq, False,
                         "sb_sample")
    half = GB_D * P_D
    yd, hr, hi = _s5(p5, 4, 5, wp["bb"], wp["cc"], wp["a_re"], wp["a_im"], wp["d"], wp["w_glu"], wp["b_glu"],
                     h_re0.reshape(bx, NGB_D, half), h_im0.reshape(bx, NGB_D, half), tt, "s5_glu")
    xn = _outproj_ln(yc.reshape(m, -1), yd.reshape(m, -1), x2, wp["w_out"], wp["ln_g"], wp["ln_b"], alpha,
                     _tile(m, OUTPROJ_ROWS), "od_outproj_ln")
    return (xn.reshape(bx, t, d), kk.reshape(bx, t, H_C, DK), vv.reshape(bx, t, H_C, DK),
            hr.reshape(bx, G_D, P_D), hi.reshape(bx, G_D, P_D))


def kernel(x_prompt, x_sample, state_delta_S, state_delta_conv, state_mlstm_C, state_mlstm_n, state_mlstm_m, cache_sb_k, cache_sb_v, state_s5_re, state_s5_im, ev_w_in, ev_conv_w, ev_a_log, ev_dt_bias, ev_norm_a, ev_ig_bias, ev_fg_bias, ev_norm_b, ev_w_out, ev_ln_g, ev_ln_b, od_w_in, od_lam_re, od_lam_im, od_b_re, od_b_im, od_c_re, od_c_im, od_d, od_log_dt, od_w_glu, od_b_glu, od_w_out, od_ln_g, od_ln_b):
    n_even = ev_w_in.shape[0]
    n_odd = od_w_in.shape[0]
    depth = n_even + n_odd
    alpha = (2 * depth) ** 0.25
    nb, seq, _ = x_prompt.shape
    ns, dseq, _ = x_sample.shape
    xp, xs = x_prompt, x_sample
    outs = [[] for _ in range(18)]
    for layer in range(depth):
        j = layer // 2
        if layer % 2 == 0:
            wp = _pack_even_weights(ev_w_in[j], ev_conv_w[j], ev_a_log[j], ev_dt_bias[j], ev_norm_a[j], ev_ig_bias[j],
                                    ev_fg_bias[j], ev_norm_b[j], ev_w_out[j], ev_ln_g[j], ev_ln_b[j])
            zs = jnp.zeros((nb, H_A, DK, DK), F32)
            xp, s, cv, c, n, m = _even_layer(
                xp, zs, jnp.zeros((nb, CONV_W - 1, 3 * H_A * DK), F32), zs, jnp.zeros((nb, H_B, DK), F32),
                jnp.zeros((nb, H_B), F32), wp, alpha)
            for idx, val in zip((0, 2, 4, 6, 8), (s, cv, c, n, m)):
                outs[idx].append(val)
            xs, s, cv, c, n, m = _even_layer(
                xs, state_delta_S[j], state_delta_conv[j], state_mlstm_C[j], state_mlstm_n[j], state_mlstm_m[j],
                wp, alpha)
            for idx, val in zip((1, 3, 5, 7, 9), (s, cv, c, n, m)):
                outs[idx].append(val)
        else:
            wp = _pack_odd_weights(od_w_in[j], od_lam_re[j], od_lam_im[j], od_b_re[j], od_b_im[j], od_c_re[j],
                                   od_c_im[j], od_d[j], od_log_dt[j], od_w_glu[j], od_b_glu[j], od_w_out[j],
                                   od_ln_g[j], od_ln_b[j])
            zst = jnp.zeros((nb, G_D, P_D), F32)
            xp, k, v, hr, hi = _odd_layer(xp, None, None, zst, zst, wp, alpha)
            for idx, val in zip((10, 12, 14, 16), (k, v, hr, hi)):
                outs[idx].append(val)
            xs, k, v, hr, hi = _odd_layer(xs, cache_sb_k[j], cache_sb_v[j], state_s5_re[j], state_s5_im[j], wp, alpha)
            for idx, val in zip((11, 13, 15, 17), (k, v, hr, hi)):
                outs[idx].append(val)
    return (xp, xs) + tuple(jnp.stack(o) for o in outs)
```

```python
import functools

import jax
import jax.numpy as jnp
from jax import lax
from jax.experimental import pallas as pl
from jax.experimental.pallas import tpu as pltpu

F32 = jnp.float32
BF16 = jnp.bfloat16

LANES = 128
SUBLANES = 8
CHUNK = 64
CONV_W = 4
H_A = 8
H_B = 8
H_C = 8
DK = 128
GC_D = 16
G_D = 64
P_D = 64
GB_D = 8
NGB_D = G_D // GB_D
W_HEADS = 1024
LN_EPS = 1e-5
NORM_EPS = 1e-6
VMEM_LIMIT = 48 * 1024 * 1024
SB_DEAD_LOG_WEIGHT = -88.0
SB_TILE = 128
PROJ_ROWS = 1024
OUTPROJ_ROWS = 512
MIXER_ROWS = 512
S5_ROWS = 512
MLSTM_CHUNK_GROUP = 1
GDN_CHUNK_GROUP = 4


def _cparams(sem):
    return pltpu.CompilerParams(dimension_semantics=sem, vmem_limit_bytes=VMEM_LIMIT)


def _sigmoid(x):
    return 1.0 / (1.0 + jnp.exp(-x))


def _silu(x):
    return x * _sigmoid(x)


def _softplus(x):
    return jnp.maximum(x, 0.0) + jnp.log1p(jnp.exp(-jnp.abs(x)))


def _dot(a, b):
    return jnp.dot(a, b, preferred_element_type=F32)


def _dot_nt(a, b):
    return lax.dot_general(a, b, (((1,), (1,)), ((), ())), preferred_element_type=F32)


def _dot_tn(a, b):
    return lax.dot_general(a, b, (((0,), (0,)), ((), ())), preferred_element_type=F32)


def _split3(x):
    hi = x.astype(BF16)
    r1 = x - hi.astype(F32)
    mid = r1.astype(BF16)
    lo = (r1 - mid.astype(F32)).astype(BF16)
    return hi, mid, lo


def _dot_mask_lhs(mask_bf16, x):
    hi, mid, lo = _split3(x)
    return _dot(mask_bf16, hi) + _dot(mask_bf16, mid) + _dot(mask_bf16, lo)


def _dot_tn_split(x, mask_bf16):
    hi, mid, lo = _split3(x)
    return _dot_tn(hi, mask_bf16) + _dot_tn(mid, mask_bf16) + _dot_tn(lo, mask_bf16)


def _iota2(shape):
    return (lax.broadcasted_iota(jnp.int32, shape, 0), lax.broadcasted_iota(jnp.int32, shape, 1))


def _head(x, h):
    return x[:, h * DK:(h + 1) * DK]


def _proj_kernel(x_ref, w_ref, o_ref):
    o_ref[0] = _dot(x_ref[...].astype(BF16), w_ref[...])


def _proj_gates_kernel(x_ref, wa_ref, wb_ref, wg_ref, o_ref, og_ref, *, na):
    j = pl.program_id(1)

    @pl.when(j < na)
    def _():
        o_ref[0] = _dot(x_ref[...].astype(BF16), wa_ref[...])

    @pl.when(j >= na)
    def _():
        o_ref[0] = _dot(x_ref[...].astype(BF16), wb_ref[...])

    @pl.when(j == 0)
    def _():
        og_ref[...] = _dot(x_ref[...].astype(BF16), wg_ref[...])


def _proj(x2, w, tm, name):
    m, d = x2.shape
    wn = W_HEADS
    ng = w.shape[1] // wn
    return pl.pallas_call(
        _proj_kernel,
        grid=(m // tm, ng),
        in_specs=[pl.BlockSpec((tm, d), lambda i, j: (i, 0)),
                  pl.BlockSpec((d, wn), lambda i, j: (0, j))],
        out_specs=pl.BlockSpec((1, tm, wn), lambda i, j: (j, i, 0)),
        out_shape=jax.ShapeDtypeStruct((ng, m, wn), F32),
        compiler_params=_cparams(("parallel", "arbitrary")),
        name=name,
    )(x2, w)


def _proj_gates(x2, w_a, w_b, w_gate, tm, name):
    m, d = x2.shape
    wn = W_HEADS
    na = w_a.shape[1] // wn
    ng = na + w_b.shape[1] // wn
    return pl.pallas_call(
        functools.partial(_proj_gates_kernel, na=na),
        grid=(m // tm, ng),
        in_specs=[pl.BlockSpec((tm, d), lambda i, j: (i, 0)),
                  pl.BlockSpec((d, wn), lambda i, j: (0, jnp.minimum(j, na - 1))),
                  pl.BlockSpec((d, wn), lambda i, j: (0, jnp.maximum(j - na, 0))),
                  pl.BlockSpec((d, LANES), lambda i, j: (0, 0))],
        out_specs=[pl.BlockSpec((1, tm, wn), lambda i, j: (j, i, 0)),
                   pl.BlockSpec((tm, LANES), lambda i, j: (i, 0))],
        out_shape=[jax.ShapeDtypeStruct((ng, m, wn), F32), jax.ShapeDtypeStruct((m, LANES), F32)],
        compiler_params=_cparams(("parallel", "arbitrary")),
        name=name,
    )(x2, w_a, w_b, w_gate)


def _outproj_ln_kernel(ya_ref, yb_ref, x_ref, wa_ref, wb_ref, g_ref, b_ref, o_ref, *, alpha):
    acc = _dot(ya_ref[...].astype(BF16), wa_ref[...]) + _dot(yb_ref[...].astype(BF16), wb_ref[...])
    r = alpha * x_ref[...] + acc
    mu = jnp.mean(r, axis=-1, keepdims=True)
    c = r - mu
    var = jnp.mean(c * c, axis=-1, keepdims=True)
    o_ref[...] = c * lax.rsqrt(var + LN_EPS) * g_ref[...] + b_ref[...]


def _outproj_ln(ya, yb, x2, w_out_bf16, ln_g, ln_b, alpha, tm, name):
    m, d = x2.shape
    wa = ya.shape[1]
    wb = yb.shape[1]
    return pl.pallas_call(
        functools.partial(_outproj_ln_kernel, alpha=alpha),
        grid=(m // tm,),
        in_specs=[pl.BlockSpec((tm, wa), lambda i: (i, 0)),
                  pl.BlockSpec((tm, wb), lambda i: (i, 0)),
                  pl.BlockSpec((tm, d), lambda i: (i, 0)),
                  pl.BlockSpec((wa, d), lambda i: (0, 0)),
                  pl.BlockSpec((wb, d), lambda i: (wa // wb, 0)),
                  pl.BlockSpec((1, d), lambda i: (0, 0)),
                  pl.BlockSpec((1, d), lambda i: (0, 0))],
        out_specs=pl.BlockSpec((tm, d), lambda i: (i, 0)),
        out_shape=jax.ShapeDtypeStruct((m, d), F32),
        compiler_params=_cparams(("parallel",)),
        name=name,
    )(ya, yb, x2, w_out_bf16, w_out_bf16, ln_g.reshape(1, d), ln_b.reshape(1, d))


def _conv_carry_init(carry, c0_ref, w_ref):
    w = w_ref[...]
    xm3, xm2, xm1 = c0_ref[5:6, :], c0_ref[6:7, :], c0_ref[7:8, :]
    carry[0:1, :] = w[0:1, :] * xm1
    carry[1:2, :] = w[1:2, :] * xm1 + w[0:1, :] * xm2
    carry[2:3, :] = w[2:3, :] * xm1 + (w[1:2, :] * xm2 + w[0:1, :] * xm3)


def _conv_silu(x_ref, carry, w_ref, tb):
    x = x_ref[...]
    w = w_ref[...]
    first = lax.broadcasted_iota(jnp.int32, (SUBLANES, x.shape[1]), 0) == 0
    acc = x * w[0:1, :]
    for i in range(1, CONV_W):
        shifted_in = carry[i - 1:i, :]
        carry[i - 1:i, :] = acc[tb - 1:tb, :]
        rolled = pltpu.roll(acc, 1, axis=0)
        head = jnp.where(first, shifted_in, rolled[0:SUBLANES])
        acc = x * w[i:i + 1, :] + jnp.concatenate([head, rolled[SUBLANES:]], axis=0)
    return _silu(acc)


def _l2norm(x):
    return x * lax.rsqrt(jnp.sum(x * x, axis=-1, keepdims=True) + NORM_EPS)


def _rms_norm(x, g):
    return x * lax.rsqrt(jnp.mean(x * x, axis=-1, keepdims=True) + NORM_EPS) * g


def _unit_lower_inverses(n_mats, length):
    r, c = _iota2((length, length))
    eye = jnp.where(r == c, 1.0, 0.0)
    xs = [eye + n for n in n_mats]
    ps = list(n_mats)
    for _ in range(length.bit_length() - 2):
        pbs = [p.astype(BF16) for p in ps]
        ps = [_dot(pb, pb) for pb in pbs]
        xs = [x + _dot(x.astype(BF16), p.astype(BF16)) for x, p in zip(xs, ps)]
    return xs


def _gdn_kernel(qp_ref, kp_ref, vp_ref, z_ref, gt_ref, c0q_ref, c0k_ref, c0v_ref,
                cwq_ref, cwk_ref, cwv_ref, prm_ref, norm_ref, s0_ref,
                y_ref, sout_ref, s_scr, xq_scr, xk_scr, xv_scr, *, chunk, tb):
    t = pl.program_id(1)
    nh = H_A

    @pl.when(t == 0)
    def _():
        s_scr[...] = s0_ref[...]
        _conv_carry_init(xq_scr, c0q_ref, cwq_ref)
        _conv_carry_init(xk_scr, c0k_ref, cwk_ref)
        _conv_carry_init(xv_scr, c0v_ref, cwv_ref)

    q_all = _conv_silu(qp_ref, xq_scr, cwq_ref, tb)
    k_all = _conv_silu(kp_ref, xk_scr, cwk_ref, tb)
    v_all = _conv_silu(vp_ref, xv_scr, cwv_ref, tb)
    q_h = [_l2norm(_head(q_all, h)) * (DK ** -0.5) for h in range(nh)]
    k_h = [_l2norm(_head(k_all, h)) for h in range(nh)]

    gt = gt_ref[...]
    prm = prm_ref[...]
    beta_all = _sigmoid(gt)
    g_all = -jnp.exp(prm[0:1, :]) * _softplus(gt + prm[1:2, :])

    r, c = _iota2((chunk, chunk))
    incl = r >= c
    strict = r > c
    a_incl = jnp.where(incl, 1.0, 0.0).astype(BF16)
    a_incl_t = jnp.where(r <= c, 1.0, 0.0).astype(BF16)
    norm_g = norm_ref[...]

    def precompute(cis):
        items = []
        for ci in cis:
            sl = slice(ci * chunk, (ci + 1) * chunk)
            g_c = g_all[sl]
            gc_cols = _dot_mask_lhs(a_incl, g_c)
            gc_rows = _dot_tn_split(g_c, a_incl_t)
            beta_c = beta_all[sl]
            for h in range(nh):
                items.append(dict(ci=ci, gcol=gc_cols[:, nh + h:nh + h + 1], grow=gc_rows[nh + h:nh + h + 1, :],
                                  bc=beta_c[:, h:h + 1], qc=q_h[h][sl], kc=k_h[h][sl], vc=_head(v_all, h)[sl]))
        qkks = [_dot_nt(jnp.concatenate([it["qc"], it["kc"]], axis=0).astype(BF16), it["kc"].astype(BF16))
                for it in items]
        decays = [jnp.where(incl, jnp.exp(it["gcol"] - it["grow"]), 0.0) for it in items]
        n_mats = [jnp.where(strict, -(it["bc"] * qkk[chunk:] * decay), 0.0)
                  for it, qkk, decay in zip(items, qkks, decays)]
        t_invs = _unit_lower_inverses(n_mats, chunk)
        gams = [jnp.exp(it["gcol"]) for it in items]
        rhss = [jnp.concatenate([it["vc"] * it["bc"], it["kc"] * (it["bc"] * gam)], axis=-1).astype(BF16)
                for it, gam in zip(items, gams)]
        sols = [_dot(ti.astype(BF16), rhs) for ti, rhs in zip(t_invs, rhss)]
        out = {ci: [] for ci in cis}
        for it, qkk, decay, gam, sol in zip(items, qkks, decays, gams, sols):
            g_last = it["gcol"][chunk - 1:chunk, :]
            out[it["ci"]].append(dict(
                u=sol[:, :DK],
                wq=jnp.concatenate([sol[:, DK:], it["qc"] * gam], axis=0).astype(BF16),
                qk=(qkk[:chunk] * decay).astype(BF16),
                k_dec=(it["kc"] * jnp.exp(g_last - it["gcol"])).astype(BF16),
                s_decay=jnp.exp(g_last)))
        return out

    def advance(ci, heads, states):
        sl = slice(ci * chunk, (ci + 1) * chunk)
        wss = [_dot(hd["wq"], s.astype(BF16)) for hd, s in zip(heads, states)]
        es = [(hd["u"] - ws[:chunk]).astype(BF16) for hd, ws in zip(heads, wss)]
        new_states = [hd["s_decay"] * s + _dot_tn(hd["k_dec"], e) for hd, s, e in zip(heads, states, es)]
        outs = [ws[chunk:] + _dot(hd["qk"], e) for hd, ws, e in zip(heads, wss, es)]
        for h in range(nh):
            y_ref[sl, h * DK:(h + 1) * DK] = (_rms_norm(outs[h], norm_g)
                                              * _silu(z_ref[sl, h * DK:(h + 1) * DK])).astype(y_ref.dtype)
        return new_states

    n_chunks = tb // chunk
    groups = [list(range(g0, min(g0 + GDN_CHUNK_GROUP, n_chunks))) for g0 in range(0, n_chunks, GDN_CHUNK_GROUP)]
    states = [s_scr[h] for h in range(nh)]
    pre = precompute(groups[0])
    for gi, group in enumerate(groups):
        nxt = precompute(groups[gi + 1]) if gi + 1 < len(groups) else None
        for ci in group:
            states = advance(ci, pre[ci], states)
        pre = nxt
    for h in range(nh):
        s_scr[h] = states[h]

    @pl.when(t == pl.num_programs(1) - 1)
    def _():
        for h in range(nh):
            sout_ref[h] = states[h]


def _gdn(p5, g3, conv0p, conv_w, prm, norm_a, s0, chunk, tb, name):
    _, bx, t, w = p5.shape
    nh = H_A

    def pspec(group):
        return pl.BlockSpec((None, None, tb, w), lambda b, i, g=group: (g, b, i, 0))

    def c0spec(group):
        return pl.BlockSpec((None, SUBLANES, w), lambda b, i, g=group: (b, 0, g))

    def cwspec(group):
        return pl.BlockSpec((CONV_W, w), lambda b, i, g=group: (0, g))

    return pl.pallas_call(
        functools.partial(_gdn_kernel, chunk=chunk, tb=tb),
        grid=(bx, t // tb),
        in_specs=[pspec(0), pspec(1), pspec(2), pspec(3),
                  pl.BlockSpec((None, tb, LANES), lambda b, i: (b, i, 0)),
                  c0spec(0), c0spec(1), c0spec(2),
                  cwspec(0), cwspec(1), cwspec(2),
                  pl.BlockSpec((SUBLANES, LANES), lambda b, i: (0, 0)),
                  pl.BlockSpec((1, DK), lambda b, i: (0, 0)),
                  pl.BlockSpec((None, nh, DK, DK), lambda b, i: (b, 0, 0, 0))],
        out_specs=[pl.BlockSpec((None, tb, w), lambda b, i: (b, i, 0)),
                   pl.BlockSpec((None, nh, DK, DK), lambda b, i: (b, 0, 0, 0))],
        out_shape=[jax.ShapeDtypeStruct((bx, t, w), BF16),
                   jax.ShapeDtypeStruct((bx, nh, DK, DK), F32)],
        scratch_shapes=[pltpu.VMEM((nh, DK, DK), F32),
                        pltpu.VMEM((SUBLANES, w), F32),
                        pltpu.VMEM((SUBLANES, w), F32),
                        pltpu.VMEM((SUBLANES, w), F32)],
        compiler_params=_cparams(("parallel", "arbitrary")),
        name=name,
    )(p5, p5, p5, p5, g3, conv0p, conv0p, conv0p, conv_w, conv_w, conv_w, prm, norm_a, s0)


def _mlstm_kernel(q_ref, k_ref, v_ref, o_ref, z_ref, gt_ref, prm_ref, norm_ref, c0_ref, n0_ref, m0_ref,
                  y_ref, cout_ref, nout_ref, mout_ref, cn_scr, m_scr, *, chunk, tb):
    t = pl.program_id(1)
    nh = H_B
    heads = range(nh)
    ig_lane = 2 * H_A
    lf_lane = 2 * H_A + H_B

    @pl.when(t == 0)
    def _():
        cn_scr[:, :, :DK] = c0_ref[...]
        cn_scr[:, :, DK:] = n0_ref[...]
        m_scr[...] = m0_ref[...]

    gt = gt_ref[...]
    prm = prm_ref[...]
    ig_all = gt + prm[2:3, :]
    lf_all = -_softplus(-(gt + prm[3:4, :]))

    r, c = _iota2((chunk, chunk))
    incl = r >= c
    a_incl = jnp.where(incl, 1.0, 0.0).astype(BF16)
    a_incl_t = jnp.where(r <= c, 1.0, 0.0).astype(BF16)
    eye = jnp.where(r == c, 1.0, 0.0).astype(BF16)
    ones_w = jnp.ones((chunk, DK), F32)
    ones_b = jnp.ones((chunk, DK), BF16)
    norm_g = norm_ref[...]

    def precompute(cis):
        items = []
        for ci in cis:
            sl = slice(ci * chunk, (ci + 1) * chunk)
            lf_c = lf_all[sl]
            ig_c = ig_all[sl]
            b_cols = _dot_mask_lhs(a_incl, lf_c)
            b_rows = _dot_tn_split(lf_c, a_incl_t)
            ig_rows = _dot_tn_split(ig_c, eye)
            for h in heads:
                items.append(dict(ci=ci, sl=sl, h=h, bcol=b_cols[:, lf_lane + h:lf_lane + h + 1],
                                  brow=b_rows[lf_lane + h:lf_lane + h + 1, :],
                                  igrow=ig_rows[ig_lane + h:ig_lane + h + 1, :],
                                  igcol=ig_c[:, ig_lane + h:ig_lane + h + 1]))
        d_logs = [jnp.where(incl, it["bcol"] - it["brow"] + it["igrow"], -jnp.inf) for it in items]
        m_intras = [jnp.max(d, axis=-1, keepdims=True) for d in d_logs]
        qbs = [q_ref[it["sl"], it["h"] * DK:(it["h"] + 1) * DK].astype(BF16) for it in items]
        kcs = [k_ref[it["sl"], it["h"] * DK:(it["h"] + 1) * DK] * (DK ** -0.5) for it in items]
        v1s = [jnp.concatenate([v_ref[it["sl"], it["h"] * DK:(it["h"] + 1) * DK].astype(BF16), ones_b], axis=1)
               for it in items]
        qks = [_dot_nt(qb, kc.astype(BF16)) for qb, kc in zip(qbs, kcs)]
        pq0s = [(jnp.exp(d - mi) * qk).astype(BF16) for d, mi, qk in zip(d_logs, m_intras, qks)]
        avs = [_dot(pq0, v1) for pq0, v1 in zip(pq0s, v1s)]
        out = {ci: [] for ci in cis}
        for it, mi, qb, kc, v1, av in zip(items, m_intras, qbs, kcs, v1s, avs):
            bcol = it["bcol"]
            out[it["ci"]].append(dict(
                b_w=bcol * ones_w, mi_w=mi * ones_w,
                tail_w=(bcol[chunk - 1:chunk, :] - bcol + it["igcol"]) * ones_w,
                qb=qb, kc=kc, v1=v1, av=av))
        return out

    def advance(ci, pre, cns, ms):
        sl = slice(ci * chunk, (ci + 1) * chunk)
        qcns = [_dot(pre[h]["qb"], cns[h].astype(BF16)) for h in heads]
        bms = [pre[h]["b_w"] + ms[h] for h in heads]
        m_ts = [jnp.maximum(bms[h], pre[h]["mi_w"]) for h in heads]
        corrs = [jnp.exp(pre[h]["mi_w"] - m_ts[h]) for h in heads]
        inters = [jnp.exp(bms[h] - m_ts[h]) for h in heads]
        m_lasts = [m_t[chunk - 1:chunk, :] for m_t in m_ts]
        i_lasts = [inter[chunk - 1:chunk, :] for inter in inters]
        kps = [(pre[h]["kc"] * jnp.exp(pre[h]["tail_w"] - m_lasts[h])).astype(BF16) for h in heads]
        new_cns = [jnp.concatenate([i_lasts[h], i_lasts[h]], axis=1) * cns[h] + _dot_tn(kps[h], pre[h]["v1"])
                   for h in heads]
        for h in heads:
            num = inters[h] * qcns[h][:, :DK] + corrs[h] * pre[h]["av"][:, :DK]
            den = inters[h] * qcns[h][:, DK:] + corrs[h] * pre[h]["av"][:, DK:]
            hh = num / jnp.maximum(jnp.abs(den), jnp.exp(-m_ts[h]))
            hh = _sigmoid(o_ref[sl, h * DK:(h + 1) * DK]) * hh
            y_ref[sl, h * DK:(h + 1) * DK] = (_rms_norm(hh, norm_g[:, h * DK:(h + 1) * DK])
                                              * _silu(z_ref[sl, h * DK:(h + 1) * DK])).astype(y_ref.dtype)
        return new_cns, m_lasts

    n_chunks = tb // chunk
    cns = [cn_scr[h] for h in heads]
    ms = [m_scr[h] for h in heads]
    groups = [list(range(g0, min(g0 + MLSTM_CHUNK_GROUP, n_chunks)))
              for g0 in range(0, n_chunks, MLSTM_CHUNK_GROUP)]
    pre = precompute(groups[0])
    for gi, group in enumerate(groups):
        nxt = precompute(groups[gi + 1]) if gi + 1 < len(groups) else None
        for ci in group:
            cns, ms = advance(ci, pre[ci], cns, ms)
        pre = nxt
    for h in heads:
        cn_scr[h] = cns[h]
        m_scr[h] = ms[h]

    @pl.when(t == pl.num_programs(1) - 1)
    def _():
        for h in heads:
            cout_ref[h] = cns[h][:, :DK]
            nout_ref[h] = cns[h][:, DK:]
            mout_ref[h] = ms[h]


def _mlstm(p5, g3, prm, norm_b, c0, n0w, m0w, chunk, tb, name):
    _, bx, t, w = p5.shape
    nh = H_B

    def pspec(group):
        return pl.BlockSpec((None, None, tb, w), lambda b, i, g=group: (g, b, i, 0))

    mat = pl.BlockSpec((None, nh, DK, DK), lambda b, i: (b, 0, 0, 0))
    row = pl.BlockSpec((None, nh, 1, DK), lambda b, i: (b, 0, 0, 0))
    return pl.pallas_call(
        functools.partial(_mlstm_kernel, chunk=chunk, tb=tb),
        grid=(bx, t // tb),
        in_specs=[pspec(4), pspec(5), pspec(6), pspec(7), pspec(8),
                  pl.BlockSpec((None, tb, LANES), lambda b, i: (b, i, 0)),
                  pl.BlockSpec((SUBLANES, LANES), lambda b, i: (0, 0)),
                  pl.BlockSpec((1, w), lambda b, i: (0, 0)),
                  mat, mat, row],
        out_specs=[pl.BlockSpec((None, tb, w), lambda b, i: (b, i, 0)), mat, mat, row],
        out_shape=[jax.ShapeDtypeStruct((bx, t, w), BF16),
                   jax.ShapeDtypeStruct((bx, nh, DK, DK), F32),
                   jax.ShapeDtypeStruct((bx, nh, DK, DK), F32),
                   jax.ShapeDtypeStruct((bx, nh, 1, DK), F32)],
        scratch_shapes=[pltpu.VMEM((nh, DK, 2 * DK), F32), pltpu.VMEM((nh, 1, DK), F32)],
        compiler_params=_cparams(("parallel", "arbitrary")),
        name=name,
    )(p5, p5, p5, p5, p5, g3, prm, norm_b, c0, n0w, m0w)


def _sb_tiles(qbs, kts, vts, runs, u_neg, causal):
    tk = kts[0].shape[0]
    zs = [_dot_nt(qb, kt.astype(BF16)) for qb, kt in zip(qbs, kts)]
    sps = [jnp.maximum(z, 0.0) + jnp.log(1.0 + jnp.exp(-jnp.abs(z))) for z in zs]
    if causal is not None:
        sps = [jnp.where(causal, sp, 0.0) for sp in sps]
    his = [sp.astype(BF16) for sp in sps]
    los = [(sp - hi.astype(F32)).astype(BF16) for sp, hi in zip(sps, his)]
    exts = [_dot(jnp.concatenate([hi, lo], axis=1), u_neg) for hi, lo in zip(his, los)]
    atts = [jnp.exp(z + ext[:, :tk] + run[:, :tk]) for z, ext, run in zip(zs, exts, runs)]
    if causal is not None:
        atts = [jnp.where(causal, att, 0.0) for att in atts]
    contribs = [_dot(att.astype(BF16), vt.astype(BF16)) for att, vt in zip(atts, vts)]
    return [run + ext[:, tk:] for run, ext in zip(runs, exts)], contribs


def _sb_kernel(q_ref, z_ref, kc_ref, vc_ref, k1_ref, v1_ref, kany_ref, vany_ref,
               y_ref, ko_ref, vo_ref, acc_scr, run_scr, q_scr, kbuf, vbuf, flag, sem,
               *, tq, tk, prev_blocks_per_step, prev_blocks_fixed, prev_token_rows, kgroup, vgroup):
    nh = H_C
    b = pl.program_id(0)
    i = pl.program_id(1)
    r, c = _iota2((tq, tq))
    causal = c < r
    def suffix_sum_matrix(n):
        rr, cc = _iota2((2 * n, n + LANES))
        rr = jnp.where(rr >= n, rr - n, rr)
        return jnp.where((rr >= cc) | (cc >= n), -1.0, 0.0).astype(BF16)

    u_diag = suffix_sum_matrix(tq)
    u_prev = suffix_sum_matrix(tk)
    q_scr[...] = (q_ref[...] * (DK ** -0.5)).astype(BF16)

    def q_head(h):
        return q_scr[:, h * DK:(h + 1) * DK]

    def set_flag():
        best = run_scr[0]
        for h in range(1, nh):
            best = jnp.maximum(best, run_scr[h])
        flag[0] = (jnp.max(best) > SB_DEAD_LOG_WEIGHT).astype(jnp.int32)

    heads = range(nh)
    kts = [kc_ref[:, h * DK:(h + 1) * DK] for h in heads]
    vts = [vc_ref[:, h * DK:(h + 1) * DK] for h in heads]
    for h in heads:
        ko_ref[pl.ds(h, tq, stride=nh), :] = kts[h]
        vo_ref[pl.ds(h, tq, stride=nh), :] = vts[h]
    runs, contribs = _sb_tiles([q_head(h) for h in heads], kts, vts, [jnp.zeros((tq, LANES), F32)] * nh, u_diag,
                               causal)
    for h in heads:
        run_scr[h] = runs[h]
        acc_scr[:, h * DK:(h + 1) * DK] = contribs[h]
    flag[0] = 1

    def prev_tile(ref, h):
        if prev_token_rows:
            return ref[pl.ds(h, tk, stride=nh), :]
        return ref[:, h * DK:(h + 1) * DK]

    def stage(k_ref, v_ref):
        runs, contribs = _sb_tiles([q_head(h) for h in heads], [prev_tile(k_ref, h) for h in heads],
                                   [prev_tile(v_ref, h) for h in heads], [run_scr[h] for h in heads], u_prev, None)
        for h in heads:
            run_scr[h] = runs[h]
            acc_scr[:, h * DK:(h + 1) * DK] += contribs[h]
        set_flag()

    nprev = i * prev_blocks_per_step + prev_blocks_fixed

    @pl.when(jnp.logical_and(nprev >= 1, flag[0] > 0))
    def _():
        stage(k1_ref, v1_ref)

    def far_copy(any_ref, group, buf, j, slot):
        start = (nprev - j) * tk
        if prev_token_rows:
            src = any_ref.at[b, pl.ds(pl.multiple_of(start * nh, tk * nh), tk * nh), :]
        else:
            src = any_ref.at[group, b, pl.ds(pl.multiple_of(start, tk), tk), :]
        return pltpu.make_async_copy(src, buf, sem.at[slot])

    def cond(carry):
        j, go = carry
        return jnp.logical_and(j <= nprev, go > 0)

    def body(carry):
        j, _ = carry
        kcopy = far_copy(kany_ref, kgroup, kbuf, j, 0)
        vcopy = far_copy(vany_ref, vgroup, vbuf, j, 1)
        kcopy.start()
        vcopy.start()
        kcopy.wait()
        vcopy.wait()
        stage(kbuf, vbuf)
        return j + 1, flag[0]

    lax.while_loop(cond, body, (jnp.int32(2), flag[0]))
    y_ref[...] = (acc_scr[...] * _silu(z_ref[...])).astype(y_ref.dtype)


def _sb(p5, kprev, vprev, tq, causal_prev, name):
    _, bx, t, w = p5.shape
    nh = H_C
    tk = SB_TILE if not causal_prev else tq

    def blk(group):
        return pl.BlockSpec((None, None, tq, w), lambda b, i, g=group: (g, b, i, 0))

    if causal_prev:
        def near(group):
            return pl.BlockSpec((None, None, tk, w), lambda b, i, g=group: (g, b, jnp.maximum(i - 1, 0), 0))
        kprev, vprev = p5, p5
        near_specs = [near(1), near(2)]
        buf_shape = (tk, w)
        per_step, fixed = tq // tk, 0
    else:
        nblk = kprev.shape[1] // (tk * nh)
        last = pl.BlockSpec((None, tk * nh, DK), lambda b, i: (b, nblk - 1, 0))
        near_specs = [last, last]
        buf_shape = (tk * nh, DK)
        per_step, fixed = 0, nblk
    kern = functools.partial(_sb_kernel, tq=tq, tk=tk, prev_blocks_per_step=per_step, prev_blocks_fixed=fixed,
                             prev_token_rows=not causal_prev, kgroup=1, vgroup=2)
    rows_out = pl.BlockSpec((None, tq * nh, DK), lambda b, i: (b, i, 0))
    return pl.pallas_call(
        kern,
        grid=(bx, t // tq),
        in_specs=[blk(0), blk(3), blk(1), blk(2)] + near_specs
                 + [pl.BlockSpec(memory_space=pl.ANY), pl.BlockSpec(memory_space=pl.ANY)],
        out_specs=[pl.BlockSpec((None, tq, w), lambda b, i: (b, i, 0)), rows_out, rows_out],
        out_shape=[jax.ShapeDtypeStruct((bx, t, w), BF16),
                   jax.ShapeDtypeStruct((bx, t * nh, DK), F32),
                   jax.ShapeDtypeStruct((bx, t * nh, DK), F32)],
        scratch_shapes=[pltpu.VMEM((tq, w), F32), pltpu.VMEM((nh, tq, LANES), F32), pltpu.VMEM((tq, w), BF16),
                        pltpu.VMEM(buf_shape, F32), pltpu.VMEM(buf_shape, F32),
                        pltpu.SMEM((1,), jnp.int32), pltpu.SemaphoreType.DMA((2,))],
        compiler_params=_cparams(("parallel", "arbitrary")),
        name=name,
    )(p5, p5, p5, p5, kprev, vprev, kprev, vprev)


def _s5_kernel(u_ref, z_ref, bb_ref, cc_ref, ar_ref, ai_ref, d_ref, wg_ref, bg_ref, h0r_ref, h0i_ref,
               y_ref, hr_out, hi_out, hs, ys, hr_scr, hi_scr, *, tt):
    t = pl.program_id(1)
    half = GB_D * P_D
    nl = half // LANES

    @pl.when(t == 0)
    def _():
        hr_scr[...] = h0r_ref[...]
        hi_scr[...] = h0i_ref[...]

    for gb in range(NGB_D):
        rows = pl.ds(gb, tt, stride=NGB_D)
        bu = _dot(u_ref[:, gb * LANES:(gb + 1) * LANES].astype(BF16), bb_ref[gb])
        for j in range(2 * nl):
            hs[j, rows, :] = bu[:, j * LANES:(j + 1) * LANES]

    ar = [ar_ref[:, j * LANES:(j + 1) * LANES] for j in range(nl)]
    ai = [ai_ref[:, j * LANES:(j + 1) * LANES] for j in range(nl)]

    def step(s, carry):
        rows = pl.ds(pl.multiple_of(s * NGB_D, NGB_D), NGB_D)
        new = [None] * (2 * nl)
        for j in range(nl):
            hr, hi = carry[j], carry[nl + j]
            new[j] = ar[j] * hr - ai[j] * hi + hs[j, rows, :]
            new[nl + j] = ar[j] * hi + ai[j] * hr + hs[nl + j, rows, :]
        for j in range(2 * nl):
            hs[j, rows, :] = new[j]
        return tuple(new)

    init = tuple(hr_scr[:, j * LANES:(j + 1) * LANES] for j in range(nl)) + tuple(
        hi_scr[:, j * LANES:(j + 1) * LANES] for j in range(nl))
    fin = lax.fori_loop(0, tt, step, init, unroll=8)
    hr = jnp.concatenate(fin[:nl], axis=1)
    hi = jnp.concatenate(fin[nl:], axis=1)
    hr_scr[...] = hr
    hi_scr[...] = hi

    for gb in range(NGB_D):
        rows = pl.ds(gb, tt, stride=NGB_D)
        hcat = jnp.concatenate([hs[j, rows, :] for j in range(2 * nl)], axis=1)
        lanes = slice(gb * LANES, (gb + 1) * LANES)
        ys[:, lanes] = _dot(hcat.astype(BF16), cc_ref[gb]) + d_ref[:, lanes] * u_ref[:, lanes]

    y = ys[...]
    yg = 0.5 * y * (1.0 + jnp.tanh(0.7978845608028654 * (y + 0.044715 * (y * y * y))))
    gate = _sigmoid(_dot(yg.astype(BF16), wg_ref[...]) + bg_ref[...])
    y_ref[...] = (yg * gate * _silu(z_ref[...])).astype(y_ref.dtype)

    @pl.when(t == pl.num_programs(1) - 1)
    def _():
        hr_out[...] = hr
        hi_out[...] = hi


def _s5(p5, ugroup, zgroup, bb, cc, a_re, a_im, d, w_glu_bf16, b_glu, h0r, h0i, tt, name):
    _, bx, t, w = p5.shape
    half = GB_D * P_D
    st = pl.BlockSpec((None, NGB_D, half), lambda b, i: (b, 0, 0))
    prm = pl.BlockSpec((NGB_D, half), lambda b, i: (0, 0))
    row = pl.BlockSpec((1, w), lambda b, i: (0, 0))
    return pl.pallas_call(
        functools.partial(_s5_kernel, tt=tt),
        grid=(bx, t // tt),
        in_specs=[pl.BlockSpec((None, None, tt, w), lambda b, i: (ugroup, b, i, 0)),
                  pl.BlockSpec((None, None, tt, w), lambda b, i: (zgroup, b, i, 0)),
                  pl.BlockSpec((NGB_D, LANES, 2 * half), lambda b, i: (0, 0, 0)),
                  pl.BlockSpec((NGB_D, 2 * half, LANES), lambda b, i: (0, 0, 0)),
                  prm, prm, row,
                  pl.BlockSpec((w, w), lambda b, i: (0, 0)),
                  row, st, st],
        out_specs=[pl.BlockSpec((None, tt, w), lambda b, i: (b, i, 0)), st, st],
        out_shape=[jax.ShapeDtypeStruct((bx, t, w), BF16),
                   jax.ShapeDtypeStruct((bx, NGB_D, half), F32),
                   jax.ShapeDtypeStruct((bx, NGB_D, half), F32)],
        scratch_shapes=[pltpu.VMEM((2 * half // LANES, tt * NGB_D, LANES), F32),
                        pltpu.VMEM((tt, w), F32),
                        pltpu.VMEM((NGB_D, half), F32), pltpu.VMEM((NGB_D, half), F32)],
        compiler_params=_cparams(("parallel", "arbitrary")),
        name=name,
    )(p5, p5, bb, cc, a_re, a_im, d.reshape(1, w), w_glu_bf16, b_glu.reshape(1, w), h0r, h0i)


def _pack_even_weights(w_in, conv_w, a_log, dt_bias, norm_a, ig_bias, fg_bias, norm_b, w_out, ln_g, ln_b):
    d = w_in.shape[0]
    wa = H_A * DK
    wb = H_B * DK
    w_all = w_in.astype(BF16)
    n_a = 4 * wa
    w_a = w_all[:, :n_a]
    o = n_a
    gate_a = w_all[:, o:o + 2 * H_A]; o += 2 * H_A
    w_b = w_all[:, o:o + 5 * wb]; o += 5 * wb
    gate_b = w_all[:, o:o + 2 * H_B]
    n_gate = 2 * H_A + 2 * H_B
    w_gate = jnp.concatenate([gate_a, gate_b, jnp.zeros((d, LANES - n_gate), BF16)], axis=1)
    prm = jnp.zeros((SUBLANES, LANES), F32)
    prm = prm.at[0, H_A:2 * H_A].set(a_log)
    prm = prm.at[1, H_A:2 * H_A].set(dt_bias)
    prm = prm.at[2, 2 * H_A:2 * H_A + H_B].set(ig_bias)
    prm = prm.at[3, 2 * H_A + H_B:2 * H_A + 2 * H_B].set(fg_bias)
    return dict(w_a=w_a, w_b=w_b, w_gate=w_gate, conv_w=conv_w, prm=prm,
                norm_a=norm_a.reshape(1, DK),
                norm_b=norm_b.reshape(1, wb), w_out=w_out.astype(BF16), ln_g=ln_g, ln_b=ln_b)


def _tile(n, preferred):
    tile = min(n, preferred)
    assert n % tile == 0, (n, preferred)
    return tile


def _even_layer(x, s0, conv0, c0, n0, m0, wp, alpha):
    bx, t, d = x.shape
    m = bx * t
    chunk = min(CHUNK, t)
    tm = _tile(m, PROJ_ROWS)
    tb = _tile(t, MIXER_ROWS)
    x2 = x.reshape(m, d)
    p3, g2 = _proj_gates(x2, wp["w_a"], wp["w_b"], wp["w_gate"], tm, "ev_proj")
    p5 = p3.reshape(9, bx, t, W_HEADS)
    g3 = g2.reshape(bx, t, LANES)
    conv0p = jnp.concatenate([jnp.zeros((bx, SUBLANES - (CONV_W - 1), conv0.shape[2]), F32), conv0], axis=1)
    ya, s_new = _gdn(p5, g3, conv0p, wp["conv_w"], wp["prm"], wp["norm_a"], s0, chunk, tb, "gdn")
    m0w = jnp.broadcast_to(m0[:, :, None, None], (bx, H_B, 1, DK))
    n0w = jnp.broadcast_to(n0[:, :, :, None], (bx, H_B, DK, DK))
    yb, c_new, n_new, m_new = _mlstm(p5, g3, wp["prm"], wp["norm_b"], c0, n0w, m0w, chunk, tb, "mlstm")
    xn = _outproj_ln(ya.reshape(m, -1), yb.reshape(m, -1), x2, wp["w_out"], wp["ln_g"], wp["ln_b"], alpha,
                     _tile(m, OUTPROJ_ROWS), "ev_outproj_ln")
    conv_new = jnp.concatenate([p5[0, :, t - (CONV_W - 1):], p5[1, :, t - (CONV_W - 1):], p5[2, :, t - (CONV_W - 1):]],
                               axis=-1)
    return xn.reshape(bx, t, d), s_new, conv_new, c_new, n_new[:, :, :, 0], m_new[:, :, 0, 0]


def _pack_odd_weights(w_in, lam_re, lam_im, b_re, b_im, c_re, c_im, d, log_dt, w_glu, b_glu, w_out, ln_g, ln_b):
    dm = w_in.shape[0]
    w_main = w_in.astype(BF16)
    dt = jnp.exp(log_dt)[:, None]
    mag = jnp.exp(lam_re * dt)
    a_re, a_im = mag * jnp.cos(lam_im * dt), mag * jnp.sin(lam_im * dt)
    den = lam_re * lam_re + lam_im * lam_im
    f_re = ((a_re - 1.0) * lam_re + a_im * lam_im) / den
    f_im = (a_im * lam_re - (a_re - 1.0) * lam_im) / den
    bb_re = f_re[..., None] * b_re - f_im[..., None] * b_im
    bb_im = f_re[..., None] * b_im + f_im[..., None] * b_re
    half = GB_D * P_D
    eye = jnp.eye(GB_D, dtype=F32)

    def blockdiag_in(bb):
        bbg = bb.reshape(NGB_D, GB_D, P_D, GC_D)
        return jnp.einsum("bgpc,gh->bgchp", bbg, eye).reshape(NGB_D, GB_D * GC_D, half)

    def blockdiag_out(cm):
        cg = cm.reshape(NGB_D, GB_D, GC_D, P_D)
        return jnp.einsum("bgcp,gh->bgphc", cg, eye).reshape(NGB_D, half, GB_D * GC_D)

    bb = jnp.concatenate([blockdiag_in(bb_re), blockdiag_in(bb_im)], axis=2).astype(BF16)
    cc = jnp.concatenate([blockdiag_out(c_re), -blockdiag_out(c_im)], axis=1).astype(BF16)
    return dict(w_main=w_main, bb=bb, cc=cc, a_re=a_re.reshape(NGB_D, half), a_im=a_im.reshape(NGB_D, half),
                d=d, w_glu=w_glu.astype(BF16), b_glu=b_glu, w_out=w_out.astype(BF16), ln_g=ln_g, ln_b=ln_b)


def _odd_layer(x, k_cache, v_cache, h_re0, h_im0, wp, alpha):
    bx, t, d = x.shape
    m = bx * t
    tm = _tile(m, PROJ_ROWS)
    tq = _tile(t, SB_TILE)
    tt = _tile(t, S5_ROWS)
    x2 = x.reshape(m, d)
    p5 = _proj(x2, wp["w_main"], tm, "od_proj").reshape(6, bx, t, W_HEADS)
    if k_cache is None:
        yc, kk, vv = _sb(p5, None, None, tq, True, "sb_prompt")
    else:
        past = k_cache.shape[1]
        yc, kk, vv = _sb(p5, k_cache.reshape(bx, past * H_C, DK), v_cache.reshape(bx, past * H_C, DK), tq, False,
                         "sb_sample")
    half = GB_D * P_D
    yd, hr, hi = _s5(p5, 4, 5, wp["bb"], wp["cc"], wp["a_re"], wp["a_im"], wp["d"], wp["w_glu"], wp["b_glu"],
                     h_re0.reshape(bx, NGB_D, half), h_im0.reshape(bx, NGB_D, half), tt, "s5_glu")
    xn = _outproj_ln(yc.reshape(m, -1), yd.reshape(m, -1), x2, wp["w_out"], wp["ln_g"], wp["ln_b"], alpha,
                     _tile(m, OUTPROJ_ROWS), "od_outproj_ln")
    return (xn.reshape(bx, t, d), kk.reshape(bx, t, H_C, DK), vv.reshape(bx, t, H_C, DK),
            hr.reshape(bx, G_D, P_D), hi.reshape(bx, G_D, P_D))


def kernel(x_prompt, x_sample, state_delta_S, state_delta_conv, state_mlstm_C, state_mlstm_n, state_mlstm_m, cache_sb_k, cache_sb_v, state_s5_re, state_s5_im, ev_w_in, ev_conv_w, ev_a_log, ev_dt_bias, ev_norm_a, ev_ig_bias, ev_fg_bias, ev_norm_b, ev_w_out, ev_ln_g, ev_ln_b, od_w_in, od_lam_re, od_lam_im, od_b_re, od_b_im, od_c_re, od_c_im, od_d, od_log_dt, od_w_glu, od_b_glu, od_w_out, od_ln_g, od_ln_b):
    n_even = ev_w_in.shape[0]
    n_odd = od_w_in.shape[0]
    depth = n_even + n_odd
    alpha = (2 * depth) ** 0.25
    nb, seq, _ = x_prompt.shape
    ns, dseq, _ = x_sample.shape
    xp, xs = x_prompt, x_sample
    outs = [[] for _ in range(18)]
    for layer in range(depth):
        j = layer // 2
        if layer % 2 == 0:
            wp = _pack_even_weights(ev_w_in[j], ev_conv_w[j], ev_a_log[j], ev_dt_bias[j], ev_norm_a[j], ev_ig_bias[j],
                                    ev_fg_bias[j], ev_norm_b[j], ev_w_out[j], ev_ln_g[j], ev_ln_b[j])
            zs = jnp.zeros((nb, H_A, DK, DK), F32)
            xp, s, cv, c, n, m = _even_layer(
                xp, zs, jnp.zeros((nb, CONV_W - 1, 3 * H_A * DK), F32), zs, jnp.zeros((nb, H_B, DK), F32),
                jnp.zeros((nb, H_B), F32), wp, alpha)
            for idx, val in zip((0, 2, 4, 6, 8), (s, cv, c, n, m)):
                outs[idx].append(val)
            xs, s, cv, c, n, m = _even_layer(
                xs, state_delta_S[j], state_delta_conv[j], state_mlstm_C[j], state_mlstm_n[j], state_mlstm_m[j],
                wp, alpha)
            for idx, val in zip((1, 3, 5, 7, 9), (s, cv, c, n, m)):
                outs[idx].append(val)
        else:
            wp = _pack_odd_weights(od_w_in[j], od_lam_re[j], od_lam_im[j], od_b_re[j], od_b_im[j], od_c_re[j],
                                   od_c_im[j], od_d[j], od_log_dt[j], od_w_glu[j], od_b_glu[j], od_w_out[j],
                                   od_ln_g[j], od_ln_b[j])
            zst = jnp.zeros((nb, G_D, P_D), F32)
            xp, k, v, hr, hi = _odd_layer(xp, None, None, zst, zst, wp, alpha)
            for idx, val in zip((10, 12, 14, 16), (k, v, hr, hi)):
                outs[idx].append(val)
            xs, k, v, hr, hi = _odd_layer(xs, cache_sb_k[j], cache_sb_v[j], state_s5_re[j], state_s5_im[j], wp, alpha)
            for idx, val in zip((11, 13, 15, 17), (k, v, hr, hi)):
                outs[idx].append(val)
    return (xp, xs) + tuple(jnp.stack(o) for o in outs)
```

```python
import functools

import jax
import jax.numpy as jnp
from jax import lax
from jax.experimental import pallas as pl
from jax.experimental.pallas import tpu as pltpu

F32 = jnp.float32
BF16 = jnp.bfloat16

LANES = 128
SUBLANES = 8
CHUNK = 64
CONV_W = 4
H_A = 8
H_B = 8
H_C = 8
DK = 128
GC_D = 16
G_D = 64
P_D = 64
GB_D = 8
NGB_D = G_D // GB_D
W_HEADS = 1024
LN_EPS = 1e-5
NORM_EPS = 1e-6
VMEM_LIMIT = 48 * 1024 * 1024
SB_DEAD_LOG_WEIGHT = -88.0
SB_TILE = 128
PROJ_ROWS = 1024
OUTPROJ_ROWS = 512
MIXER_ROWS = 512
S5_ROWS = 512
MLSTM_CHUNK_GROUP = 1
GDN_CHUNK_GROUP = 4


def _cparams(sem):
    return pltpu.CompilerParams(dimension_semantics=sem, vmem_limit_bytes=VMEM_LIMIT)


def _sigmoid(x):
    return 1.0 / (1.0 + jnp.exp(-x))


def _silu(x):
    return x * _sigmoid(x)


def _softplus(x):
    return jnp.maximum(x, 0.0) + jnp.log1p(jnp.exp(-jnp.abs(x)))


def _dot(a, b):
    return jnp.dot(a, b, preferred_element_type=F32)


def _dot_nt(a, b):
    return lax.dot_general(a, b, (((1,), (1,)), ((), ())), preferred_element_type=F32)


def _dot_tn(a, b):
    return lax.dot_general(a, b, (((0,), (0,)), ((), ())), preferred_element_type=F32)


def _split3(x):
    hi = x.astype(BF16)
    r1 = x - hi.astype(F32)
    mid = r1.astype(BF16)
    lo = (r1 - mid.astype(F32)).astype(BF16)
    return hi, mid, lo


def _dot_mask_lhs(mask_bf16, x):
    hi, mid, lo = _split3(x)
    return _dot(mask_bf16, hi) + _dot(mask_bf16, mid) + _dot(mask_bf16, lo)


def _dot_tn_split(x, mask_bf16):
    hi, mid, lo = _split3(x)
    return _dot_tn(hi, mask_bf16) + _dot_tn(mid, mask_bf16) + _dot_tn(lo, mask_bf16)


def _iota2(shape):
    return (lax.broadcasted_iota(jnp.int32, shape, 0), lax.broadcasted_iota(jnp.int32, shape, 1))


def _head(x, h):
    return x[:, h * DK:(h + 1) * DK]


def _proj_kernel(x_ref, w_ref, o_ref):
    o_ref[0] = _dot(x_ref[...].astype(BF16), w_ref[...])


def _proj_gates_kernel(x_ref, w_ref, wg_ref, o_ref, og_ref):
    xb = x_ref[...].astype(BF16)
    o_ref[0] = _dot(xb, w_ref[...])

    @pl.when(pl.program_id(1) == 0)
    def _():
        og_ref[...] = _dot(xb, wg_ref[...])


def _proj(x2, w, w_gate, tm, name):
    m, d = x2.shape
    wn = W_HEADS
    ng = w.shape[1] // wn
    x_spec = pl.BlockSpec((tm, d), lambda i, j: (i, 0))
    w_spec = pl.BlockSpec((d, wn), lambda i, j: (0, j))
    o_spec = pl.BlockSpec((1, tm, wn), lambda i, j: (j, i, 0))
    o_shape = jax.ShapeDtypeStruct((ng, m, wn), F32)
    if w_gate is None:
        return pl.pallas_call(
            _proj_kernel, grid=(m // tm, ng), in_specs=[x_spec, w_spec], out_specs=o_spec, out_shape=o_shape,
            compiler_params=_cparams(("parallel", "arbitrary")), name=name)(x2, w)
    return pl.pallas_call(
        _proj_gates_kernel,
        grid=(m // tm, ng),
        in_specs=[x_spec, w_spec, pl.BlockSpec((d, LANES), lambda i, j: (0, 0))],
        out_specs=[o_spec, pl.BlockSpec((tm, LANES), lambda i, j: (i, 0))],
        out_shape=[o_shape, jax.ShapeDtypeStruct((m, LANES), F32)],
        compiler_params=_cparams(("parallel", "arbitrary")),
        name=name,
    )(x2, w, w_gate)


def _outproj_ln_kernel(ya_ref, yb_ref, x_ref, wa_ref, wb_ref, g_ref, b_ref, o_ref, *, alpha):
    acc = _dot(ya_ref[...].astype(BF16), wa_ref[...]) + _dot(yb_ref[...].astype(BF16), wb_ref[...])
    r = alpha * x_ref[...] + acc
    mu = jnp.mean(r, axis=-1, keepdims=True)
    c = r - mu
    var = jnp.mean(c * c, axis=-1, keepdims=True)
    o_ref[...] = c * lax.rsqrt(var + LN_EPS) * g_ref[...] + b_ref[...]


def _outproj_ln(ya, yb, x2, w_out_bf16, ln_g, ln_b, alpha, tm, name):
    m, d = x2.shape
    wa = ya.shape[1]
    wb = yb.shape[1]
    return pl.pallas_call(
        functools.partial(_outproj_ln_kernel, alpha=alpha),
        grid=(m // tm,),
        in_specs=[pl.BlockSpec((tm, wa), lambda i: (i, 0)),
                  pl.BlockSpec((tm, wb), lambda i: (i, 0)),
                  pl.BlockSpec((tm, d), lambda i: (i, 0)),
                  pl.BlockSpec((wa, d), lambda i: (0, 0)),
                  pl.BlockSpec((wb, d), lambda i: (wa // wb, 0)),
                  pl.BlockSpec((1, d), lambda i: (0, 0)),
                  pl.BlockSpec((1, d), lambda i: (0, 0))],
        out_specs=pl.BlockSpec((tm, d), lambda i: (i, 0)),
        out_shape=jax.ShapeDtypeStruct((m, d), F32),
        compiler_params=_cparams(("parallel",)),
        name=name,
    )(ya, yb, x2, w_out_bf16, w_out_bf16, ln_g.reshape(1, d), ln_b.reshape(1, d))


def _conv_carry_init(carry, c0_ref, w_ref):
    w = w_ref[...]
    xm3, xm2, xm1 = c0_ref[5:6, :], c0_ref[6:7, :], c0_ref[7:8, :]
    carry[0:1, :] = w[0:1, :] * xm1
    carry[1:2, :] = w[1:2, :] * xm1 + w[0:1, :] * xm2
    carry[2:3, :] = w[2:3, :] * xm1 + (w[1:2, :] * xm2 + w[0:1, :] * xm3)


def _conv_silu(x_ref, carry, w_ref, tb):
    x = x_ref[...]
    w = w_ref[...]
    first = lax.broadcasted_iota(jnp.int32, (SUBLANES, x.shape[1]), 0) == 0
    acc = x * w[0:1, :]
    for i in range(1, CONV_W):
        shifted_in = carry[i - 1:i, :]
        carry[i - 1:i, :] = acc[tb - 1:tb, :]
        rolled = pltpu.roll(acc, 1, axis=0)
        head = jnp.where(first, shifted_in, rolled[0:SUBLANES])
        acc = x * w[i:i + 1, :] + jnp.concatenate([head, rolled[SUBLANES:]], axis=0)
    return _silu(acc)


def _l2norm(x):
    return x * lax.rsqrt(jnp.sum(x * x, axis=-1, keepdims=True) + NORM_EPS)


def _rms_norm(x, g):
    return x * lax.rsqrt(jnp.mean(x * x, axis=-1, keepdims=True) + NORM_EPS) * g


def _unit_lower_inverses(n_mats, length):
    r, c = _iota2((length, length))
    eye = jnp.where(r == c, 1.0, 0.0)
    xs = [eye + n for n in n_mats]
    ps = list(n_mats)
    for _ in range(length.bit_length() - 2):
        pbs = [p.astype(BF16) for p in ps]
        ps = [_dot(pb, pb) for pb in pbs]
        xs = [x + _dot(x.astype(BF16), p.astype(BF16)) for x, p in zip(xs, ps)]
    return xs


def _gdn_kernel(qp_ref, kp_ref, vp_ref, z_ref, gt_ref, c0q_ref, c0k_ref, c0v_ref,
                cwq_ref, cwk_ref, cwv_ref, prm_ref, norm_ref, s0_ref,
                y_ref, sout_ref, s_scr, xq_scr, xk_scr, xv_scr, *, chunk, tb):
    t = pl.program_id(1)
    nh = H_A

    @pl.when(t == 0)
    def _():
        s_scr[...] = s0_ref[...]
        _conv_carry_init(xq_scr, c0q_ref, cwq_ref)
        _conv_carry_init(xk_scr, c0k_ref, cwk_ref)
        _conv_carry_init(xv_scr, c0v_ref, cwv_ref)

    q_all = _conv_silu(qp_ref, xq_scr, cwq_ref, tb)
    k_all = _conv_silu(kp_ref, xk_scr, cwk_ref, tb)
    v_all = _conv_silu(vp_ref, xv_scr, cwv_ref, tb)
    q_h = [_l2norm(_head(q_all, h)) * (DK ** -0.5) for h in range(nh)]
    k_h = [_l2norm(_head(k_all, h)) for h in range(nh)]

    gt = gt_ref[...]
    prm = prm_ref[...]
    beta_all = _sigmoid(gt)
    g_all = -jnp.exp(prm[0:1, :]) * _softplus(gt + prm[1:2, :])

    r, c = _iota2((chunk, chunk))
    incl = r >= c
    strict = r > c
    a_incl = jnp.where(incl, 1.0, 0.0).astype(BF16)
    a_incl_t = jnp.where(r <= c, 1.0, 0.0).astype(BF16)
    norm_g = norm_ref[...]

    def precompute(cis):
        items = []
        for ci in cis:
            sl = slice(ci * chunk, (ci + 1) * chunk)
            g_c = g_all[sl]
            gc_cols = _dot_mask_lhs(a_incl, g_c)
            gc_rows = _dot_tn_split(g_c, a_incl_t)
            beta_c = beta_all[sl]
            for h in range(nh):
                items.append(dict(ci=ci, gcol=gc_cols[:, nh + h:nh + h + 1], grow=gc_rows[nh + h:nh + h + 1, :],
                                  bc=beta_c[:, h:h + 1], qc=q_h[h][sl], kc=k_h[h][sl], vc=_head(v_all, h)[sl]))
        qkks = [_dot_nt(jnp.concatenate([it["qc"], it["kc"]], axis=0).astype(BF16), it["kc"].astype(BF16))
                for it in items]
        decays = [jnp.where(incl, jnp.exp(it["gcol"] - it["grow"]), 0.0) for it in items]
        n_mats = [jnp.where(strict, -(it["bc"] * qkk[chunk:] * decay), 0.0)
                  for it, qkk, decay in zip(items, qkks, decays)]
        t_invs = _unit_lower_inverses(n_mats, chunk)
        gams = [jnp.exp(it["gcol"]) for it in items]
        rhss = [jnp.concatenate([it["vc"] * it["bc"], it["kc"] * (it["bc"] * gam)], axis=-1).astype(BF16)
                for it, gam in zip(items, gams)]
        sols = [_dot(ti.astype(BF16), rhs) for ti, rhs in zip(t_invs, rhss)]
        out = {ci: [] for ci in cis}
        for it, qkk, decay, gam, sol in zip(items, qkks, decays, gams, sols):
            g_last = it["gcol"][chunk - 1:chunk, :]
            out[it["ci"]].append(dict(
                u=sol[:, :DK],
                wq=jnp.concatenate([sol[:, DK:], it["qc"] * gam], axis=0).astype(BF16),
                qk=(qkk[:chunk] * decay).astype(BF16),
                k_dec=(it["kc"] * jnp.exp(g_last - it["gcol"])).astype(BF16),
                s_decay=jnp.exp(g_last)))
        return out

    def advance(ci, heads, states):
        sl = slice(ci * chunk, (ci + 1) * chunk)
        wss = [_dot(hd["wq"], s.astype(BF16)) for hd, s in zip(heads, states)]
        es = [(hd["u"] - ws[:chunk]).astype(BF16) for hd, ws in zip(heads, wss)]
        new_states = [hd["s_decay"] * s + _dot_tn(hd["k_dec"], e) for hd, s, e in zip(heads, states, es)]
        outs = [ws[chunk:] + _dot(hd["qk"], e) for hd, ws, e in zip(heads, wss, es)]
        for h in range(nh):
            y_ref[sl, h * DK:(h + 1) * DK] = (_rms_norm(outs[h], norm_g)
                                              * _silu(z_ref[sl, h * DK:(h + 1) * DK])).astype(y_ref.dtype)
        return new_states

    n_chunks = tb // chunk
    groups = [list(range(g0, min(g0 + GDN_CHUNK_GROUP, n_chunks))) for g0 in range(0, n_chunks, GDN_CHUNK_GROUP)]
    states = [s_scr[h] for h in range(nh)]
    pre = precompute(groups[0])
    for gi, group in enumerate(groups):
        nxt = precompute(groups[gi + 1]) if gi + 1 < len(groups) else None
        for ci in group:
            states = advance(ci, pre[ci], states)
        pre = nxt
    for h in range(nh):
        s_scr[h] = states[h]

    @pl.when(t == pl.num_programs(1) - 1)
    def _():
        for h in range(nh):
            sout_ref[h] = states[h]


def _gdn(p5, g3, conv0p, conv_w, prm, norm_a, s0, chunk, tb, name):
    _, bx, t, w = p5.shape
    nh = H_A

    def pspec(group):
        return pl.BlockSpec((None, None, tb, w), lambda b, i, g=group: (g, b, i, 0))

    def c0spec(group):
        return pl.BlockSpec((None, SUBLANES, w), lambda b, i, g=group: (b, 0, g))

    def cwspec(group):
        return pl.BlockSpec((CONV_W, w), lambda b, i, g=group: (0, g))

    return pl.pallas_call(
        functools.partial(_gdn_kernel, chunk=chunk, tb=tb),
        grid=(bx, t // tb),
        in_specs=[pspec(0), pspec(1), pspec(2), pspec(3),
                  pl.BlockSpec((None, tb, LANES), lambda b, i: (b, i, 0)),
                  c0spec(0), c0spec(1), c0spec(2),
                  cwspec(0), cwspec(1), cwspec(2),
                  pl.BlockSpec((SUBLANES, LANES), lambda b, i: (0, 0)),
                  pl.BlockSpec((1, DK), lambda b, i: (0, 0)),
                  pl.BlockSpec((None, nh, DK, DK), lambda b, i: (b, 0, 0, 0))],
        out_specs=[pl.BlockSpec((None, tb, w), lambda b, i: (b, i, 0)),
                   pl.BlockSpec((None, nh, DK, DK), lambda b, i: (b, 0, 0, 0))],
        out_shape=[jax.ShapeDtypeStruct((bx, t, w), BF16),
                   jax.ShapeDtypeStruct((bx, nh, DK, DK), F32)],
        scratch_shapes=[pltpu.VMEM((nh, DK, DK), F32),
                        pltpu.VMEM((SUBLANES, w), F32),
                        pltpu.VMEM((SUBLANES, w), F32),
                        pltpu.VMEM((SUBLANES, w), F32)],
        compiler_params=_cparams(("parallel", "arbitrary")),
        name=name,
    )(p5, p5, p5, p5, g3, conv0p, conv0p, conv0p, conv_w, conv_w, conv_w, prm, norm_a, s0)


def _mlstm_kernel(q_ref, k_ref, v_ref, o_ref, z_ref, gt_ref, prm_ref, norm_ref, c0_ref, n0_ref, m0_ref,
                  y_ref, cout_ref, nout_ref, mout_ref, cn_scr, m_scr, *, chunk, tb):
    t = pl.program_id(1)
    nh = H_B
    heads = range(nh)
    ig_lane = 2 * H_A
    lf_lane = 2 * H_A + H_B

    @pl.when(t == 0)
    def _():
        cn_scr[:, :, :DK] = c0_ref[...]
        cn_scr[:, :, DK:] = n0_ref[...]
        m_scr[...] = m0_ref[...]

    gt = gt_ref[...]
    prm = prm_ref[...]
    ig_all = gt + prm[2:3, :]
    lf_all = -_softplus(-(gt + prm[3:4, :]))

    r, c = _iota2((chunk, chunk))
    incl = r >= c
    a_incl = jnp.where(incl, 1.0, 0.0).astype(BF16)
    a_incl_t = jnp.where(r <= c, 1.0, 0.0).astype(BF16)
    eye = jnp.where(r == c, 1.0, 0.0).astype(BF16)
    ones_w = jnp.ones((chunk, DK), F32)
    ones_b = jnp.ones((chunk, DK), BF16)
    norm_g = norm_ref[...]

    def precompute(cis):
        items = []
        for ci in cis:
            sl = slice(ci * chunk, (ci + 1) * chunk)
            lf_c = lf_all[sl]
            ig_c = ig_all[sl]
            b_cols = _dot_mask_lhs(a_incl, lf_c)
            b_rows = _dot_tn_split(lf_c, a_incl_t)
            ig_rows = _dot_tn_split(ig_c, eye)
            for h in heads:
                items.append(dict(ci=ci, sl=sl, h=h, bcol=b_cols[:, lf_lane + h:lf_lane + h + 1],
                                  brow=b_rows[lf_lane + h:lf_lane + h + 1, :],
                                  igrow=ig_rows[ig_lane + h:ig_lane + h + 1, :],
                                  igcol=ig_c[:, ig_lane + h:ig_lane + h + 1]))
        d_logs = [jnp.where(incl, it["bcol"] - it["brow"] + it["igrow"], -jnp.inf) for it in items]
        m_intras = [jnp.max(d, axis=-1, keepdims=True) for d in d_logs]
        qbs = [q_ref[it["sl"], it["h"] * DK:(it["h"] + 1) * DK].astype(BF16) for it in items]
        kcs = [k_ref[it["sl"], it["h"] * DK:(it["h"] + 1) * DK] * (DK ** -0.5) for it in items]
        v1s = [jnp.concatenate([v_ref[it["sl"], it["h"] * DK:(it["h"] + 1) * DK].astype(BF16), ones_b], axis=1)
               for it in items]
        qks = [_dot_nt(qb, kc.astype(BF16)) for qb, kc in zip(qbs, kcs)]
        pq0s = [(jnp.exp(d - mi) * qk).astype(BF16) for d, mi, qk in zip(d_logs, m_intras, qks)]
        avs = [_dot(pq0, v1) for pq0, v1 in zip(pq0s, v1s)]
        out = {ci: [] for ci in cis}
        for it, mi, qb, kc, v1, av in zip(items, m_intras, qbs, kcs, v1s, avs):
            bcol = it["bcol"]
            out[it["ci"]].append(dict(
                b_w=bcol * ones_w, mi_w=mi * ones_w,
                tail_w=(bcol[chunk - 1:chunk, :] - bcol + it["igcol"]) * ones_w,
                qb=qb, kc=kc, v1=v1, av=av))
        return out

    def advance(ci, pre, cns, ms):
        sl = slice(ci * chunk, (ci + 1) * chunk)
        qcns = [_dot(pre[h]["qb"], cns[h].astype(BF16)) for h in heads]
        bms = [pre[h]["b_w"] + ms[h] for h in heads]
        m_ts = [jnp.maximum(bms[h], pre[h]["mi_w"]) for h in heads]
        corrs = [jnp.exp(pre[h]["mi_w"] - m_ts[h]) for h in heads]
        inters = [jnp.exp(bms[h] - m_ts[h]) for h in heads]
        m_lasts = [m_t[chunk - 1:chunk, :] for m_t in m_ts]
        i_lasts = [inter[chunk - 1:chunk, :] for inter in inters]
        kps = [(pre[h]["kc"] * jnp.exp(pre[h]["tail_w"] - m_lasts[h])).astype(BF16) for h in heads]
        new_cns = [jnp.concatenate([i_lasts[h], i_lasts[h]], axis=1) * cns[h] + _dot_tn(kps[h], pre[h]["v1"])
                   for h in heads]
        for h in heads:
            num = inters[h] * qcns[h][:, :DK] + corrs[h] * pre[h]["av"][:, :DK]
            den = inters[h] * qcns[h][:, DK:] + corrs[h] * pre[h]["av"][:, DK:]
            hh = num / jnp.maximum(jnp.abs(den), jnp.exp(-m_ts[h]))
            hh = _sigmoid(o_ref[sl, h * DK:(h + 1) * DK]) * hh
            y_ref[sl, h * DK:(h + 1) * DK] = (_rms_norm(hh, norm_g[:, h * DK:(h + 1) * DK])
                                              * _silu(z_ref[sl, h * DK:(h + 1) * DK])).astype(y_ref.dtype)
        return new_cns, m_lasts

    n_chunks = tb // chunk
    cns = [cn_scr[h] for h in heads]
    ms = [m_scr[h] for h in heads]
    groups = [list(range(g0, min(g0 + MLSTM_CHUNK_GROUP, n_chunks)))
              for g0 in range(0, n_chunks, MLSTM_CHUNK_GROUP)]
    pre = precompute(groups[0])
    for gi, group in enumerate(groups):
        nxt = precompute(groups[gi + 1]) if gi + 1 < len(groups) else None
        for ci in group:
            cns, ms = advance(ci, pre[ci], cns, ms)
        pre = nxt
    for h in heads:
        cn_scr[h] = cns[h]
        m_scr[h] = ms[h]

    @pl.when(t == pl.num_programs(1) - 1)
    def _():
        for h in heads:
            cout_ref[h] = cns[h][:, :DK]
            nout_ref[h] = cns[h][:, DK:]
            mout_ref[h] = ms[h]


def _mlstm(p5, g3, prm, norm_b, c0, n0w, m0w, chunk, tb, name):
    _, bx, t, w = p5.shape
    nh = H_B

    def pspec(group):
        return pl.BlockSpec((None, None, tb, w), lambda b, i, g=group: (g, b, i, 0))

    mat = pl.BlockSpec((None, nh, DK, DK), lambda b, i: (b, 0, 0, 0))
    row = pl.BlockSpec((None, nh, 1, DK), lambda b, i: (b, 0, 0, 0))
    return pl.pallas_call(
        functools.partial(_mlstm_kernel, chunk=chunk, tb=tb),
        grid=(bx, t // tb),
        in_specs=[pspec(4), pspec(5), pspec(6), pspec(7), pspec(8),
                  pl.BlockSpec((None, tb, LANES), lambda b, i: (b, i, 0)),
                  pl.BlockSpec((SUBLANES, LANES), lambda b, i: (0, 0)),
                  pl.BlockSpec((1, w), lambda b, i: (0, 0)),
                  mat, mat, row],
        out_specs=[pl.BlockSpec((None, tb, w), lambda b, i: (b, i, 0)), mat, mat, row],
        out_shape=[jax.ShapeDtypeStruct((bx, t, w), BF16),
                   jax.ShapeDtypeStruct((bx, nh, DK, DK), F32),
                   jax.ShapeDtypeStruct((bx, nh, DK, DK), F32),
                   jax.ShapeDtypeStruct((bx, nh, 1, DK), F32)],
        scratch_shapes=[pltpu.VMEM((nh, DK, 2 * DK), F32), pltpu.VMEM((nh, 1, DK), F32)],
        compiler_params=_cparams(("parallel", "arbitrary")),
        name=name,
    )(p5, p5, p5, p5, p5, g3, prm, norm_b, c0, n0w, m0w)


def _sb_tiles(qbs, kts, vts, runs, u_neg, causal):
    tk = kts[0].shape[0]
    zs = [_dot_nt(qb, kt.astype(BF16)) for qb, kt in zip(qbs, kts)]
    sps = [jnp.maximum(z, 0.0) + jnp.log(1.0 + jnp.exp(-jnp.abs(z))) for z in zs]
    if causal is not None:
        sps = [jnp.where(causal, sp, 0.0) for sp in sps]
    his = [sp.astype(BF16) for sp in sps]
    los = [(sp - hi.astype(F32)).astype(BF16) for sp, hi in zip(sps, his)]
    exts = [_dot(jnp.concatenate([hi, lo], axis=1), u_neg) for hi, lo in zip(his, los)]
    atts = [jnp.exp(z + ext[:, :tk] + run[:, :tk]) for z, ext, run in zip(zs, exts, runs)]
    if causal is not None:
        atts = [jnp.where(causal, att, 0.0) for att in atts]
    contribs = [_dot(att.astype(BF16), vt.astype(BF16)) for att, vt in zip(atts, vts)]
    return [run + ext[:, tk:] for run, ext in zip(runs, exts)], contribs


def _sb_kernel(q_ref, z_ref, kc_ref, vc_ref, k1_ref, v1_ref, k2_ref, v2_ref, kany_ref, vany_ref,
               y_ref, ko_ref, vo_ref, acc_scr, run_scr, q_scr, kbuf, vbuf, flag, sem,
               *, tq, tk, prev_blocks_per_step, prev_blocks_fixed, prev_token_rows, kgroup, vgroup):
    nh = H_C
    b = pl.program_id(0)
    i = pl.program_id(1)
    r, c = _iota2((tq, tq))
    causal = c < r
    def suffix_sum_matrix(n):
        rr, cc = _iota2((2 * n, n + LANES))
        rr = jnp.where(rr >= n, rr - n, rr)
        return jnp.where((rr >= cc) | (cc >= n), -1.0, 0.0).astype(BF16)

    u_diag = suffix_sum_matrix(tq)
    u_prev = suffix_sum_matrix(tk)
    q_scr[...] = (q_ref[...] * (DK ** -0.5)).astype(BF16)

    def q_head(h):
        return q_scr[:, h * DK:(h + 1) * DK]

    def set_flag():
        best = run_scr[0]
        for h in range(1, nh):
            best = jnp.maximum(best, run_scr[h])
        flag[0] = (jnp.max(best) > SB_DEAD_LOG_WEIGHT).astype(jnp.int32)

    heads = range(nh)
    kts = [kc_ref[:, h * DK:(h + 1) * DK] for h in heads]
    vts = [vc_ref[:, h * DK:(h + 1) * DK] for h in heads]
    for h in heads:
        ko_ref[pl.ds(h, tq, stride=nh), :] = kts[h]
        vo_ref[pl.ds(h, tq, stride=nh), :] = vts[h]
    runs, contribs = _sb_tiles([q_head(h) for h in heads], kts, vts, [jnp.zeros((tq, LANES), F32)] * nh, u_diag,
                               causal)
    for h in heads:
        run_scr[h] = runs[h]
        acc_scr[:, h * DK:(h + 1) * DK] = contribs[h]
    flag[0] = 1

    def prev_tile(ref, h):
        if prev_token_rows:
            return ref[pl.ds(h, tk, stride=nh), :]
        return ref[:, h * DK:(h + 1) * DK]

    def stage(k_ref, v_ref):
        runs, contribs = _sb_tiles([q_head(h) for h in heads], [prev_tile(k_ref, h) for h in heads],
                                   [prev_tile(v_ref, h) for h in heads], [run_scr[h] for h in heads], u_prev, None)
        for h in heads:
            run_scr[h] = runs[h]
            acc_scr[:, h * DK:(h + 1) * DK] += contribs[h]
        set_flag()

    nprev = i * prev_blocks_per_step + prev_blocks_fixed

    @pl.when(jnp.logical_and(nprev >= 1, flag[0] > 0))
    def _():
        stage(k1_ref, v1_ref)

    @pl.when(jnp.logical_and(nprev >= 2, flag[0] > 0))
    def _():
        stage(k2_ref, v2_ref)

    def far_copy(any_ref, group, buf, j, slot):
        start = (nprev - j) * tk
        if prev_token_rows:
            src = any_ref.at[b, pl.ds(pl.multiple_of(start * nh, tk * nh), tk * nh), :]
        else:
            src = any_ref.at[group, b, pl.ds(pl.multiple_of(start, tk), tk), :]
        return pltpu.make_async_copy(src, buf, sem.at[slot])

    def cond(carry):
        j, go = carry
        return jnp.logical_and(j <= nprev, go > 0)

    def body(carry):
        j, _ = carry
        kcopy = far_copy(kany_ref, kgroup, kbuf, j, 0)
        vcopy = far_copy(vany_ref, vgroup, vbuf, j, 1)
        kcopy.start()
        vcopy.start()
        kcopy.wait()
        vcopy.wait()
        stage(kbuf, vbuf)
        return j + 1, flag[0]

    lax.while_loop(cond, body, (jnp.int32(3), flag[0]))
    y_ref[...] = (acc_scr[...] * _silu(z_ref[...])).astype(y_ref.dtype)


def _sb(p5, kprev, vprev, tq, causal_prev, name):
    _, bx, t, w = p5.shape
    nh = H_C
    tk = SB_TILE if not causal_prev else tq

    def blk(group):
        return pl.BlockSpec((None, None, tq, w), lambda b, i, g=group: (g, b, i, 0))

    if causal_prev:
        def near(group, back):
            return pl.BlockSpec((None, None, tk, w), lambda b, i, g=group: (g, b, jnp.maximum(i - back, 0), 0))
        kprev, vprev = p5, p5
        near_specs = [near(1, 1), near(2, 1), near(1, 2), near(2, 2)]
        buf_shape = (tk, w)
        per_step, fixed = tq // tk, 0
    else:
        nblk = kprev.shape[1] // (tk * nh)

        def near(back):
            return pl.BlockSpec((None, tk * nh, DK), lambda b, i: (b, max(nblk - back, 0), 0))
        near_specs = [near(1), near(1), near(2), near(2)]
        buf_shape = (tk * nh, DK)
        per_step, fixed = 0, nblk
    kern = functools.partial(_sb_kernel, tq=tq, tk=tk, prev_blocks_per_step=per_step, prev_blocks_fixed=fixed,
                             prev_token_rows=not causal_prev, kgroup=1, vgroup=2)
    rows_out = pl.BlockSpec((None, tq * nh, DK), lambda b, i: (b, i, 0))
    return pl.pallas_call(
        kern,
        grid=(bx, t // tq),
        in_specs=[blk(0), blk(3), blk(1), blk(2)] + near_specs
                 + [pl.BlockSpec(memory_space=pl.ANY), pl.BlockSpec(memory_space=pl.ANY)],
        out_specs=[pl.BlockSpec((None, tq, w), lambda b, i: (b, i, 0)), rows_out, rows_out],
        out_shape=[jax.ShapeDtypeStruct((bx, t, w), BF16),
                   jax.ShapeDtypeStruct((bx, t * nh, DK), F32),
                   jax.ShapeDtypeStruct((bx, t * nh, DK), F32)],
        scratch_shapes=[pltpu.VMEM((tq, w), F32), pltpu.VMEM((nh, tq, LANES), F32), pltpu.VMEM((tq, w), BF16),
                        pltpu.VMEM(buf_shape, F32), pltpu.VMEM(buf_shape, F32),
                        pltpu.SMEM((1,), jnp.int32), pltpu.SemaphoreType.DMA((2,))],
        compiler_params=_cparams(("parallel", "arbitrary")),
        name=name,
    )(p5, p5, p5, p5, kprev, vprev, kprev, vprev, kprev, vprev)


def _s5_kernel(u_ref, z_ref, bb_ref, cc_ref, ar_ref, ai_ref, d_ref, wg_ref, bg_ref, h0r_ref, h0i_ref,
               y_ref, hr_out, hi_out, hs, ys, hr_scr, hi_scr, *, tt):
    t = pl.program_id(1)
    half = GB_D * P_D
    nl = half // LANES

    @pl.when(t == 0)
    def _():
        hr_scr[...] = h0r_ref[...]
        hi_scr[...] = h0i_ref[...]

    for gb in range(NGB_D):
        rows = pl.ds(gb, tt, stride=NGB_D)
        bu = _dot(u_ref[:, gb * LANES:(gb + 1) * LANES].astype(BF16), bb_ref[gb])
        for j in range(2 * nl):
            hs[j, rows, :] = bu[:, j * LANES:(j + 1) * LANES]

    ar = [ar_ref[:, j * LANES:(j + 1) * LANES] for j in range(nl)]
    ai = [ai_ref[:, j * LANES:(j + 1) * LANES] for j in range(nl)]

    def step(s, carry):
        rows = pl.ds(pl.multiple_of(s * NGB_D, NGB_D), NGB_D)
        new = [None] * (2 * nl)
        for j in range(nl):
            hr, hi = carry[j], carry[nl + j]
            new[j] = ar[j] * hr - ai[j] * hi + hs[j, rows, :]
            new[nl + j] = ar[j] * hi + ai[j] * hr + hs[nl + j, rows, :]
        for j in range(2 * nl):
            hs[j, rows, :] = new[j]
        return tuple(new)

    init = tuple(hr_scr[:, j * LANES:(j + 1) * LANES] for j in range(nl)) + tuple(
        hi_scr[:, j * LANES:(j + 1) * LANES] for j in range(nl))
    fin = lax.fori_loop(0, tt, step, init, unroll=8)
    hr = jnp.concatenate(fin[:nl], axis=1)
    hi = jnp.concatenate(fin[nl:], axis=1)
    hr_scr[...] = hr
    hi_scr[...] = hi

    for gb in range(NGB_D):
        rows = pl.ds(gb, tt, stride=NGB_D)
        hcat = jnp.concatenate([hs[j, rows, :] for j in range(2 * nl)], axis=1)
        lanes = slice(gb * LANES, (gb + 1) * LANES)
        ys[:, lanes] = _dot(hcat.astype(BF16), cc_ref[gb]) + d_ref[:, lanes] * u_ref[:, lanes]

    y = ys[...]
    yg = 0.5 * y * (1.0 + jnp.tanh(0.7978845608028654 * (y + 0.044715 * (y * y * y))))
    gate = _sigmoid(_dot(yg.astype(BF16), wg_ref[...]) + bg_ref[...])
    y_ref[...] = (yg * gate * _silu(z_ref[...])).astype(y_ref.dtype)

    @pl.when(t == pl.num_programs(1) - 1)
    def _():
        hr_out[...] = hr
        hi_out[...] = hi


def _s5(p5, ugroup, zgroup, bb, cc, a_re, a_im, d, w_glu_bf16, b_glu, h0r, h0i, tt, name):
    _, bx, t, w = p5.shape
    half = GB_D * P_D
    st = pl.BlockSpec((None, NGB_D, half), lambda b, i: (b, 0, 0))
    prm = pl.BlockSpec((NGB_D, half), lambda b, i: (0, 0))
    row = pl.BlockSpec((1, w), lambda b, i: (0, 0))
    return pl.pallas_call(
        functools.partial(_s5_kernel, tt=tt),
        grid=(bx, t // tt),
        in_specs=[pl.BlockSpec((None, None, tt, w), lambda b, i: (ugroup, b, i, 0)),
                  pl.BlockSpec((None, None, tt, w), lambda b, i: (zgroup, b, i, 0)),
                  pl.BlockSpec((NGB_D, LANES, 2 * half), lambda b, i: (0, 0, 0)),
                  pl.BlockSpec((NGB_D, 2 * half, LANES), lambda b, i: (0, 0, 0)),
                  prm, prm, row,
                  pl.BlockSpec((w, w), lambda b, i: (0, 0)),
                  row, st, st],
        out_specs=[pl.BlockSpec((None, tt, w), lambda b, i: (b, i, 0)), st, st],
        out_shape=[jax.ShapeDtypeStruct((bx, t, w), BF16),
                   jax.ShapeDtypeStruct((bx, NGB_D, half), F32),
                   jax.ShapeDtypeStruct((bx, NGB_D, half), F32)],
        scratch_shapes=[pltpu.VMEM((2 * half // LANES, tt * NGB_D, LANES), F32),
                        pltpu.VMEM((tt, w), F32),
                        pltpu.VMEM((NGB_D, half), F32), pltpu.VMEM((NGB_D, half), F32)],
        compiler_params=_cparams(("parallel", "arbitrary")),
        name=name,
    )(p5, p5, bb, cc, a_re, a_im, d.reshape(1, w), w_glu_bf16, b_glu.reshape(1, w), h0r, h0i)


def _pack_even_weights(w_in, conv_w, a_log, dt_bias, norm_a, ig_bias, fg_bias, norm_b, w_out, ln_g, ln_b):
    d = w_in.shape[0]
    wa = H_A * DK
    wb = H_B * DK
    o = 0
    qkv_a = w_in[:, o:o + 3 * wa]; o += 3 * wa
    z_a = w_in[:, o:o + wa]; o += wa
    gate_a = w_in[:, o:o + 2 * H_A]; o += 2 * H_A
    main_b = w_in[:, o:o + 5 * wb]; o += 5 * wb
    gate_b = w_in[:, o:o + 2 * H_B]
    w_main = jnp.concatenate([qkv_a, z_a, main_b], axis=1).astype(BF16)
    n_gate = 2 * H_A + 2 * H_B
    w_gate = jnp.concatenate([gate_a, gate_b, jnp.zeros((d, LANES - n_gate), F32)], axis=1).astype(BF16)
    prm = jnp.zeros((SUBLANES, LANES), F32)
    prm = prm.at[0, H_A:2 * H_A].set(a_log)
    prm = prm.at[1, H_A:2 * H_A].set(dt_bias)
    prm = prm.at[2, 2 * H_A:2 * H_A + H_B].set(ig_bias)
    prm = prm.at[3, 2 * H_A + H_B:2 * H_A + 2 * H_B].set(fg_bias)
    return dict(w_main=w_main, w_gate=w_gate, conv_w=conv_w, prm=prm, norm_a=norm_a.reshape(1, DK),
                norm_b=norm_b.reshape(1, wb), w_out=w_out.astype(BF16), ln_g=ln_g, ln_b=ln_b)


def _tile(n, preferred):
    tile = min(n, preferred)
    assert n % tile == 0, (n, preferred)
    return tile


def _even_layer(x, s0, conv0, c0, n0, m0, wp, alpha):
    bx, t, d = x.shape
    m = bx * t
    chunk = min(CHUNK, t)
    tm = _tile(m, PROJ_ROWS)
    tb = _tile(t, MIXER_ROWS)
    x2 = x.reshape(m, d)
    p3, g2 = _proj(x2, wp["w_main"], wp["w_gate"], tm, "ev_proj")
    p5 = p3.reshape(9, bx, t, W_HEADS)
    g3 = g2.reshape(bx, t, LANES)
    conv0p = jnp.concatenate([jnp.zeros((bx, SUBLANES - (CONV_W - 1), conv0.shape[2]), F32), conv0], axis=1)
    ya, s_new = _gdn(p5, g3, conv0p, wp["conv_w"], wp["prm"], wp["norm_a"], s0, chunk, tb, "gdn")
    m0w = jnp.broadcast_to(m0[:, :, None, None], (bx, H_B, 1, DK))
    n0w = jnp.broadcast_to(n0[:, :, :, None], (bx, H_B, DK, DK))
    yb, c_new, n_new, m_new = _mlstm(p5, g3, wp["prm"], wp["norm_b"], c0, n0w, m0w, chunk, tb, "mlstm")
    xn = _outproj_ln(ya.reshape(m, -1), yb.reshape(m, -1), x2, wp["w_out"], wp["ln_g"], wp["ln_b"], alpha,
                     _tile(m, OUTPROJ_ROWS), "ev_outproj_ln")
    conv_new = jnp.concatenate([p5[0, :, t - (CONV_W - 1):], p5[1, :, t - (CONV_W - 1):], p5[2, :, t - (CONV_W - 1):]],
                               axis=-1)
    return xn.reshape(bx, t, d), s_new, conv_new, c_new, n_new[:, :, :, 0], m_new[:, :, 0, 0]


def _pack_odd_weights(w_in, lam_re, lam_im, b_re, b_im, c_re, c_im, d, log_dt, w_glu, b_glu, w_out, ln_g, ln_b):
    dm = w_in.shape[0]
    w_main = w_in.astype(BF16)
    dt = jnp.exp(log_dt)[:, None]
    mag = jnp.exp(lam_re * dt)
    a_re, a_im = mag * jnp.cos(lam_im * dt), mag * jnp.sin(lam_im * dt)
    den = lam_re * lam_re + lam_im * lam_im
    f_re = ((a_re - 1.0) * lam_re + a_im * lam_im) / den
    f_im = (a_im * lam_re - (a_re - 1.0) * lam_im) / den
    bb_re = f_re[..., None] * b_re - f_im[..., None] * b_im
    bb_im = f_re[..., None] * b_im + f_im[..., None] * b_re
    half = GB_D * P_D
    eye = jnp.eye(GB_D, dtype=F32)

    def blockdiag_in(bb):
        bbg = bb.reshape(NGB_D, GB_D, P_D, GC_D)
        return jnp.einsum("bgpc,gh->bgchp", bbg, eye).reshape(NGB_D, GB_D * GC_D, half)

    def blockdiag_out(cm):
        cg = cm.reshape(NGB_D, GB_D, GC_D, P_D)
        return jnp.einsum("bgcp,gh->bgphc", cg, eye).reshape(NGB_D, half, GB_D * GC_D)

    bb = jnp.concatenate([blockdiag_in(bb_re), blockdiag_in(bb_im)], axis=2).astype(BF16)
    cc = jnp.concatenate([blockdiag_out(c_re), -blockdiag_out(c_im)], axis=1).astype(BF16)
    return dict(w_main=w_main, bb=bb, cc=cc, a_re=a_re.reshape(NGB_D, half), a_im=a_im.reshape(NGB_D, half),
                d=d, w_glu=w_glu.astype(BF16), b_glu=b_glu, w_out=w_out.astype(BF16), ln_g=ln_g, ln_b=ln_b)


def _odd_layer(x, k_cache, v_cache, h_re0, h_im0, wp, alpha):
    bx, t, d = x.shape
    m = bx * t
    tm = _tile(m, PROJ_ROWS)
    tq = _tile(t, SB_TILE)
    tt = _tile(t, S5_ROWS)
    x2 = x.reshape(m, d)
    p5 = _proj(x2, wp["w_main"], None, tm, "od_proj").reshape(6, bx, t, W_HEADS)
    if k_cache is None:
        yc, kk, vv = _sb(p5, None, None, tq, True, "sb_prompt")
    else:
        past = k_cache.shape[1]
        yc, kk, vv = _sb(p5, k_cache.reshape(bx, past * H_C, DK), v_cache.reshape(bx, past * H_C, DK), tq, False,
                         "sb_sample")
    half = GB_D * P_D
    yd, hr, hi = _s5(p5, 4, 5, wp["bb"], wp["cc"], wp["a_re"], wp["a_im"], wp["d"], wp["w_glu"], wp["b_glu"],
                     h_re0.reshape(bx, NGB_D, half), h_im0.reshape(bx, NGB_D, half), tt, "s5_glu")
    xn = _outproj_ln(yc.reshape(m, -1), yd.reshape(m, -1), x2, wp["w_out"], wp["ln_g"], wp["ln_b"], alpha,
                     _tile(m, OUTPROJ_ROWS), "od_outproj_ln")
    return (xn.reshape(bx, t, d), kk.reshape(bx, t, H_C, DK), vv.reshape(bx, t, H_C, DK),
            hr.reshape(bx, G_D, P_D), hi.reshape(bx, G_D, P_D))


def kernel(x_prompt, x_sample, state_delta_S, state_delta_conv, state_mlstm_C, state_mlstm_n, state_mlstm_m, cache_sb_k, cache_sb_v, state_s5_re, state_s5_im, ev_w_in, ev_conv_w, ev_a_log, ev_dt_bias, ev_norm_a, ev_ig_bias, ev_fg_bias, ev_norm_b, ev_w_out, ev_ln_g, ev_ln_b, od_w_in, od_lam_re, od_lam_im, od_b_re, od_b_im, od_c_re, od_c_im, od_d, od_log_dt, od_w_glu, od_b_glu, od_w_out, od_ln_g, od_ln_b):
    n_even = ev_w_in.shape[0]
    n_odd = od_w_in.shape[0]
    depth = n_even + n_odd
    alpha = (2 * depth) ** 0.25
    nb, seq, _ = x_prompt.shape
    ns, dseq, _ = x_sample.shape
    xp, xs = x_prompt, x_sample
    outs = [[] for _ in range(18)]
    for layer in range(depth):
        j = layer // 2
        if layer % 2 == 0:
            wp = _pack_even_weights(ev_w_in[j], ev_conv_w[j], ev_a_log[j], ev_dt_bias[j], ev_norm_a[j], ev_ig_bias[j],
                                    ev_fg_bias[j], ev_norm_b[j], ev_w_out[j], ev_ln_g[j], ev_ln_b[j])
            zs = jnp.zeros((nb, H_A, DK, DK), F32)
            xp, s, cv, c, n, m = _even_layer(
                xp, zs, jnp.zeros((nb, CONV_W - 1, 3 * H_A * DK), F32), zs, jnp.zeros((nb, H_B, DK), F32),
                jnp.zeros((nb, H_B), F32), wp, alpha)
            for idx, val in zip((0, 2, 4, 6, 8), (s, cv, c, n, m)):
                outs[idx].append(val)
            xs, s, cv, c, n, m = _even_layer(
                xs, state_delta_S[j], state_delta_conv[j], state_mlstm_C[j], state_mlstm_n[j], state_mlstm_m[j],
                wp, alpha)
            for idx, val in zip((1, 3, 5, 7, 9), (s, cv, c, n, m)):
                outs[idx].append(val)
        else:
            wp = _pack_odd_weights(od_w_in[j], od_lam_re[j], od_lam_im[j], od_b_re[j], od_b_im[j], od_c_re[j],
                                   od_c_im[j], od_d[j], od_log_dt[j], od_w_glu[j], od_b_glu[j], od_w_out[j],
                                   od_ln_g[j], od_ln_b[j])
            zst = jnp.zeros((nb, G_D, P_D), F32)
            xp, k, v, hr, hi = _odd_layer(xp, None, None, zst, zst, wp, alpha)
            for idx, val in zip((10, 12, 14, 16), (k, v, hr, hi)):
                outs[idx].append(val)
            xs, k, v, hr, hi = _odd_layer(xs, cache_sb_k[j], cache_sb_v[j], state_s5_re[j], state_s5_im[j], wp, alpha)
            for idx, val in zip((11, 13, 15, 17), (k, v, hr, hi)):
                outs[idx].append(val)
    return (xp, xs) + tuple(jnp.stack(o) for o in outs)
```

```python
import functools

import jax
import jax.numpy as jnp
from jax import lax
from jax.experimental import pallas as pl
from jax.experimental.pallas import tpu as pltpu

F32 = jnp.float32
BF16 = jnp.bfloat16

LANES = 128
SUBLANES = 8
CHUNK = 64
CONV_W = 4
H_A = 8
H_B = 8
H_C = 8
DK = 128
GC_D = 16
G_D = 64
P_D = 64
GB_D = 8
NGB_D = G_D // GB_D
W_HEADS = 1024
LN_EPS = 1e-5
NORM_EPS = 1e-6
VMEM_LIMIT = 48 * 1024 * 1024
SB_DEAD_LOG_WEIGHT = -88.0
SB_TILE = 128
PROJ_ROWS = 1024
OUTPROJ_ROWS = 512
MIXER_ROWS = 512
S5_ROWS = 512
MLSTM_CHUNK_GROUP = 1
GDN_CHUNK_GROUP = 4


def _cparams(sem):
    return pltpu.CompilerParams(dimension_semantics=sem, vmem_limit_bytes=VMEM_LIMIT)


def _sigmoid(x):
    return 1.0 / (1.0 + jnp.exp(-x))


def _silu(x):
    return x * _sigmoid(x)


def _softplus(x):
    return jnp.maximum(x, 0.0) + jnp.log1p(jnp.exp(-jnp.abs(x)))


def _dot(a, b):
    return jnp.dot(a, b, preferred_element_type=F32)


def _dot_nt(a, b):
    return lax.dot_general(a, b, (((1,), (1,)), ((), ())), preferred_element_type=F32)


def _dot_tn(a, b):
    return lax.dot_general(a, b, (((0,), (0,)), ((), ())), preferred_element_type=F32)


def _split3(x):
    hi = x.astype(BF16)
    r1 = x - hi.astype(F32)
    mid = r1.astype(BF16)
    lo = (r1 - mid.astype(F32)).astype(BF16)
    return hi, mid, lo


def _dot_mask_lhs(mask_bf16, x):
    hi, mid, lo = _split3(x)
    return _dot(mask_bf16, hi) + _dot(mask_bf16, mid) + _dot(mask_bf16, lo)


def _dot_tn_split(x, mask_bf16):
    hi, mid, lo = _split3(x)
    return _dot_tn(hi, mask_bf16) + _dot_tn(mid, mask_bf16) + _dot_tn(lo, mask_bf16)


def _iota2(shape):
    return (lax.broadcasted_iota(jnp.int32, shape, 0), lax.broadcasted_iota(jnp.int32, shape, 1))


def _head(x, h):
    return x[:, h * DK:(h + 1) * DK]


def _proj_kernel(x_ref, w_ref, o_ref):
    o_ref[0] = _dot(x_ref[...].astype(BF16), w_ref[...])


def _proj_gates_kernel(x_ref, w_ref, wg_ref, o_ref, og_ref):
    o_ref[0] = _dot(x_ref[...].astype(BF16), w_ref[...])

    @pl.when(pl.program_id(1) == 0)
    def _():
        og_ref[...] = _dot(x_ref[...].astype(BF16), wg_ref[...])


def _proj(x2, w, w_gate, tm, name):
    m, d = x2.shape
    wn = W_HEADS
    ng = w.shape[1] // wn
    x_spec = pl.BlockSpec((tm, d), lambda i, j: (i, 0))
    w_spec = pl.BlockSpec((d, wn), lambda i, j: (0, j))
    o_spec = pl.BlockSpec((1, tm, wn), lambda i, j: (j, i, 0))
    o_shape = jax.ShapeDtypeStruct((ng, m, wn), F32)
    if w_gate is None:
        return pl.pallas_call(
            _proj_kernel, grid=(m // tm, ng), in_specs=[x_spec, w_spec], out_specs=o_spec, out_shape=o_shape,
            compiler_params=_cparams(("parallel", "arbitrary")), name=name)(x2, w)
    return pl.pallas_call(
        _proj_gates_kernel,
        grid=(m // tm, ng),
        in_specs=[x_spec, w_spec, pl.BlockSpec((d, LANES), lambda i, j: (0, 0))],
        out_specs=[o_spec, pl.BlockSpec((tm, LANES), lambda i, j: (i, 0))],
        out_shape=[o_shape, jax.ShapeDtypeStruct((m, LANES), F32)],
        compiler_params=_cparams(("parallel", "arbitrary")),
        name=name,
    )(x2, w, w_gate)


def _outproj_ln_kernel(ya_ref, yb_ref, x_ref, wa_ref, wb_ref, g_ref, b_ref, o_ref, *, alpha):
    acc = _dot(ya_ref[...].astype(BF16), wa_ref[...]) + _dot(yb_ref[...].astype(BF16), wb_ref[...])
    r = alpha * x_ref[...] + acc
    mu = jnp.mean(r, axis=-1, keepdims=True)
    c = r - mu
    var = jnp.mean(c * c, axis=-1, keepdims=True)
    o_ref[...] = c * lax.rsqrt(var + LN_EPS) * g_ref[...] + b_ref[...]


def _outproj_ln(ya, yb, x2, w_out_bf16, ln_g, ln_b, alpha, tm, name):
    m, d = x2.shape
    wa = ya.shape[1]
    wb = yb.shape[1]
    return pl.pallas_call(
        functools.partial(_outproj_ln_kernel, alpha=alpha),
        grid=(m // tm,),
        in_specs=[pl.BlockSpec((tm, wa), lambda i: (i, 0)),
                  pl.BlockSpec((tm, wb), lambda i: (i, 0)),
                  pl.BlockSpec((tm, d), lambda i: (i, 0)),
                  pl.BlockSpec((wa, d), lambda i: (0, 0)),
                  pl.BlockSpec((wb, d), lambda i: (wa // wb, 0)),
                  pl.BlockSpec((1, d), lambda i: (0, 0)),
                  pl.BlockSpec((1, d), lambda i: (0, 0))],
        out_specs=pl.BlockSpec((tm, d), lambda i: (i, 0)),
        out_shape=jax.ShapeDtypeStruct((m, d), F32),
        compiler_params=_cparams(("parallel",)),
        name=name,
    )(ya, yb, x2, w_out_bf16, w_out_bf16, ln_g.reshape(1, d), ln_b.reshape(1, d))


def _conv_carry_init(carry, c0_ref, w_ref):
    w = w_ref[...]
    xm3, xm2, xm1 = c0_ref[5:6, :], c0_ref[6:7, :], c0_ref[7:8, :]
    carry[0:1, :] = w[0:1, :] * xm1
    carry[1:2, :] = w[1:2, :] * xm1 + w[0:1, :] * xm2
    carry[2:3, :] = w[2:3, :] * xm1 + (w[1:2, :] * xm2 + w[0:1, :] * xm3)


def _conv_silu(x_ref, carry, w_ref, tb):
    x = x_ref[...]
    w = w_ref[...]
    first = lax.broadcasted_iota(jnp.int32, (SUBLANES, x.shape[1]), 0) == 0
    acc = x * w[0:1, :]
    for i in range(1, CONV_W):
        shifted_in = carry[i - 1:i, :]
        carry[i - 1:i, :] = acc[tb - 1:tb, :]
        rolled = pltpu.roll(acc, 1, axis=0)
        head = jnp.where(first, shifted_in, rolled[0:SUBLANES])
        acc = x * w[i:i + 1, :] + jnp.concatenate([head, rolled[SUBLANES:]], axis=0)
    return _silu(acc)


def _l2norm(x):
    return x * lax.rsqrt(jnp.sum(x * x, axis=-1, keepdims=True) + NORM_EPS)


def _rms_norm(x, g):
    return x * lax.rsqrt(jnp.mean(x * x, axis=-1, keepdims=True) + NORM_EPS) * g


def _unit_lower_inverses(n_mats, length):
    r, c = _iota2((length, length))
    eye = jnp.where(r == c, 1.0, 0.0)
    xs = [eye + n for n in n_mats]
    ps = list(n_mats)
    for _ in range(length.bit_length() - 2):
        pbs = [p.astype(BF16) for p in ps]
        ps = [_dot(pb, pb) for pb in pbs]
        xs = [x + _dot(x.astype(BF16), p.astype(BF16)) for x, p in zip(xs, ps)]
    return xs


def _gdn_kernel(qp_ref, kp_ref, vp_ref, z_ref, gt_ref, c0q_ref, c0k_ref, c0v_ref,
                cwq_ref, cwk_ref, cwv_ref, prm_ref, norm_ref, s0_ref,
                y_ref, sout_ref, s_scr, xq_scr, xk_scr, xv_scr, *, chunk, tb):
    t = pl.program_id(1)
    nh = H_A

    @pl.when(t == 0)
    def _():
        s_scr[...] = s0_ref[...]
        _conv_carry_init(xq_scr, c0q_ref, cwq_ref)
        _conv_carry_init(xk_scr, c0k_ref, cwk_ref)
        _conv_carry_init(xv_scr, c0v_ref, cwv_ref)

    q_all = _conv_silu(qp_ref, xq_scr, cwq_ref, tb)
    k_all = _conv_silu(kp_ref, xk_scr, cwk_ref, tb)
    v_all = _conv_silu(vp_ref, xv_scr, cwv_ref, tb)
    q_h = [_l2norm(_head(q_all, h)) * (DK ** -0.5) for h in range(nh)]
    k_h = [_l2norm(_head(k_all, h)) for h in range(nh)]

    gt = gt_ref[...]
    prm = prm_ref[...]
    beta_all = _sigmoid(gt)
    g_all = -jnp.exp(prm[0:1, :]) * _softplus(gt + prm[1:2, :])

    r, c = _iota2((chunk, chunk))
    incl = r >= c
    strict = r > c
    a_incl = jnp.where(incl, 1.0, 0.0).astype(BF16)
    a_incl_t = jnp.where(r <= c, 1.0, 0.0).astype(BF16)
    norm_g = norm_ref[...]

    def precompute(cis):
        items = []
        for ci in cis:
            sl = slice(ci * chunk, (ci + 1) * chunk)
            g_c = g_all[sl]
            gc_cols = _dot_mask_lhs(a_incl, g_c)
            gc_rows = _dot_tn_split(g_c, a_incl_t)
            beta_c = beta_all[sl]
            for h in range(nh):
                items.append(dict(ci=ci, gcol=gc_cols[:, nh + h:nh + h + 1], grow=gc_rows[nh + h:nh + h + 1, :],
                                  bc=beta_c[:, h:h + 1], qc=q_h[h][sl], kc=k_h[h][sl], vc=_head(v_all, h)[sl]))
        qkks = [_dot_nt(jnp.concatenate([it["qc"], it["kc"]], axis=0).astype(BF16), it["kc"].astype(BF16))
                for it in items]
        decays = [jnp.where(incl, jnp.exp(it["gcol"] - it["grow"]), 0.0) for it in items]
        n_mats = [jnp.where(strict, -(it["bc"] * qkk[chunk:] * decay), 0.0)
                  for it, qkk, decay in zip(items, qkks, decays)]
        t_invs = _unit_lower_inverses(n_mats, chunk)
        gams = [jnp.exp(it["gcol"]) for it in items]
        rhss = [jnp.concatenate([it["vc"] * it["bc"], it["kc"] * (it["bc"] * gam)], axis=-1).astype(BF16)
                for it, gam in zip(items, gams)]
        sols = [_dot(ti.astype(BF16), rhs) for ti, rhs in zip(t_invs, rhss)]
        out = {ci: [] for ci in cis}
        for it, qkk, decay, gam, sol in zip(items, qkks, decays, gams, sols):
            g_last = it["gcol"][chunk - 1:chunk, :]
            out[it["ci"]].append(dict(
                u=sol[:, :DK],
                wq=jnp.concatenate([sol[:, DK:], it["qc"] * gam], axis=0).astype(BF16),
                qk=(qkk[:chunk] * decay).astype(BF16),
                k_dec=(it["kc"] * jnp.exp(g_last - it["gcol"])).astype(BF16),
                s_decay=jnp.exp(g_last)))
        return out

    def advance(ci, heads, states):
        sl = slice(ci * chunk, (ci + 1) * chunk)
        wss = [_dot(hd["wq"], s.astype(BF16)) for hd, s in zip(heads, states)]
        es = [(hd["u"] - ws[:chunk]).astype(BF16) for hd, ws in zip(heads, wss)]
        new_states = [hd["s_decay"] * s + _dot_tn(hd["k_dec"], e) for hd, s, e in zip(heads, states, es)]
        outs = [ws[chunk:] + _dot(hd["qk"], e) for hd, ws, e in zip(heads, wss, es)]
        for h in range(nh):
            y_ref[sl, h * DK:(h + 1) * DK] = (_rms_norm(outs[h], norm_g)
                                              * _silu(z_ref[sl, h * DK:(h + 1) * DK])).astype(y_ref.dtype)
        return new_states

    n_chunks = tb // chunk
    groups = [list(range(g0, min(g0 + GDN_CHUNK_GROUP, n_chunks))) for g0 in range(0, n_chunks, GDN_CHUNK_GROUP)]
    states = [s_scr[h] for h in range(nh)]
    pre = precompute(groups[0])
    for gi, group in enumerate(groups):
        nxt = precompute(groups[gi + 1]) if gi + 1 < len(groups) else None
        for ci in group:
            states = advance(ci, pre[ci], states)
        pre = nxt
    for h in range(nh):
        s_scr[h] = states[h]

    @pl.when(t == pl.num_programs(1) - 1)
    def _():
        for h in range(nh):
            sout_ref[h] = states[h]


def _gdn(p5, g3, conv0p, conv_w, prm, norm_a, s0, chunk, tb, name):
    _, bx, t, w = p5.shape
    nh = H_A

    def pspec(group):
        return pl.BlockSpec((None, None, tb, w), lambda b, i, g=group: (g, b, i, 0))

    def c0spec(group):
        return pl.BlockSpec((None, SUBLANES, w), lambda b, i, g=group: (b, 0, g))

    def cwspec(group):
        return pl.BlockSpec((CONV_W, w), lambda b, i, g=group: (0, g))

    return pl.pallas_call(
        functools.partial(_gdn_kernel, chunk=chunk, tb=tb),
        grid=(bx, t // tb),
        in_specs=[pspec(0), pspec(1), pspec(2), pspec(3),
                  pl.BlockSpec((None, tb, LANES), lambda b, i: (b, i, 0)),
                  c0spec(0), c0spec(1), c0spec(2),
                  cwspec(0), cwspec(1), cwspec(2),
                  pl.BlockSpec((SUBLANES, LANES), lambda b, i: (0, 0)),
                  pl.BlockSpec((1, DK), lambda b, i: (0, 0)),
                  pl.BlockSpec((None, nh, DK, DK), lambda b, i: (b, 0, 0, 0))],
        out_specs=[pl.BlockSpec((None, tb, w), lambda b, i: (b, i, 0)),
                   pl.BlockSpec((None, nh, DK, DK), lambda b, i: (b, 0, 0, 0))],
        out_shape=[jax.ShapeDtypeStruct((bx, t, w), BF16),
                   jax.ShapeDtypeStruct((bx, nh, DK, DK), F32)],
        scratch_shapes=[pltpu.VMEM((nh, DK, DK), F32),
                        pltpu.VMEM((SUBLANES, w), F32),
                        pltpu.VMEM((SUBLANES, w), F32),
                        pltpu.VMEM((SUBLANES, w), F32)],
        compiler_params=_cparams(("parallel", "arbitrary")),
        name=name,
    )(p5, p5, p5, p5, g3, conv0p, conv0p, conv0p, conv_w, conv_w, conv_w, prm, norm_a, s0)


def _mlstm_kernel(q_ref, k_ref, v_ref, o_ref, z_ref, gt_ref, prm_ref, norm_ref, c0_ref, n0_ref, m0_ref,
                  y_ref, cout_ref, nout_ref, mout_ref, cn_scr, m_scr, *, chunk, tb):
    t = pl.program_id(1)
    nh = H_B
    heads = range(nh)
    ig_lane = 2 * H_A
    lf_lane = 2 * H_A + H_B

    @pl.when(t == 0)
    def _():
        cn_scr[:, :, :DK] = c0_ref[...]
        cn_scr[:, :, DK:] = n0_ref[...]
        m_scr[...] = m0_ref[...]

    gt = gt_ref[...]
    prm = prm_ref[...]
    ig_all = gt + prm[2:3, :]
    lf_all = -_softplus(-(gt + prm[3:4, :]))

    r, c = _iota2((chunk, chunk))
    incl = r >= c
    a_incl = jnp.where(incl, 1.0, 0.0).astype(BF16)
    a_incl_t = jnp.where(r <= c, 1.0, 0.0).astype(BF16)
    eye = jnp.where(r == c, 1.0, 0.0).astype(BF16)
    ones_w = jnp.ones((chunk, DK), F32)
    ones_b = jnp.ones((chunk, DK), BF16)
    norm_g = norm_ref[...]

    def precompute(cis):
        items = []
        for ci in cis:
            sl = slice(ci * chunk, (ci + 1) * chunk)
            lf_c = lf_all[sl]
            ig_c = ig_all[sl]
            b_cols = _dot_mask_lhs(a_incl, lf_c)
            b_rows = _dot_tn_split(lf_c, a_incl_t)
            ig_rows = _dot_tn_split(ig_c, eye)
            for h in heads:
                items.append(dict(ci=ci, sl=sl, h=h, bcol=b_cols[:, lf_lane + h:lf_lane + h + 1],
                                  brow=b_rows[lf_lane + h:lf_lane + h + 1, :],
                                  igrow=ig_rows[ig_lane + h:ig_lane + h + 1, :],
                                  igcol=ig_c[:, ig_lane + h:ig_lane + h + 1]))
        d_logs = [jnp.where(incl, it["bcol"] - it["brow"] + it["igrow"], -jnp.inf) for it in items]
        m_intras = [jnp.max(d, axis=-1, keepdims=True) for d in d_logs]
        qbs = [q_ref[it["sl"], it["h"] * DK:(it["h"] + 1) * DK].astype(BF16) for it in items]
        kcs = [k_ref[it["sl"], it["h"] * DK:(it["h"] + 1) * DK] * (DK ** -0.5) for it in items]
        v1s = [jnp.concatenate([v_ref[it["sl"], it["h"] * DK:(it["h"] + 1) * DK].astype(BF16), ones_b], axis=1)
               for it in items]
        qks = [_dot_nt(qb, kc.astype(BF16)) for qb, kc in zip(qbs, kcs)]
        pq0s = [(jnp.exp(d - mi) * qk).astype(BF16) for d, mi, qk in zip(d_logs, m_intras, qks)]
        avs = [_dot(pq0, v1) for pq0, v1 in zip(pq0s, v1s)]
        out = {ci: [] for ci in cis}
        for it, mi, qb, kc, v1, av in zip(items, m_intras, qbs, kcs, v1s, avs):
            bcol = it["bcol"]
            out[it["ci"]].append(dict(
                b_w=bcol * ones_w, mi_w=mi * ones_w,
                tail_w=(bcol[chunk - 1:chunk, :] - bcol + it["igcol"]) * ones_w,
                qb=qb, kc=kc, v1=v1, av=av))
        return out

    def advance(ci, pre, cns, ms):
        sl = slice(ci * chunk, (ci + 1) * chunk)
        qcns = [_dot(pre[h]["qb"], cns[h].astype(BF16)) for h in heads]
        bms = [pre[h]["b_w"] + ms[h] for h in heads]
        m_ts = [jnp.maximum(bms[h], pre[h]["mi_w"]) for h in heads]
        corrs = [jnp.exp(pre[h]["mi_w"] - m_ts[h]) for h in heads]
        inters = [jnp.exp(bms[h] - m_ts[h]) for h in heads]
        m_lasts = [m_t[chunk - 1:chunk, :] for m_t in m_ts]
        i_lasts = [inter[chunk - 1:chunk, :] for inter in inters]
        kps = [(pre[h]["kc"] * jnp.exp(pre[h]["tail_w"] - m_lasts[h])).astype(BF16) for h in heads]
        new_cns = [jnp.concatenate([i_lasts[h], i_lasts[h]], axis=1) * cns[h] + _dot_tn(kps[h], pre[h]["v1"])
                   for h in heads]
        for h in heads:
            num = inters[h] * qcns[h][:, :DK] + corrs[h] * pre[h]["av"][:, :DK]
            den = inters[h] * qcns[h][:, DK:] + corrs[h] * pre[h]["av"][:, DK:]
            hh = num / jnp.maximum(jnp.abs(den), jnp.exp(-m_ts[h]))
            hh = _sigmoid(o_ref[sl, h * DK:(h + 1) * DK]) * hh
            y_ref[sl, h * DK:(h + 1) * DK] = (_rms_norm(hh, norm_g[:, h * DK:(h + 1) * DK])
                                              * _silu(z_ref[sl, h * DK:(h + 1) * DK])).astype(y_ref.dtype)
        return new_cns, m_lasts

    n_chunks = tb // chunk
    cns = [cn_scr[h] for h in heads]
    ms = [m_scr[h] for h in heads]
    groups = [list(range(g0, min(g0 + MLSTM_CHUNK_GROUP, n_chunks)))
              for g0 in range(0, n_chunks, MLSTM_CHUNK_GROUP)]
    pre = precompute(groups[0])
    for gi, group in enumerate(groups):
        nxt = precompute(groups[gi + 1]) if gi + 1 < len(groups) else None
        for ci in group:
            cns, ms = advance(ci, pre[ci], cns, ms)
        pre = nxt
    for h in heads:
        cn_scr[h] = cns[h]
        m_scr[h] = ms[h]

    @pl.when(t == pl.num_programs(1) - 1)
    def _():
        for h in heads:
            cout_ref[h] = cns[h][:, :DK]
            nout_ref[h] = cns[h][:, DK:]
            mout_ref[h] = ms[h]


def _mlstm(p5, g3, prm, norm_b, c0, n0w, m0w, chunk, tb, name):
    _, bx, t, w = p5.shape
    nh = H_B

    def pspec(group):
        return pl.BlockSpec((None, None, tb, w), lambda b, i, g=group: (g, b, i, 0))

    mat = pl.BlockSpec((None, nh, DK, DK), lambda b, i: (b, 0, 0, 0))
    row = pl.BlockSpec((None, nh, 1, DK), lambda b, i: (b, 0, 0, 0))
    return pl.pallas_call(
        functools.partial(_mlstm_kernel, chunk=chunk, tb=tb),
        grid=(bx, t // tb),
        in_specs=[pspec(4), pspec(5), pspec(6), pspec(7), pspec(8),
                  pl.BlockSpec((None, tb, LANES), lambda b, i: (b, i, 0)),
                  pl.BlockSpec((SUBLANES, LANES), lambda b, i: (0, 0)),
                  pl.BlockSpec((1, w), lambda b, i: (0, 0)),
                  mat, mat, row],
        out_specs=[pl.BlockSpec((None, tb, w), lambda b, i: (b, i, 0)), mat, mat, row],
        out_shape=[jax.ShapeDtypeStruct((bx, t, w), BF16),
                   jax.ShapeDtypeStruct((bx, nh, DK, DK), F32),
                   jax.ShapeDtypeStruct((bx, nh, DK, DK), F32),
                   jax.ShapeDtypeStruct((bx, nh, 1, DK), F32)],
        scratch_shapes=[pltpu.VMEM((nh, DK, 2 * DK), F32), pltpu.VMEM((nh, 1, DK), F32)],
        compiler_params=_cparams(("parallel", "arbitrary")),
        name=name,
    )(p5, p5, p5, p5, p5, g3, prm, norm_b, c0, n0w, m0w)


def _sb_tiles(qbs, kts, vts, runs, u_neg, causal):
    tk = kts[0].shape[0]
    zs = [_dot_nt(qb, kt.astype(BF16)) for qb, kt in zip(qbs, kts)]
    sps = [jnp.maximum(z, 0.0) + jnp.log(1.0 + jnp.exp(-jnp.abs(z))) for z in zs]
    if causal is not None:
        sps = [jnp.where(causal, sp, 0.0) for sp in sps]
    his = [sp.astype(BF16) for sp in sps]
    los = [(sp - hi.astype(F32)).astype(BF16) for sp, hi in zip(sps, his)]
    exts = [_dot(jnp.concatenate([hi, lo], axis=1), u_neg) for hi, lo in zip(his, los)]
    atts = [jnp.exp(z + ext[:, :tk] + run[:, :tk]) for z, ext, run in zip(zs, exts, runs)]
    if causal is not None:
        atts = [jnp.where(causal, att, 0.0) for att in atts]
    contribs = [_dot(att.astype(BF16), vt.astype(BF16)) for att, vt in zip(atts, vts)]
    return [run + ext[:, tk:] for run, ext in zip(runs, exts)], contribs


def _sb_kernel(q_ref, z_ref, kc_ref, vc_ref, k1_ref, v1_ref, k2_ref, v2_ref, kany_ref, vany_ref,
               y_ref, ko_ref, vo_ref, acc_scr, run_scr, q_scr, kbuf, vbuf, flag, sem,
               *, tq, tk, prev_blocks_per_step, prev_blocks_fixed, prev_token_rows, kgroup, vgroup):
    nh = H_C
    b = pl.program_id(0)
    i = pl.program_id(1)
    r, c = _iota2((tq, tq))
    causal = c < r
    def suffix_sum_matrix(n):
        rr, cc = _iota2((2 * n, n + LANES))
        rr = jnp.where(rr >= n, rr - n, rr)
        return jnp.where((rr >= cc) | (cc >= n), -1.0, 0.0).astype(BF16)

    u_diag = suffix_sum_matrix(tq)
    u_prev = suffix_sum_matrix(tk)
    q_scr[...] = (q_ref[...] * (DK ** -0.5)).astype(BF16)

    def q_head(h):
        return q_scr[:, h * DK:(h + 1) * DK]

    def set_flag():
        best = run_scr[0]
        for h in range(1, nh):
            best = jnp.maximum(best, run_scr[h])
        flag[0] = (jnp.max(best) > SB_DEAD_LOG_WEIGHT).astype(jnp.int32)

    heads = range(nh)
    kts = [kc_ref[:, h * DK:(h + 1) * DK] for h in heads]
    vts = [vc_ref[:, h * DK:(h + 1) * DK] for h in heads]
    for h in heads:
        ko_ref[pl.ds(h, tq, stride=nh), :] = kts[h]
        vo_ref[pl.ds(h, tq, stride=nh), :] = vts[h]
    runs, contribs = _sb_tiles([q_head(h) for h in heads], kts, vts, [jnp.zeros((tq, LANES), F32)] * nh, u_diag,
                               causal)
    for h in heads:
        run_scr[h] = runs[h]
        acc_scr[:, h * DK:(h + 1) * DK] = contribs[h]
    flag[0] = 1

    def prev_tile(ref, h):
        if prev_token_rows:
            return ref[pl.ds(h, tk, stride=nh), :]
        return ref[:, h * DK:(h + 1) * DK]

    def stage(k_ref, v_ref):
        runs, contribs = _sb_tiles([q_head(h) for h in heads], [prev_tile(k_ref, h) for h in heads],
                                   [prev_tile(v_ref, h) for h in heads], [run_scr[h] for h in heads], u_prev, None)
        for h in heads:
            run_scr[h] = runs[h]
            acc_scr[:, h * DK:(h + 1) * DK] += contribs[h]
        set_flag()

    nprev = i * prev_blocks_per_step + prev_blocks_fixed

    @pl.when(jnp.logical_and(nprev >= 1, flag[0] > 0))
    def _():
        stage(k1_ref, v1_ref)

    @pl.when(jnp.logical_and(nprev >= 2, flag[0] > 0))
    def _():
        stage(k2_ref, v2_ref)

    def far_copy(any_ref, group, buf, j, slot):
        start = (nprev - j) * tk
        if prev_token_rows:
            src = any_ref.at[b, pl.ds(pl.multiple_of(start * nh, tk * nh), tk * nh), :]
        else:
            src = any_ref.at[group, b, pl.ds(pl.multiple_of(start, tk), tk), :]
        return pltpu.make_async_copy(src, buf, sem.at[slot])

    def cond(carry):
        j, go = carry
        return jnp.logical_and(j <= nprev, go > 0)

    def body(carry):
        j, _ = carry
        kcopy = far_copy(kany_ref, kgroup, kbuf, j, 0)
        vcopy = far_copy(vany_ref, vgroup, vbuf, j, 1)
        kcopy.start()
        vcopy.start()
        kcopy.wait()
        vcopy.wait()
        stage(kbuf, vbuf)
        return j + 1, flag[0]

    lax.while_loop(cond, body, (jnp.int32(3), flag[0]))
    y_ref[...] = (acc_scr[...] * _silu(z_ref[...])).astype(y_ref.dtype)


def _sb(p5, kprev, vprev, tq, causal_prev, name):
    _, bx, t, w = p5.shape
    nh = H_C
    tk = SB_TILE if not causal_prev else tq

    def blk(group):
        return pl.BlockSpec((None, None, tq, w), lambda b, i, g=group: (g, b, i, 0))

    if causal_prev:
        def near(group, back):
            return pl.BlockSpec((None, None, tk, w), lambda b, i, g=group: (g, b, jnp.maximum(i - back, 0), 0))
        kprev, vprev = p5, p5
        near_specs = [near(1, 1), near(2, 1), near(1, 2), near(2, 2)]
        buf_shape = (tk, w)
        per_step, fixed = tq // tk, 0
    else:
        nblk = kprev.shape[1] // (tk * nh)

        def near(back):
            return pl.BlockSpec((None, tk * nh, DK), lambda b, i: (b, max(nblk - back, 0), 0))
        near_specs = [near(1), near(1), near(2), near(2)]
        buf_shape = (tk * nh, DK)
        per_step, fixed = 0, nblk
    kern = functools.partial(_sb_kernel, tq=tq, tk=tk, prev_blocks_per_step=per_step, prev_blocks_fixed=fixed,
                             prev_token_rows=not causal_prev, kgroup=1, vgroup=2)
    rows_out = pl.BlockSpec((None, tq * nh, DK), lambda b, i: (b, i, 0))
    return pl.pallas_call(
        kern,
        grid=(bx, t // tq),
        in_specs=[blk(0), blk(3), blk(1), blk(2)] + near_specs
                 + [pl.BlockSpec(memory_space=pl.ANY), pl.BlockSpec(memory_space=pl.ANY)],
        out_specs=[pl.BlockSpec((None, tq, w), lambda b, i: (b, i, 0)), rows_out, rows_out],
        out_shape=[jax.ShapeDtypeStruct((bx, t, w), BF16),
                   jax.ShapeDtypeStruct((bx, t * nh, DK), F32),
                   jax.ShapeDtypeStruct((bx, t * nh, DK), F32)],
        scratch_shapes=[pltpu.VMEM((tq, w), F32), pltpu.VMEM((nh, tq, LANES), F32), pltpu.VMEM((tq, w), BF16),
                        pltpu.VMEM(buf_shape, F32), pltpu.VMEM(buf_shape, F32),
                        pltpu.SMEM((1,), jnp.int32), pltpu.SemaphoreType.DMA((2,))],
        compiler_params=_cparams(("parallel", "arbitrary")),
        name=name,
    )(p5, p5, p5, p5, kprev, vprev, kprev, vprev, kprev, vprev)


def _s5_kernel(u_ref, z_ref, bb_ref, cc_ref, ar_ref, ai_ref, d_ref, wg_ref, bg_ref, h0r_ref, h0i_ref,
               y_ref, hr_out, hi_out, hs, ys, hr_scr, hi_scr, *, tt):
    t = pl.program_id(1)
    half = GB_D * P_D
    nl = half // LANES

    @pl.when(t == 0)
    def _():
        hr_scr[...] = h0r_ref[...]
        hi_scr[...] = h0i_ref[...]

    for gb in range(NGB_D):
        rows = pl.ds(gb, tt, stride=NGB_D)
        bu = _dot(u_ref[:, gb * LANES:(gb + 1) * LANES].astype(BF16), bb_ref[gb])
        for j in range(2 * nl):
            hs[j, rows, :] = bu[:, j * LANES:(j + 1) * LANES]

    ar = [ar_ref[:, j * LANES:(j + 1) * LANES] for j in range(nl)]
    ai = [ai_ref[:, j * LANES:(j + 1) * LANES] for j in range(nl)]

    def step(s, carry):
        rows = pl.ds(pl.multiple_of(s * NGB_D, NGB_D), NGB_D)
        new = [None] * (2 * nl)
        for j in range(nl):
            hr, hi = carry[j], carry[nl + j]
            new[j] = ar[j] * hr - ai[j] * hi + hs[j, rows, :]
            new[nl + j] = ar[j] * hi + ai[j] * hr + hs[nl + j, rows, :]
        for j in range(2 * nl):
            hs[j, rows, :] = new[j]
        return tuple(new)

    init = tuple(hr_scr[:, j * LANES:(j + 1) * LANES] for j in range(nl)) + tuple(
        hi_scr[:, j * LANES:(j + 1) * LANES] for j in range(nl))
    fin = lax.fori_loop(0, tt, step, init, unroll=8)
    hr = jnp.concatenate(fin[:nl], axis=1)
    hi = jnp.concatenate(fin[nl:], axis=1)
    hr_scr[...] = hr
    hi_scr[...] = hi

    for gb in range(NGB_D):
        rows = pl.ds(gb, tt, stride=NGB_D)
        hcat = jnp.concatenate([hs[j, rows, :] for j in range(2 * nl)], axis=1)
        lanes = slice(gb * LANES, (gb + 1) * LANES)
        ys[:, lanes] = _dot(hcat.astype(BF16), cc_ref[gb]) + d_ref[:, lanes] * u_ref[:, lanes]

    y = ys[...]
    yg = 0.5 * y * (1.0 + jnp.tanh(0.7978845608028654 * (y + 0.044715 * (y * y * y))))
    gate = _sigmoid(_dot(yg.astype(BF16), wg_ref[...]) + bg_ref[...])
    y_ref[...] = (yg * gate * _silu(z_ref[...])).astype(y_ref.dtype)

    @pl.when(t == pl.num_programs(1) - 1)
    def _():
        hr_out[...] = hr
        hi_out[...] = hi


def _s5(p5, ugroup, zgroup, bb, cc, a_re, a_im, d, w_glu_bf16, b_glu, h0r, h0i, tt, name):
    _, bx, t, w = p5.shape
    half = GB_D * P_D
    st = pl.BlockSpec((None, NGB_D, half), lambda b, i: (b, 0, 0))
    prm = pl.BlockSpec((NGB_D, half), lambda b, i: (0, 0))
    row = pl.BlockSpec((1, w), lambda b, i: (0, 0))
    return pl.pallas_call(
        functools.partial(_s5_kernel, tt=tt),
        grid=(bx, t // tt),
        in_specs=[pl.BlockSpec((None, None, tt, w), lambda b, i: (ugroup, b, i, 0)),
                  pl.BlockSpec((None, None, tt, w), lambda b, i: (zgroup, b, i, 0)),
                  pl.BlockSpec((NGB_D, LANES, 2 * half), lambda b, i: (0, 0, 0)),
                  pl.BlockSpec((NGB_D, 2 * half, LANES), lambda b, i: (0, 0, 0)),
                  prm, prm, row,
                  pl.BlockSpec((w, w), lambda b, i: (0, 0)),
                  row, st, st],
        out_specs=[pl.BlockSpec((None, tt, w), lambda b, i: (b, i, 0)), st, st],
        out_shape=[jax.ShapeDtypeStruct((bx, t, w), BF16),
                   jax.ShapeDtypeStruct((bx, NGB_D, half), F32),
                   jax.ShapeDtypeStruct((bx, NGB_D, half), F32)],
        scratch_shapes=[pltpu.VMEM((2 * half // LANES, tt * NGB_D, LANES), F32),
                        pltpu.VMEM((tt, w), F32),
                        pltpu.VMEM((NGB_D, half), F32), pltpu.VMEM((NGB_D, half), F32)],
        compiler_params=_cparams(("parallel", "arbitrary")),
        name=name,
    )(p5, p5, bb, cc, a_re, a_im, d.reshape(1, w), w_glu_bf16, b_glu.reshape(1, w), h0r, h0i)


def _pack_even_weights(w_in, conv_w, a_log, dt_bias, norm_a, ig_bias, fg_bias, norm_b, w_out, ln_g, ln_b):
    d = w_in.shape[0]
    wa = H_A * DK
    wb = H_B * DK
    o = 0
    qkv_a = w_in[:, o:o + 3 * wa]; o += 3 * wa
    z_a = w_in[:, o:o + wa]; o += wa
    gate_a = w_in[:, o:o + 2 * H_A]; o += 2 * H_A
    main_b = w_in[:, o:o + 5 * wb]; o += 5 * wb
    gate_b = w_in[:, o:o + 2 * H_B]
    w_main = jnp.concatenate([qkv_a, z_a, main_b], axis=1).astype(BF16)
    n_gate = 2 * H_A + 2 * H_B
    w_gate = jnp.concatenate([gate_a, gate_b, jnp.zeros((d, LANES - n_gate), F32)], axis=1).astype(BF16)
    prm = jnp.zeros((SUBLANES, LANES), F32)
    prm = prm.at[0, H_A:2 * H_A].set(a_log)
    prm = prm.at[1, H_A:2 * H_A].set(dt_bias)
    prm = prm.at[2, 2 * H_A:2 * H_A + H_B].set(ig_bias)
    prm = prm.at[3, 2 * H_A + H_B:2 * H_A + 2 * H_B].set(fg_bias)
    return dict(w_main=w_main, w_gate=w_gate, conv_w=conv_w, prm=prm, norm_a=norm_a.reshape(1, DK),
                norm_b=norm_b.reshape(1, wb), w_out=w_out.astype(BF16), ln_g=ln_g, ln_b=ln_b)


def _tile(n, preferred):
    tile = min(n, preferred)
    assert n % tile == 0, (n, preferred)
    return tile


def _even_layer(x, s0, conv0, c0, n0, m0, wp, alpha):
    bx, t, d = x.shape
    m = bx * t
    chunk = min(CHUNK, t)
    tm = _tile(m, PROJ_ROWS)
    tb = _tile(t, MIXER_ROWS)
    x2 = x.reshape(m, d)
    p3, g2 = _proj(x2, wp["w_main"], wp["w_gate"], tm, "ev_proj")
    p5 = p3.reshape(9, bx, t, W_HEADS)
    g3 = g2.reshape(bx, t, LANES)
    conv0p = jnp.concatenate([jnp.zeros((bx, SUBLANES - (CONV_W - 1), conv0.shape[2]), F32), conv0], axis=1)
    ya, s_new = _gdn(p5, g3, conv0p, wp["conv_w"], wp["prm"], wp["norm_a"], s0, chunk, tb, "gdn")
    m0w = jnp.broadcast_to(m0[:, :, None, None], (bx, H_B, 1, DK))
    n0w = jnp.broadcast_to(n0[:, :, :, None], (bx, H_B, DK, DK))
    yb, c_new, n_new, m_new = _mlstm(p5, g3, wp["prm"], wp["norm_b"], c0, n0w, m0w, chunk, tb, "mlstm")
    xn = _outproj_ln(ya.reshape(m, -1), yb.reshape(m, -1), x2, wp["w_out"], wp["ln_g"], wp["ln_b"], alpha,
                     _tile(m, OUTPROJ_ROWS), "ev_outproj_ln")
    conv_new = jnp.concatenate([p5[0, :, t - (CONV_W - 1):], p5[1, :, t - (CONV_W - 1):], p5[2, :, t - (CONV_W - 1):]],
                               axis=-1)
    return xn.reshape(bx, t, d), s_new, conv_new, c_new, n_new[:, :, :, 0], m_new[:, :, 0, 0]


def _pack_odd_weights(w_in, lam_re, lam_im, b_re, b_im, c_re, c_im, d, log_dt, w_glu, b_glu, w_out, ln_g, ln_b):
    dm = w_in.shape[0]
    w_main = w_in.astype(BF16)
    dt = jnp.exp(log_dt)[:, None]
    mag = jnp.exp(lam_re * dt)
    a_re, a_im = mag * jnp.cos(lam_im * dt), mag * jnp.sin(lam_im * dt)
    den = lam_re * lam_re + lam_im * lam_im
    f_re = ((a_re - 1.0) * lam_re + a_im * lam_im) / den
    f_im = (a_im * lam_re - (a_re - 1.0) * lam_im) / den
    bb_re = f_re[..., None] * b_re - f_im[..., None] * b_im
    bb_im = f_re[..., None] * b_im + f_im[..., None] * b_re
    half = GB_D * P_D
    eye = jnp.eye(GB_D, dtype=F32)

    def blockdiag_in(bb):
        bbg = bb.reshape(NGB_D, GB_D, P_D, GC_D)
        return jnp.einsum("bgpc,gh->bgchp", bbg, eye).reshape(NGB_D, GB_D * GC_D, half)

    def blockdiag_out(cm):
        cg = cm.reshape(NGB_D, GB_D, GC_D, P_D)
        return jnp.einsum("bgcp,gh->bgphc", cg, eye).reshape(NGB_D, half, GB_D * GC_D)

    bb = jnp.concatenate([blockdiag_in(bb_re), blockdiag_in(bb_im)], axis=2).astype(BF16)
    cc = jnp.concatenate([blockdiag_out(c_re), -blockdiag_out(c_im)], axis=1).astype(BF16)
    return dict(w_main=w_main, bb=bb, cc=cc, a_re=a_re.reshape(NGB_D, half), a_im=a_im.reshape(NGB_D, half),
                d=d, w_glu=w_glu.astype(BF16), b_glu=b_glu, w_out=w_out.astype(BF16), ln_g=ln_g, ln_b=ln_b)


def _odd_layer(x, k_cache, v_cache, h_re0, h_im0, wp, alpha):
    bx, t, d = x.shape
    m = bx * t
    tm = _tile(m, PROJ_ROWS)
    tq = _tile(t, SB_TILE)
    tt = _tile(t, S5_ROWS)
    x2 = x.reshape(m, d)
    p5 = _proj(x2, wp["w_main"], None, tm, "od_proj").reshape(6, bx, t, W_HEADS)
    if k_cache is None:
        yc, kk, vv = _sb(p5, None, None, tq, True, "sb_prompt")
    else:
        past = k_cache.shape[1]
        yc, kk, vv = _sb(p5, k_cache.reshape(bx, past * H_C, DK), v_cache.reshape(bx, past * H_C, DK), tq, False,
                         "sb_sample")
    half = GB_D * P_D
    yd, hr, hi = _s5(p5, 4, 5, wp["bb"], wp["cc"], wp["a_re"], wp["a_im"], wp["d"], wp["w_glu"], wp["b_glu"],
                     h_re0.reshape(bx, NGB_D, half), h_im0.reshape(bx, NGB_D, half), tt, "s5_glu")
    xn = _outproj_ln(yc.reshape(m, -1), yd.reshape(m, -1), x2, wp["w_out"], wp["ln_g"], wp["ln_b"], alpha,
                     _tile(m, OUTPROJ_ROWS), "od_outproj_ln")
    return (xn.reshape(bx, t, d), kk.reshape(bx, t, H_C, DK), vv.reshape(bx, t, H_C, DK),
            hr.reshape(bx, G_D, P_D), hi.reshape(bx, G_D, P_D))


def kernel(x_prompt, x_sample, state_delta_S, state_delta_conv, state_mlstm_C, state_mlstm_n, state_mlstm_m, cache_sb_k, cache_sb_v, state_s5_re, state_s5_im, ev_w_in, ev_conv_w, ev_a_log, ev_dt_bias, ev_norm_a, ev_ig_bias, ev_fg_bias, ev_norm_b, ev_w_out, ev_ln_g, ev_ln_b, od_w_in, od_lam_re, od_lam_im, od_b_re, od_b_im, od_c_re, od_c_im, od_d, od_log_dt, od_w_glu, od_b_glu, od_w_out, od_ln_g, od_ln_b):
    n_even = ev_w_in.shape[0]
    n_odd = od_w_in.shape[0]
    depth = n_even + n_odd
    alpha = (2 * depth) ** 0.25
    nb, seq, _ = x_prompt.shape
    ns, dseq, _ = x_sample.shape
    xp, xs = x_prompt, x_sample
    outs = [[] for _ in range(18)]
    for layer in range(depth):
        j = layer // 2
        if layer % 2 == 0:
            wp = _pack_even_weights(ev_w_in[j], ev_conv_w[j], ev_a_log[j], ev_dt_bias[j], ev_norm_a[j], ev_ig_bias[j],
                                    ev_fg_bias[j], ev_norm_b[j], ev_w_out[j], ev_ln_g[j], ev_ln_b[j])
            zs = jnp.zeros((nb, H_A, DK, DK), F32)
            xp, s, cv, c, n, m = _even_layer(
                xp, zs, jnp.zeros((nb, CONV_W - 1, 3 * H_A * DK), F32), zs, jnp.zeros((nb, H_B, DK), F32),
                jnp.zeros((nb, H_B), F32), wp, alpha)
            for idx, val in zip((0, 2, 4, 6, 8), (s, cv, c, n, m)):
                outs[idx].append(val)
            xs, s, cv, c, n, m = _even_layer(
                xs, state_delta_S[j], state_delta_conv[j], state_mlstm_C[j], state_mlstm_n[j], state_mlstm_m[j],
                wp, alpha)
            for idx, val in zip((1, 3, 5, 7, 9), (s, cv, c, n, m)):
                outs[idx].append(val)
        else:
            wp = _pack_odd_weights(od_w_in[j], od_lam_re[j], od_lam_im[j], od_b_re[j], od_b_im[j], od_c_re[j],
                                   od_c_im[j], od_d[j], od_log_dt[j], od_w_glu[j], od_b_glu[j], od_w_out[j],
                                   od_ln_g[j], od_ln_b[j])
            zst = jnp.zeros((nb, G_D, P_D), F32)
            xp, k, v, hr, hi = _odd_layer(xp, None, None, zst, zst, wp, alpha)
            for idx, val in zip((10, 12, 14, 16), (k, v, hr, hi)):
                outs[idx].append(val)
            xs, k, v, hr, hi = _odd_layer(xs, cache_sb_k[j], cache_sb_v[j], state_s5_re[j], state_s5_im[j], wp, alpha)
            for idx, val in zip((11, 13, 15, 17), (k, v, hr, hi)):
                outs[idx].append(val)
    return (xp, xs) + tuple(jnp.stack(o) for o in outs)
```
